```python
import jax
import jax.numpy as jnp
from jax import lax

D_MODEL = 1024
BATCH = 8
SEQ = 8192
DEPTH = 2

GLA_HEADS = 4
GLA_DK = 64
GLA_DV = 128
GLA_RANK = 16
GLA_TAU = 16.0
GLA_CHUNK = 64
SWA_HEADS = 8
SWA_KV_HEADS = 2
SWA_HD = 64
WINDOW = 128
ROPE_DIMS = SWA_HD // 4
ROPE_THETA = 500000.0
N_EXPERTS = 16
N_GROUPS = 4
EXPERTS_PER_GROUP = N_EXPERTS // N_GROUPS
TOP_K = 2
D_FF = 512
MOE_BLOCK = 128
EPS = 1e-6
POS_OFFSET_MAX = 1024
IN_WIDTH = 2 * GLA_HEADS * GLA_DK + 2 * GLA_HEADS * GLA_DV + GLA_RANK + SWA_HEADS * SWA_HD + 2 * SWA_KV_HEADS * SWA_HD + 2 * D_MODEL

kernel_name = "hybrid_gla_swa_sink_grouped_moe_adaln"


def rms_norm(x, g):
    xf = x.astype(jnp.float32)
    y = xf * lax.rsqrt(jnp.mean(xf * xf, axis=-1, keepdims=True) + EPS)
    return (y * g).astype(x.dtype)


def partial_rope(x, positions):
    half = ROPE_DIMS // 2
    inv_freq = jnp.power(ROPE_THETA, -jnp.arange(half, dtype=jnp.float32) / half)
    ang = positions.astype(jnp.float32)[..., None] * inv_freq
    cos = jnp.cos(ang)[:, :, None, :]
    sin = jnp.sin(ang)[:, :, None, :]
    xr = x[..., :ROPE_DIMS].astype(jnp.float32)
    x1, x2 = xr[..., :half], xr[..., half:]
    rot = jnp.concatenate([x1 * cos - x2 * sin, x2 * cos + x1 * sin], axis=-1)
    return jnp.concatenate([rot.astype(x.dtype), x[..., ROPE_DIMS:]], axis=-1)


def split_columns(z):
    sizes = (GLA_HEADS * GLA_DK, GLA_HEADS * GLA_DK, GLA_HEADS * GLA_DV, GLA_RANK, GLA_HEADS * GLA_DV,
             SWA_HEADS * SWA_HD, SWA_KV_HEADS * SWA_HD, SWA_KV_HEADS * SWA_HD, D_MODEL, D_MODEL)
    idx, acc = [], 0
    for s in sizes[:-1]:
        acc += s
        idx.append(acc)
    return jnp.split(z, idx, axis=-1)


def gla_branch(q, k, v, a_low, r, w_alpha2, b_alpha, gn_gain):
    B, S, _ = q.shape
    N = S // GLA_CHUNK
    f32 = jnp.float32
    shp_k = (B, N, GLA_CHUNK, GLA_HEADS, GLA_DK)
    shp_v = (B, N, GLA_CHUNK, GLA_HEADS, GLA_DV)
    log_a = jax.nn.log_sigmoid((a_low @ w_alpha2 + b_alpha).astype(f32)) / GLA_TAU
    b = jnp.cumsum(log_a.reshape(shp_k), axis=2)
    q = q.astype(f32).reshape(shp_k) * (GLA_DK ** -0.5)
    k = k.astype(f32).reshape(shp_k)
    v = v.astype(f32).reshape(shp_v)
    q_dec = q * jnp.exp(b)
    k_inv = k * jnp.exp(-b)
    causal = jnp.tril(jnp.ones((GLA_CHUNK, GLA_CHUNK), dtype=bool))
    att = jnp.where(causal, jnp.einsum('bnihd,bnjhd->bnhij', q_dec, k_inv), 0.0)
    o_intra = jnp.einsum('bnhij,bnjhe->bnihe', att, v)
    b_end = b[:, :, -1]
    k_end = k * jnp.exp(b_end[:, :, None] - b)
    kv = jnp.einsum('bnjhd,bnjhe->nbhde', k_end, v)

    def step(state, inp):
        decay, kv_n = inp
        return decay[..., None] * state + kv_n, state

    s0 = jnp.zeros((B, GLA_HEADS, GLA_DK, GLA_DV), f32)
    _, s_prev = lax.scan(step, s0, (jnp.exp(b_end).transpose(1, 0, 2, 3), kv))
    o_inter = jnp.einsum('bnihd,nbhde->bnihe', q_dec, s_prev)
    o = (o_intra + o_inter).reshape(B, S, GLA_HEADS, GLA_DV)
    o = rms_norm(o, gn_gain.reshape(GLA_HEADS, GLA_DV))
    return (o.reshape(B, S, GLA_HEADS * GLA_DV) * jax.nn.silu(r.astype(f32))).astype(r.dtype)


def swa_branch(q, k, v, positions, qn_gain, kn_gain, sinks):
    B, S, _ = q.shape
    N = S // WINDOW
    G = SWA_HEADS // SWA_KV_HEADS
    q = partial_rope(rms_norm(q.reshape(B, S, SWA_HEADS, SWA_HD), qn_gain), positions)
    k = partial_rope(rms_norm(k.reshape(B, S, SWA_KV_HEADS, SWA_HD), kn_gain), positions)
    v = v.reshape(B, S, SWA_KV_HEADS, SWA_HD)
    qb = q.reshape(B, N, WINDOW, SWA_KV_HEADS, G, SWA_HD)
    kb = k.reshape(B, N, WINDOW, SWA_KV_HEADS, SWA_HD)
    vb = v.reshape(B, N, WINDOW, SWA_KV_HEADS, SWA_HD)
    pad = ((0, 0), (1, 0), (0, 0), (0, 0), (0, 0))
    k_band = jnp.concatenate([jnp.pad(kb[:, :-1], pad), kb], axis=2)
    v_band = jnp.concatenate([jnp.pad(vb[:, :-1], pad), vb], axis=2)
    s = jnp.einsum('bnikgd,bnjkd->bnkgij', qb, k_band).astype(jnp.float32) * (SWA_HD ** -0.5)
    i = jnp.arange(WINDOW)[:, None]
    j = jnp.arange(2 * WINDOW)[None, :]
    band = (j > i) & (j <= i + WINDOW)
    valid = band[None] & ((jnp.arange(N)[:, None, None] > 0) | (j >= WINDOW)[None])
    s = jnp.where(valid[None, :, None, None], s, jnp.float32(-1e30))
    sink = sinks.astype(jnp.float32).reshape(SWA_KV_HEADS, G)[None, None, :, :, None, None]
    m = jnp.maximum(jnp.max(s, axis=-1, keepdims=True), sink)
    p = jnp.exp(s - m)
    p = p / (jnp.sum(p, axis=-1, keepdims=True) + jnp.exp(sink - m))
    o = jnp.einsum('bnkgij,bnjkd->bnikgd', p.astype(v.dtype), v_band)
    return o.reshape(B, S, SWA_HEADS * SWA_HD)


def group_limited_route(h, w_router, b_router):
    aff = jax.nn.sigmoid((h @ w_router).astype(jnp.float32))
    sel = (aff + b_router.astype(jnp.float32)).reshape(-1, N_GROUPS, EXPERTS_PER_GROUP)
    group_score = lax.top_k(sel, TOP_K)[0].sum(-1)
    group = jnp.argmax(group_score, axis=-1)
    in_group = jnp.take_along_axis(sel, group[:, None, None], axis=1)[:, 0]
    _, local = lax.top_k(in_group, TOP_K)
    expert = (group[:, None] * EXPERTS_PER_GROUP + local).astype(jnp.int32)
    w = jnp.take_along_axis(aff, expert, axis=1)
    return expert, w / jnp.sum(w, axis=-1, keepdims=True)


def moe_ffn(h, w_router, b_router, w_gate, w_up, w_down):
    T, D = h.shape
    expert, weight = group_limited_route(h, w_router, b_router)
    A = T * TOP_K
    e_flat = expert.reshape(A)
    tok_flat = jnp.repeat(jnp.arange(T, dtype=jnp.int32), TOP_K)
    order = jnp.argsort(e_flat)
    e_s = e_flat[order]
    tok_s = tok_flat[order]
    w_s = weight.reshape(A)[order]
    counts = jnp.bincount(e_flat, length=N_EXPERTS)
    starts = jnp.cumsum(counts) - counts
    padded = (counts + MOE_BLOCK - 1) // MOE_BLOCK * MOE_BLOCK
    pad_end = jnp.cumsum(padded)
    dest = pad_end[e_s] - padded[e_s] + jnp.arange(A, dtype=jnp.int32) - starts[e_s]
    n_blocks = -(-A // MOE_BLOCK) + N_EXPERTS
    slots = n_blocks * MOE_BLOCK
    tok_buf = jnp.zeros((slots,), jnp.int32).at[dest].set(tok_s)
    w_buf = jnp.zeros((slots,), jnp.float32).at[dest].set(w_s)
    blk_start = jnp.arange(n_blocks, dtype=jnp.int32) * MOE_BLOCK
    blk_exp = jnp.minimum(jnp.searchsorted(pad_end, blk_start, side='right'), N_EXPERTS - 1)

    def expert_block(args):
        tok, e = args
        xb = h[tok]
        return (jax.nn.silu(xb @ w_gate[e]) * (xb @ w_up[e])) @ w_down[e]

    y = lax.map(expert_block, (tok_buf.reshape(n_blocks, MOE_BLOCK), blk_exp))
    contrib = y.reshape(slots, D) * w_buf[:, None].astype(h.dtype)
    return jnp.zeros_like(h).at[tok_buf].add(contrib)


def setup_inputs(seed: int = 0) -> dict:
    key = jax.random.key(seed)
    ks = jax.random.split(key, 24)
    f32 = jnp.float32
    L, D = DEPTH, D_MODEL

    def nrm(k, shape, scale):
        return jax.random.normal(k, shape, f32) * scale

    offsets = jax.random.randint(ks[2], (BATCH, 1), 0, POS_OFFSET_MAX, dtype=jnp.int32)
    return {
        'x': nrm(ks[0], (BATCH, SEQ, D), 1.0),
        'c': nrm(ks[1], (BATCH, D), 1.0),
        'positions': offsets + jnp.arange(SEQ, dtype=jnp.int32)[None, :],
        'w_ada': nrm(ks[3], (L, D, 6 * D), 0.5 * D ** -0.5),
        'b_ada': nrm(ks[4], (L, 6 * D), 0.02),
        'norm_mix': 1.0 + nrm(ks[5], (L, D), 0.05),
        'w_in': nrm(ks[6], (L, D, IN_WIDTH), D ** -0.5),
        'w_alpha2': nrm(ks[7], (L, GLA_RANK, GLA_HEADS * GLA_DK), GLA_RANK ** -0.5),
        'b_alpha': nrm(ks[8], (L, GLA_HEADS * GLA_DK), 0.02),
        'gla_norm': 1.0 + nrm(ks[9], (L, GLA_HEADS * GLA_DV), 0.05),
        'q_norm': 1.0 + nrm(ks[10], (L, SWA_HD), 0.05),
        'k_norm': 1.0 + nrm(ks[11], (L, SWA_HD), 0.05),
        'sinks': nrm(ks[12], (L, SWA_HEADS), 0.5),
        'w_up_gla': nrm(ks[13], (L, GLA_HEADS * GLA_DV, D), (GLA_HEADS * GLA_DV) ** -0.5),
        'w_up_swa': nrm(ks[14], (L, SWA_HEADS * SWA_HD, D), (SWA_HEADS * SWA_HD) ** -0.5),
        'w_out': nrm(ks[15], (L, D, D), D ** -0.5),
        'norm_ffn': 1.0 + nrm(ks[16], (L, D), 0.05),
        'w_router': nrm(ks[17], (D, N_EXPERTS), D ** -0.5),
        'b_router': nrm(ks[18], (N_EXPERTS,), 0.01),
        'w_gate': nrm(ks[19], (L, N_EXPERTS, D, D_FF), D ** -0.5),
        'w_up': nrm(ks[20], (L, N_EXPERTS, D, D_FF), D ** -0.5),
        'w_down': nrm(ks[21], (L, N_EXPERTS, D_FF, D), D_FF ** -0.5),
    }


def reference(x, c, positions, w_ada, b_ada, norm_mix, w_in, w_alpha2, b_alpha, gla_norm,
              q_norm, k_norm, sinks, w_up_gla, w_up_swa, w_out, norm_ffn, w_router, b_router,
              w_gate, w_up, w_down):
    B, S, D = x.shape
    cond = jax.nn.silu(c)
    for l in range(DEPTH):
        mod = cond @ w_ada[l] + b_ada[l]
        sh1, sc1, g1, sh2, sc2, g2 = [m[:, None, :] for m in jnp.split(mod, 6, axis=-1)]
        h = rms_norm(x, norm_mix[l]) * (1.0 + sc1) + sh1
        gq, gk, gv, ga, gr, sq, sk, sv, mix_a, mix_b = split_columns(h @ w_in[l])
        y_gla = gla_branch(gq, gk, gv, ga, gr, w_alpha2[l], b_alpha[l], gla_norm[l]) @ w_up_gla[l]
        y_swa = swa_branch(sq, sk, sv, positions, q_norm[l], k_norm[l], sinks[l]) @ w_up_swa[l]
        merged = jax.nn.sigmoid(mix_a) * y_gla + jax.nn.sigmoid(mix_b) * y_swa
        x = x + g1 * (merged @ w_out[l])
        h2 = rms_norm(x, norm_ffn[l]) * (1.0 + sc2) + sh2
        y = moe_ffn(h2.reshape(B * S, D), w_router, b_router, w_gate[l], w_up[l], w_down[l])
        x = x + g2 * y.reshape(B, S, D)
    return x
```

```python
import functools

import jax
import jax.numpy as jnp
from jax import lax
from jax.experimental import pallas as pl
from jax.experimental.pallas import tpu as pltpu

F32 = jnp.float32
BF16 = jnp.bfloat16
I32 = jnp.int32
HIGHEST = lax.Precision.HIGHEST

GLA_HEADS = 4
GLA_DK = 64
GLA_DV = 128
GLA_RANK = 16
GLA_TAU = 16.0
GLA_CHUNK = 64
SWA_HEADS = 8
SWA_KV_HEADS = 2
SWA_HD = 64
WINDOW = 128
ROPE_DIMS = SWA_HD // 4
ROPE_THETA = 500000.0
N_EXPERTS = 16
N_GROUPS = 4
EXPERTS_PER_GROUP = 4
D_FF = 512
EPS = 1e-6

LANES = 128
SUBLANES = 8
VMEM_LIMIT = 56 * 1024 * 1024

PAIRS = ((0, 1), (0, 2), (0, 3), (1, 2), (1, 3), (2, 3))
N_CLASSES = N_GROUPS * len(PAIRS)
CLASS_ROWS = 32
MOE_ROWS = 256
WCOLS = LANES


def _params(sem, vmem=None):
    return pltpu.CompilerParams(dimension_semantics=sem, vmem_limit_bytes=vmem)


def _dot(a, b):
    return jnp.dot(a, b, preferred_element_type=F32)


def _iota(shape, axis):
    return lax.broadcasted_iota(I32, shape, axis)


def _mod_kernel(c_ref, w_ref, b_ref, o_ref):
    c = c_ref[...]
    cond = c * jax.nn.sigmoid(c)
    o_ref[...] = jnp.dot(cond, w_ref[...], precision=HIGHEST,
                         preferred_element_type=F32) + b_ref[...]


def _modulation(c, w_ada, b_ada):
    L, D, D6 = w_ada.shape
    B = c.shape[0]
    nj = D6 // D
    return pl.pallas_call(
        _mod_kernel,
        grid=(L, nj),
        in_specs=[
            pl.BlockSpec((B, D), lambda l, j: (0, 0)),
            pl.BlockSpec((None, D, D), lambda l, j: (l, 0, j)),
            pl.BlockSpec((None, 1, D), lambda l, j: (l, 0, j)),
        ],
        out_specs=pl.BlockSpec((None, B, D), lambda l, j: (l, 0, j)),
        out_shape=jax.ShapeDtypeStruct((L, B, D6), F32),
        compiler_params=_params(("arbitrary", "arbitrary")),
        name="modulation",
    )(c, w_ada, b_ada.reshape(L, 1, D6))


def _rope_kernel(pos_ref, inv_ref, c_ref, s_ref):
    ang = pos_ref[...].astype(F32) * inv_ref[...]
    l64 = _iota(ang.shape, 1) & (SWA_HD - 1)
    half = ROPE_DIMS // 2
    cos = jnp.cos(ang)
    sin = jnp.sin(ang)
    c_ref[...] = jnp.where(l64 < ROPE_DIMS, cos, 1.0)
    s_ref[...] = jnp.where(l64 < half, -sin, jnp.where(l64 < ROPE_DIMS, sin, 0.0))


def _rope_tables(positions):
    T = positions.size
    half = ROPE_DIMS // 2
    inv_freq = jnp.power(ROPE_THETA, -jnp.arange(half, dtype=F32) / half)
    lane = jnp.arange(LANES)
    inv_lane = inv_freq[(lane % SWA_HD) % half].reshape(1, LANES)
    tm = min(1024, T)
    return pl.pallas_call(
        _rope_kernel,
        grid=(T // tm,),
        in_specs=[pl.BlockSpec((tm, 1), lambda i: (i, 0)),
                  pl.BlockSpec((1, LANES), lambda i: (0, 0))],
        out_specs=[pl.BlockSpec((tm, LANES), lambda i: (i, 0))] * 2,
        out_shape=[jax.ShapeDtypeStruct((T, LANES), F32)] * 2,
        compiler_params=_params(("arbitrary",)),
        name="rope_tables",
    )(positions.reshape(T, 1), inv_lane)


_GQ, _GK, _GV, _GR = 0, 256, 512, 1024
_SQ, _SK, _SV = 1536, 2048, 2176
_MA, _MB, _GA, _WIN = 2304, 3328, 4352, 4480


def _premix_kernel(x_ref, gain_ref, sc_ref, sh_ref, w_ref, wa2_ref, ba_ref,
                   gq_ref, gk_ref, gv_ref, la_ref, gr_ref, sq_ref, sk_ref, sv_ref,
                   ma_ref, mb_ref):
    x = x_ref[...]
    ms = jnp.mean(x * x, axis=-1, keepdims=True)
    h = x * lax.rsqrt(ms + EPS) * gain_ref[...]
    h = h * (1.0 + sc_ref[...]) + sh_ref[...]
    hb = h.astype(BF16)

    def sec(lo, hi):
        return _dot(hb, w_ref[:, lo:hi])

    gq_ref[...] = (sec(_GQ, _GK) * (GLA_DK ** -0.5)).astype(BF16)
    gk_ref[...] = sec(_GK, _GV).astype(BF16)
    gv_ref[...] = sec(_GV, _GR).astype(BF16)
    r = sec(_GR, _SQ)
    gr_ref[...] = (r * jax.nn.sigmoid(r)).astype(BF16)
    sq_ref[...] = sec(_SQ, _SK).astype(BF16)
    sk_ref[...] = sec(_SK, _SV).astype(BF16)
    sv_ref[...] = sec(_SV, _MA).astype(BF16)
    ma_ref[...] = jax.nn.sigmoid(sec(_MA, _MB)).astype(BF16)
    mb_ref[...] = jax.nn.sigmoid(sec(_MB, _GA)).astype(BF16)
    a_low = sec(_GA, _WIN).astype(BF16)
    z = _dot(a_low, wa2_ref[...]) + ba_ref[...]
    log_sig = jnp.minimum(z, 0.0) - jnp.log1p(jnp.exp(-jnp.abs(z)))
    la_ref[...] = log_sig * (1.0 / GLA_TAU)


def _premix(x, gain, sc, sh, w_in_p, wa2_p, b_alpha, seq, tm):
    T, D = x.shape
    per_b = seq // tm
    tok = lambda i: (i, 0)
    row = lambda i: (0, 0)
    bat = lambda i: (i // per_b, 0, 0)
    widths = (256, 256, 512, 256, 512, 512, 128, 128, 1024, 1024)
    dts = (BF16, BF16, BF16, F32, BF16, BF16, BF16, BF16, BF16, BF16)
    return pl.pallas_call(
        _premix_kernel,
        grid=(T // tm,),
        in_specs=[
            pl.BlockSpec((tm, D), tok),
            pl.BlockSpec((1, D), row),
            pl.BlockSpec((None, 1, D), bat),
            pl.BlockSpec((None, 1, D), bat),
            pl.BlockSpec((D, _WIN), row),
            pl.BlockSpec((LANES, 256), row),
            pl.BlockSpec((1, 256), row),
        ],
        out_specs=[pl.BlockSpec((tm, w), tok) for w in widths],
        out_shape=[jax.ShapeDtypeStruct((T, w), dt) for w, dt in zip(widths, dts)],
        compiler_params=_params(("arbitrary",), VMEM_LIMIT),
        name="premix",
    )(x, gain, sc, sh, w_in_p, wa2_p, b_alpha)


def _gla_kernel(q_ref, k_ref, v_ref, la_ref, rs_ref, gn_ref, o_ref, s_ref, *, nchunk):
    C = GLA_CHUNK
    HK = GLA_HEADS * GLA_DK
    HV = GLA_HEADS * GLA_DV

    @pl.when(pl.program_id(1) == 0)
    def _():
        s_ref[...] = jnp.zeros_like(s_ref)

    tri = (_iota((C, C), 1) <= _iota((C, C), 0)).astype(F32)
    bdk = (_iota((HK, HK), 0) >> 6) == (_iota((HK, HK), 1) >> 6)
    bdv = (_iota((HK, HV), 0) >> 6) == (_iota((HK, HV), 1) >> 7)
    causal = (_iota((C, HK), 1) & (C - 1)) <= _iota((C, HK), 0)
    ones = jnp.ones((C, LANES), F32)
    gain = gn_ref[...]

    def body(c, carry):
        r0 = pl.multiple_of(c * C, C)
        rows = pl.ds(r0, C)
        la = la_ref[rows, :]
        b = jnp.dot(tri, la, precision=HIGHEST, preferred_element_type=F32)
        b_end = b[C - 1:C, :]
        q = q_ref[rows, :].astype(F32)
        k = k_ref[rows, :].astype(F32)
        v = v_ref[rows, :]
        q_dec = (q * jnp.exp(b)).astype(BF16)
        k_inv = (k * jnp.exp(-b)).astype(BF16)
        k_end = (k * jnp.exp(b_end - b)).astype(BF16)
        k_bd = jnp.where(bdk, jnp.concatenate([k_inv] * GLA_HEADS, axis=0),
                         jnp.zeros((HK, HK), BF16))
        att = lax.dot_general(q_dec, k_bd, (((1,), (1,)), ((), ())),
                              preferred_element_type=F32)
        att = jnp.where(causal, att, 0.0).astype(BF16)
        v_bd = jnp.where(bdv, jnp.concatenate([v] * GLA_HEADS, axis=0),
                         jnp.zeros((HK, HV), BF16))
        state = s_ref[...]
        o = _dot(att, v_bd) + _dot(q_dec, state.astype(BF16))
        kv = lax.dot_general(k_end, v, (((0,), (0,)), ((), ())),
                             preferred_element_type=F32)
        d_col = lax.dot_general(la, ones, (((0,), (0,)), ((), ())),
                                precision=HIGHEST, preferred_element_type=F32)
        decay = jnp.exp(d_col)
        decay = jnp.concatenate([decay] * (HV // LANES), axis=1)
        s_ref[...] = decay * state + jnp.where(bdv, kv, 0.0)
        outs = []
        for h in range(GLA_HEADS):
            oh = o[:, h * GLA_DV:(h + 1) * GLA_DV]
            ms = jnp.mean(oh * oh, axis=-1, keepdims=True)
            outs.append(oh * lax.rsqrt(ms + EPS) * gain[:, h * GLA_DV:(h + 1) * GLA_DV])
        y = jnp.concatenate(outs, axis=1) * rs_ref[rows, :].astype(F32)
        o_ref[rows, :] = y.astype(BF16)
        return carry

    lax.fori_loop(0, nchunk, body, 0)


def _gla(gq, gk, gv, la, rs, gn, batch, seq, ts):
    T = gq.shape[0]
    per_b = seq // ts
    tok = lambda b, i: (b * per_b + i, 0)
    HK = GLA_HEADS * GLA_DK
    HV = GLA_HEADS * GLA_DV
    return pl.pallas_call(
        functools.partial(_gla_kernel, nchunk=ts // GLA_CHUNK),
        grid=(batch, per_b),
        in_specs=[
            pl.BlockSpec((ts, HK), tok),
            pl.BlockSpec((ts, HK), tok),
            pl.BlockSpec((ts, HV), tok),
            pl.BlockSpec((ts, HK), tok),
            pl.BlockSpec((ts, HV), tok),
            pl.BlockSpec((1, HV), lambda b, i: (0, 0)),
        ],
        out_specs=pl.BlockSpec((ts, HV), tok),
        out_shape=jax.ShapeDtypeStruct((T, HV), BF16),
        scratch_shapes=[pltpu.VMEM((HK, HV), F32)],
        compiler_params=_params(("arbitrary", "arbitrary")),
        name="gla",
    )(gq, gk, gv, la, rs, gn)


def _swa_kernel(q_ref, kc_ref, kp_ref, vc_ref, vp_ref, cc_ref, sc_ref, cp_ref, sp_ref,
                qg_ref, kg_ref, sink_ref, o_ref, *, tq):
    W = WINDOW
    nw = tq // W
    G = SWA_HEADS // SWA_KV_HEADS
    step = pl.program_id(1)

    lane = _iota((1, LANES), 1)
    lo_half = lane < SWA_HD
    first = (lane & (SWA_HD - 1)) < (ROPE_DIMS // 2)
    seg = ((_iota((LANES, LANES), 0) >> 6) == (_iota((LANES, LANES), 1) >> 6))
    seg_mean = jnp.where(seg, 1.0 / SWA_HD, 0.0).astype(BF16)

    def norm_rope(x, cos, sg, gain):
        x2 = x * x
        hi = x2.astype(BF16)
        lo = (x2 - hi.astype(F32)).astype(BF16)
        ms = _dot(hi, seg_mean) + _dot(lo, seg_mean)
        y = x * lax.rsqrt(ms + EPS) * gain
        partner = jnp.where(first, pltpu.roll(y, LANES - ROPE_DIMS // 2, 1),
                            pltpu.roll(y, ROPE_DIMS // 2, 1))
        return y * cos + partner * sg

    cos_c = cc_ref[...]
    sg_c = sc_ref[...]
    cos_k = jnp.concatenate([cp_ref[...], cos_c], axis=0)
    sg_k = jnp.concatenate([sp_ref[...], sg_c], axis=0)
    kb = jnp.concatenate([kp_ref[...], kc_ref[...]], axis=0).astype(F32)
    kb = norm_rope(kb, cos_k, sg_k, kg_ref[...])
    kb_r = pltpu.roll(kb, SWA_HD, 1)
    vb = jnp.concatenate([vp_ref[...], vc_ref[...]], axis=0).astype(F32)
    vb_r = pltpu.roll(vb, SWA_HD, 1)
    k_dup = (jnp.where(lo_half, kb, kb_r).astype(BF16), jnp.where(lo_half, kb_r, kb).astype(BF16))
    v_dup = (jnp.where(lo_half, vb, vb_r).astype(BF16), jnp.where(lo_half, vb_r, vb).astype(BF16))

    q_cols = []
    for c in range(SWA_HEADS // 2):
        qc = q_ref[:, c * LANES:(c + 1) * LANES].astype(F32)
        q_cols.append(norm_rope(qc, cos_c, sg_c, qg_ref[...]) * (SWA_HD ** -0.5))

    qi = _iota((W, 2 * W), 0)
    kj = _iota((W, 2 * W), 1)
    band = (kj > qi) & (kj <= qi + W)
    band_first = band & ((kj >= W) | (step > 0))
    sinks = sink_ref[...]

    for w in range(nw):
        valid = band_first if w == 0 else band
        for kvh in range(SWA_KV_HEADS):
            parts = []
            for cc in range(G // 2):
                qw = q_cols[kvh * (G // 2) + cc][w * W:(w + 1) * W, :]
                parts.append(jnp.where(lo_half, qw, 0.0))
                parts.append(jnp.where(lo_half, 0.0, qw))
            q_stack = jnp.concatenate(parts, axis=0).astype(BF16)
            k_band = k_dup[kvh][w * W:w * W + 2 * W, :]
            v_band = v_dup[kvh][w * W:w * W + 2 * W, :]
            s = lax.dot_general(q_stack, k_band, (((1,), (1,)), ((), ())),
                                preferred_element_type=F32)
            ps, inv = [], []
            for g in range(G):
                sg = jnp.where(valid, s[g * W:(g + 1) * W, :], -1e30)
                sink = sinks[kvh * G + g:kvh * G + g + 1, 0:1]
                m = jnp.maximum(jnp.max(sg, axis=-1, keepdims=True), sink)
                p = jnp.exp(sg - m)
                den = jnp.sum(p, axis=-1, keepdims=True) + jnp.exp(sink - m)
                ps.append(p.astype(BF16))
                inv.append(1.0 / den)
            o = _dot(jnp.concatenate(ps, axis=0), v_band)
            og = [o[g * W:(g + 1) * W, :] * inv[g] for g in range(G)]
            for cc in range(G // 2):
                col = jnp.where(lo_half, og[2 * cc], og[2 * cc + 1])
                c0 = (kvh * (G // 2) + cc) * LANES
                o_ref[w * W:(w + 1) * W, c0:c0 + LANES] = col.astype(BF16)


def _swa(sq, sk, sv, cos_t, sg_t, qg, kg, sinks_b, batch, seq, tq):
    T = sq.shape[0]
    per_b = seq // tq
    r = tq // WINDOW
    cur = lambda b, i: (b * per_b + i, 0)
    prev = lambda b, i: (jnp.maximum((b * per_b + i) * r - 1, 0), 0)
    const = lambda b, i: (0, 0)
    QW = SWA_HEADS * SWA_HD
    return pl.pallas_call(
        functools.partial(_swa_kernel, tq=tq),
        grid=(batch, per_b),
        in_specs=[
            pl.BlockSpec((tq, QW), cur),
            pl.BlockSpec((tq, LANES), cur),
            pl.BlockSpec((WINDOW, LANES), prev),
            pl.BlockSpec((tq, LANES), cur),
            pl.BlockSpec((WINDOW, LANES), prev),
            pl.BlockSpec((tq, LANES), cur),
            pl.BlockSpec((tq, LANES), cur),
            pl.BlockSpec((WINDOW, LANES), prev),
            pl.BlockSpec((WINDOW, LANES), prev),
            pl.BlockSpec((1, LANES), const),
            pl.BlockSpec((1, LANES), const),
            pl.BlockSpec((SWA_HEADS, LANES), const),
        ],
        out_specs=pl.BlockSpec((tq, QW), cur),
        out_shape=jax.ShapeDtypeStruct((T, QW), BF16),
        compiler_params=_params(("arbitrary", "arbitrary")),
        name="swa",
    )(sq, sk, sk, sv, sv, cos_t, sg_t, cos_t, sg_t, qg, kg, sinks_b)


def _second_largest(a, b, c, d):
    hi1, lo1 = jnp.maximum(a, b), jnp.minimum(a, b)
    hi2, lo2 = jnp.maximum(c, d), jnp.minimum(c, d)
    return jnp.maximum(hi1, hi2), jnp.maximum(jnp.minimum(hi1, hi2), jnp.maximum(lo1, lo2))


def _argmax4(vals):
    best, idx = vals[0], jnp.zeros(vals[0].shape, I32)
    for k in range(1, 4):
        upd = vals[k] > best
        idx = jnp.where(upd, k, idx)
        best = jnp.where(upd, vals[k], best)
    return idx, best


def _pick4(idx, vals):
    return jnp.where(idx == 0, vals[0],
                     jnp.where(idx == 1, vals[1], jnp.where(idx == 2, vals[2], vals[3])))


def _postmix_kernel(yg_ref, ys_ref, sa_ref, sb_ref, x_ref, g1_ref, wug_ref, wus_ref, wo_ref,
                    gain_ref, sc_ref, sh_ref, wr_ref, br_ref,
                    x1_ref, h2e_ref, meta_ref, cnt_ref, carry_ref, *, tm):
    @pl.when(pl.program_id(0) == 0)
    def _():
        carry_ref[...] = jnp.zeros_like(carry_ref)

    D = x_ref.shape[1]
    u = _dot(yg_ref[...], wug_ref[...])
    v = _dot(ys_ref[...], wus_ref[...])
    merged = sa_ref[...].astype(F32) * u + sb_ref[...].astype(F32) * v
    x1 = x_ref[...] + g1_ref[...] * _dot(merged.astype(BF16), wo_ref[...])
    x1_ref[...] = x1
    ms = jnp.mean(x1 * x1, axis=-1, keepdims=True)
    h2 = x1 * lax.rsqrt(ms + EPS) * gain_ref[...]
    h2 = h2 * (1.0 + sc_ref[...]) + sh_ref[...]
    h2e_ref[:, 0:D] = h2

    hi = h2.astype(BF16)
    lo = (h2 - hi.astype(F32)).astype(BF16)
    r = _dot(hi, wr_ref[...]) + _dot(lo, wr_ref[...])
    logits = r + pltpu.roll(r, LANES - N_EXPERTS, 1)
    lt = jnp.transpose(logits)[0:N_EXPERTS, :]
    aff = jax.nn.sigmoid(lt)
    sel = aff + br_ref[...]
    aff_r = [aff[e:e + 1, :] for e in range(N_EXPERTS)]
    sel_r = [sel[e:e + 1, :] for e in range(N_EXPERTS)]

    scores = []
    for g in range(N_GROUPS):
        m1, m2 = _second_largest(*sel_r[4 * g:4 * g + 4])
        scores.append(m1 + m2)
    grp, _ = _argmax4(scores)
    sel_g = [_pick4(grp, [sel_r[4 * g + k] for g in range(N_GROUPS)]) for k in range(4)]
    aff_g = [_pick4(grp, [aff_r[4 * g + k] for g in range(N_GROUPS)]) for k in range(4)]
    l1, _ = _argmax4(sel_g)
    masked = [jnp.where(l1 == k, -jnp.inf, sel_g[k]) for k in range(4)]
    l2, _ = _argmax4(masked)
    a1 = _pick4(l1, aff_g)
    a2 = _pick4(l2, aff_g)
    den = a1 + a2
    w1 = a1 / den
    w2 = a2 / den
    lo_e = jnp.minimum(l1, l2)
    hi_e = jnp.maximum(l1, l2)
    pair = jnp.where(lo_e == 0, hi_e - 1, jnp.where(lo_e == 1, hi_e + 1, 5))
    cls = grp * len(PAIRS) + pair
    w_lo = jnp.where(l1 < l2, w1, w2)
    w_hi = jnp.where(l1 < l2, w2, w1)

    onehot = _iota((CLASS_ROWS, tm), 0) == cls
    oh = jnp.where(onehot, 1.0, 0.0).astype(BF16)
    upper = jnp.where(_iota((tm, tm), 0) <= _iota((tm, tm), 1), 1.0, 0.0).astype(BF16)
    incl = _dot(oh, upper)
    total = _dot(oh, jnp.ones((tm, LANES), BF16))
    carry = carry_ref[...]
    base = jnp.concatenate([carry] * (tm // LANES), axis=1)
    rank = jnp.sum(jnp.where(onehot, base + incl, 0.0), axis=0, keepdims=True) - 1.0
    carry_ref[...] = carry + total
    cnt_ref[...] = carry + total

    meta_ref[...] = jnp.concatenate(
        [cls, rank.astype(I32), jnp.zeros((SUBLANES - 2, tm), I32)], axis=0)
    w_rows = jnp.concatenate([w_lo, w_hi, jnp.zeros((LANES - 2, tm), F32)], axis=0)
    h2e_ref[:, D:D + WCOLS] = jnp.transpose(w_rows)


def _postmix(yg, ys, sa, sb, x, g1, wug, wus, wo, gain, sc, sh, wr, br, seq, tm):
    T, D = x.shape
    per_b = seq // tm
    nt = T // tm
    tok = lambda i: (i, 0)
    row = lambda i: (0, 0)
    bat = lambda i: (i // per_b, 0, 0)
    return pl.pallas_call(
        functools.partial(_postmix_kernel, tm=tm),
        grid=(nt,),
        in_specs=[
            pl.BlockSpec((tm, yg.shape[1]), tok),
            pl.BlockSpec((tm, ys.shape[1]), tok),
            pl.BlockSpec((tm, D), tok),
            pl.BlockSpec((tm, D), tok),
            pl.BlockSpec((tm, D), tok),
            pl.BlockSpec((None, 1, D), bat),
            pl.BlockSpec(wug.shape, row),
            pl.BlockSpec(wus.shape, row),
            pl.BlockSpec(wo.shape, row),
            pl.BlockSpec((1, D), row),
            pl.BlockSpec((None, 1, D), bat),
            pl.BlockSpec((None, 1, D), bat),
            pl.BlockSpec((D, LANES), row),
            pl.BlockSpec((N_EXPERTS, 1), row),
        ],
        out_specs=[
            pl.BlockSpec((tm, D), tok),
            pl.BlockSpec((tm, D + WCOLS), tok),
            pl.BlockSpec((None, SUBLANES, tm), lambda i: (i, 0, 0)),
            pl.BlockSpec((CLASS_ROWS, LANES), row),
        ],
        out_shape=[
            jax.ShapeDtypeStruct((T, D), F32),
            jax.ShapeDtypeStruct((T, D + WCOLS), F32),
            jax.ShapeDtypeStruct((nt, SUBLANES, tm), I32),
            jax.ShapeDtypeStruct((CLASS_ROWS, LANES), F32),
        ],
        scratch_shapes=[pltpu.VMEM((CLASS_ROWS, LANES), F32)],
        compiler_params=_params(("arbitrary",), VMEM_LIMIT),
        name="postmix",
    )(yg, ys, sa, sb, x, g1, wug, wus, wo, gain, sc, sh, wr, br)


def _plan_kernel(start_ref, meta_ref, dest_ref):
    cls = meta_ref[0:1, :]
    rank = meta_ref[1:2, :]
    base = jnp.zeros(cls.shape, I32)
    for c in range(N_CLASSES):
        base = jnp.where(cls == c, start_ref[c], base)
    dest_ref[...] = base + rank


def _plan(start, meta):
    nt, _, tm = meta.shape
    return pl.pallas_call(
        _plan_kernel,
        grid_spec=pltpu.PrefetchScalarGridSpec(
            num_scalar_prefetch=1,
            grid=(nt,),
            in_specs=[pl.BlockSpec((None, SUBLANES, tm), lambda i, s: (i, 0, 0))],
            out_specs=pl.BlockSpec((None, 1, tm), lambda i, s: (i, 0, 0)),
        ),
        out_shape=jax.ShapeDtypeStruct((nt, 1, tm), I32),
        compiler_params=_params(("arbitrary",)),
        name="plan",
    )(start, meta)


ROW_UNROLL = 8


def _dispatch_kernel(dest_ref, h_ref, xs_in_ref, xs_ref, sem, *, tm):
    del xs_in_ref

    def row_copy(r, d):
        return pltpu.make_async_copy(h_ref.at[pl.ds(r, 1)], xs_ref.at[pl.ds(d, 1)], sem)

    def issue(i, carry):
        for u in range(ROW_UNROLL):
            r = i * ROW_UNROLL + u
            row_copy(r, dest_ref[0, r]).start()
        return carry

    def drain(i, carry):
        for u in range(ROW_UNROLL):
            row_copy(0, 0).wait()
        return carry

    lax.fori_loop(0, tm // ROW_UNROLL, issue, 0)
    lax.fori_loop(0, tm // ROW_UNROLL, drain, 0)


def _dispatch(dest, h2e, xs_zero):
    nt, _, tm = dest.shape
    width = h2e.shape[1]
    return pl.pallas_call(
        functools.partial(_dispatch_kernel, tm=tm),
        grid=(nt,),
        in_specs=[
            pl.BlockSpec((None, 1, tm), lambda i: (i, 0, 0), memory_space=pltpu.SMEM),
            pl.BlockSpec((tm, width), lambda i: (i, 0)),
            pl.BlockSpec(memory_space=pl.ANY),
        ],
        out_specs=pl.BlockSpec(memory_space=pl.ANY),
        out_shape=jax.ShapeDtypeStruct(xs_zero.shape, F32),
        scratch_shapes=[pltpu.SemaphoreType.DMA(())],
        input_output_aliases={2: 0},
        compiler_params=_params(("arbitrary",)),
        name="dispatch",
    )(dest, h2e, xs_zero)


def _moe_kernel(elo_ref, ehi_ref, nact_ref, xs_ref, wg0, wu0, wd0, wg1, wu1, wd1, o_ref):
    j = pl.program_id(0)
    D = o_ref.shape[1]

    @pl.when(j < nact_ref[0])
    def _():
        x = xs_ref[:, 0:D].astype(BF16)
        w_lo = xs_ref[:, D:D + 1]
        w_hi = xs_ref[:, D + 1:D + 2]

        def ffn(wg, wu, wd):
            g = _dot(x, wg[...])
            a = (g * jax.nn.sigmoid(g)) * _dot(x, wu[...])
            return _dot(a.astype(BF16), wd[...])

        o_ref[...] = w_lo * ffn(wg0, wu0, wd0) + w_hi * ffn(wg1, wu1, wd1)

    @pl.when(j >= nact_ref[0])
    def _():
        o_ref[...] = jnp.zeros_like(o_ref)


def _moe(elo, ehi, nact, xs, w_gate, w_up, w_down):
    ns, width = xs.shape
    D = width - WCOLS
    nb = ns // MOE_ROWS
    lo = lambda j, a, b, n: (a[j], 0, 0)
    hi = lambda j, a, b, n: (b[j], 0, 0)
    gu = (None, D, D_FF)
    dn = (None, D_FF, D)
    return pl.pallas_call(
        _moe_kernel,
        grid_spec=pltpu.PrefetchScalarGridSpec(
            num_scalar_prefetch=3,
            grid=(nb,),
            in_specs=[
                pl.BlockSpec((MOE_ROWS, width), lambda j, a, b, n: (j, 0)),
                pl.BlockSpec(gu, lo), pl.BlockSpec(gu, lo), pl.BlockSpec(dn, lo),
                pl.BlockSpec(gu, hi), pl.BlockSpec(gu, hi), pl.BlockSpec(dn, hi),
            ],
            out_specs=pl.BlockSpec((MOE_ROWS, D), lambda j, a, b, n: (j, 0)),
        ),
        out_shape=jax.ShapeDtypeStruct((ns, D), F32),
        compiler_params=_params(("arbitrary",), VMEM_LIMIT),
        name="moe",
    )(elo, ehi, nact, xs, w_gate, w_up, w_down, w_gate, w_up, w_down)


def _combine_kernel(dest_ref, x1_ref, g2_ref, ys_ref, o_ref, buf, sem, *, tm):
    def row_copy(r, d):
        return pltpu.make_async_copy(ys_ref.at[pl.ds(d, 1)], buf.at[pl.ds(r, 1)], sem)

    def issue(i, carry):
        for u in range(ROW_UNROLL):
            r = i * ROW_UNROLL + u
            row_copy(r, dest_ref[0, r]).start()
        return carry

    def drain(i, carry):
        for u in range(ROW_UNROLL):
            row_copy(0, 0).wait()
        return carry

    lax.fori_loop(0, tm // ROW_UNROLL, issue, 0)
    lax.fori_loop(0, tm // ROW_UNROLL, drain, 0)
    o_ref[...] = x1_ref[...] + g2_ref[...] * buf[...]


def _combine(dest, x1, g2, ys, seq):
    nt, _, tm = dest.shape
    T, D = x1.shape
    per_b = seq // tm
    return pl.pallas_call(
        functools.partial(_combine_kernel, tm=tm),
        grid=(nt,),
        in_specs=[
            pl.BlockSpec((None, 1, tm), lambda i: (i, 0, 0), memory_space=pltpu.SMEM),
            pl.BlockSpec((tm, D), lambda i: (i, 0)),
            pl.BlockSpec((None, 1, D), lambda i: (i // per_b, 0, 0)),
            pl.BlockSpec(memory_space=pl.ANY),
        ],
        out_specs=pl.BlockSpec((tm, D), lambda i: (i, 0)),
        out_shape=jax.ShapeDtypeStruct((T, D), F32),
        scratch_shapes=[pltpu.VMEM((tm, D), F32), pltpu.SemaphoreType.DMA(())],
        compiler_params=_params(("arbitrary",)),
        name="combine",
    )(dest, x1, g2, ys)


def _block_plan(counts, nb):
    cnt = counts[:N_CLASSES, 0].astype(I32)
    nblk = (cnt + MOE_ROWS - 1) // MOE_ROWS
    cum = jnp.cumsum(nblk)
    start = ((cum - nblk) * MOE_ROWS).astype(I32)
    nact = cum[-1:].astype(I32)
    blk = jnp.arange(nb, dtype=I32)
    cls = jnp.searchsorted(cum, jnp.minimum(blk, nact[0] - 1), side="right").astype(I32)
    cls = jnp.minimum(cls, N_CLASSES - 1)
    lo_tab = jnp.array([p[0] for p in PAIRS], I32)
    hi_tab = jnp.array([p[1] for p in PAIRS], I32)
    grp = cls // len(PAIRS)
    elo = grp * EXPERTS_PER_GROUP + lo_tab[cls % len(PAIRS)]
    ehi = grp * EXPERTS_PER_GROUP + hi_tab[cls % len(PAIRS)]
    start = jnp.concatenate([start, jnp.zeros((CLASS_ROWS - N_CLASSES,), I32)])
    return start, elo, ehi, nact


def kernel(x, c, positions, w_ada, b_ada, norm_mix, w_in, w_alpha2, b_alpha, gla_norm, q_norm,
           k_norm, sinks, w_up_gla, w_up_swa, w_out, norm_ffn, w_router, b_router, w_gate, w_up,
           w_down):
    B, S, D = x.shape
    L = w_ada.shape[0]
    T = B * S
    tm = min(512, S)

    mod = _modulation(c, w_ada, b_ada).reshape(L, B, 6, 1, D)
    cos_t, sg_t = _rope_tables(positions)

    sizes = (256, 256, 512, 16, 512, 512, 128, 128, 1024, 1024)
    offs = [0]
    for s in sizes:
        offs.append(offs[-1] + s)
    order = (0, 1, 2, 4, 5, 6, 7, 8, 9, 3)
    w_in_p = jnp.concatenate(
        [w_in[:, :, offs[i]:offs[i + 1]] for i in order]
        + [jnp.zeros((L, D, LANES - GLA_RANK), F32)], axis=-1).astype(BF16)
    wa2_p = jnp.concatenate(
        [w_alpha2, jnp.zeros((L, LANES - GLA_RANK, w_alpha2.shape[-1]), F32)], axis=1).astype(BF16)
    wr_hi = w_router.astype(BF16)
    wr_lo = (w_router - wr_hi.astype(F32)).astype(BF16)
    wr = jnp.concatenate([wr_hi, wr_lo, jnp.zeros((D, LANES - 2 * N_EXPERTS), BF16)], axis=1)
    br = b_router.reshape(N_EXPERTS, 1)
    wug = w_up_gla.astype(BF16)
    wus = w_up_swa.astype(BF16)
    wo = w_out.astype(BF16)
    wg = w_gate.astype(BF16)
    wu = w_up.astype(BF16)
    wd = w_down.astype(BF16)

    nb = T // MOE_ROWS + N_CLASSES
    xt = x.reshape(T, D)
    for l in range(L):
        sh1, sc1, g1, sh2, sc2, g2 = [mod[l, :, i] for i in range(6)]
        gq, gk, gv, la, rs, sq, sk, sv, sa, sb = _premix(
            xt, norm_mix[l].reshape(1, D), sc1, sh1, w_in_p[l], wa2_p[l],
            b_alpha[l].reshape(1, -1), S, tm)
        y_gla = _gla(gq, gk, gv, la, rs, gla_norm[l].reshape(1, -1), B, S, tm)
        qg = jnp.tile(q_norm[l], LANES // SWA_HD).reshape(1, LANES)
        kg = jnp.tile(k_norm[l], LANES // SWA_HD).reshape(1, LANES)
        sinks_b = jnp.broadcast_to(sinks[l][:, None], (SWA_HEADS, LANES))
        y_swa = _swa(sq, sk, sv, cos_t, sg_t, qg, kg, sinks_b, B, S, tm)
        x1, h2e, meta, counts = _postmix(
            y_gla, y_swa, sa, sb, xt, g1, wug[l], wus[l], wo[l],
            norm_ffn[l].reshape(1, D), sc2, sh2, wr, br, S, tm)
        start, elo, ehi, nact = _block_plan(counts, nb)
        dest = _plan(start, meta)
        xs = _dispatch(dest, h2e, jnp.zeros((nb * MOE_ROWS, D + WCOLS), F32))
        ys = _moe(elo, ehi, nact, xs, wg[l], wu[l], wd[l])
        xt = _combine(dest, x1, g2, ys, S)
    return xt.reshape(B, S, D)
```

```python
import functools

import jax
import jax.numpy as jnp
from jax import lax
from jax.experimental import pallas as pl
from jax.experimental.pallas import tpu as pltpu

F32 = jnp.float32
BF16 = jnp.bfloat16
I32 = jnp.int32
HIGHEST = lax.Precision.HIGHEST

GLA_HEADS = 4
GLA_DK = 64
GLA_DV = 128
GLA_RANK = 16
GLA_TAU = 16.0
GLA_CHUNK = 64
SWA_HEADS = 8
SWA_KV_HEADS = 2
SWA_HD = 64
WINDOW = 128
ROPE_DIMS = SWA_HD // 4
ROPE_THETA = 500000.0
N_EXPERTS = 16
N_GROUPS = 4
EXPERTS_PER_GROUP = 4
D_FF = 512
EPS = 1e-6

LANES = 128
SUBLANES = 8
VMEM_LIMIT = 56 * 1024 * 1024

PAIRS = ((0, 1), (0, 2), (0, 3), (1, 2), (1, 3), (2, 3))
N_CLASSES = N_GROUPS * len(PAIRS)
CLASS_ROWS = 32
MOE_ROWS = 256
WCOLS = LANES


def _params(sem, vmem=None):
    return pltpu.CompilerParams(dimension_semantics=sem, vmem_limit_bytes=vmem)


def _dot(a, b):
    return jnp.dot(a, b, preferred_element_type=F32)


def _iota(shape, axis):
    return lax.broadcasted_iota(I32, shape, axis)


def _mod_kernel(c_ref, w_ref, b_ref, o_ref):
    c = c_ref[...]
    cond = c * jax.nn.sigmoid(c)
    o_ref[...] = jnp.dot(cond, w_ref[...], precision=HIGHEST,
                         preferred_element_type=F32) + b_ref[...]


def _modulation(c, w_ada, b_ada):
    L, D, D6 = w_ada.shape
    B = c.shape[0]
    nj = D6 // D
    return pl.pallas_call(
        _mod_kernel,
        grid=(L, nj),
        in_specs=[
            pl.BlockSpec((B, D), lambda l, j: (0, 0)),
            pl.BlockSpec((None, D, D), lambda l, j: (l, 0, j)),
            pl.BlockSpec((None, 1, D), lambda l, j: (l, 0, j)),
        ],
        out_specs=pl.BlockSpec((None, B, D), lambda l, j: (l, 0, j)),
        out_shape=jax.ShapeDtypeStruct((L, B, D6), F32),
        compiler_params=_params(("arbitrary", "arbitrary")),
        name="modulation",
    )(c, w_ada, b_ada.reshape(L, 1, D6))


def _rope_kernel(pos_ref, inv_ref, c_ref, s_ref):
    ang = pos_ref[...].astype(F32) * inv_ref[...]
    l64 = _iota(ang.shape, 1) & (SWA_HD - 1)
    half = ROPE_DIMS // 2
    cos = jnp.cos(ang)
    sin = jnp.sin(ang)
    c_ref[...] = jnp.where(l64 < ROPE_DIMS, cos, 1.0)
    s_ref[...] = jnp.where(l64 < half, -sin, jnp.where(l64 < ROPE_DIMS, sin, 0.0))


def _rope_tables(positions):
    T = positions.size
    half = ROPE_DIMS // 2
    inv_freq = jnp.power(ROPE_THETA, -jnp.arange(half, dtype=F32) / half)
    lane = jnp.arange(LANES)
    inv_lane = inv_freq[(lane % SWA_HD) % half].reshape(1, LANES)
    tm = min(1024, T)
    return pl.pallas_call(
        _rope_kernel,
        grid=(T // tm,),
        in_specs=[pl.BlockSpec((tm, 1), lambda i: (i, 0)),
                  pl.BlockSpec((1, LANES), lambda i: (0, 0))],
        out_specs=[pl.BlockSpec((tm, LANES), lambda i: (i, 0))] * 2,
        out_shape=[jax.ShapeDtypeStruct((T, LANES), F32)] * 2,
        compiler_params=_params(("arbitrary",)),
        name="rope_tables",
    )(positions.reshape(T, 1), inv_lane)


_GQ, _GK, _GV, _GR = 0, 256, 512, 1024
_SQ, _SK, _SV = 1536, 2048, 2176
_MA, _MB, _GA, _WIN = 2304, 3328, 4352, 4480


def _premix_kernel(x_ref, gain_ref, sc_ref, sh_ref, w_ref, wa2_ref, ba_ref,
                   gq_ref, gk_ref, gv_ref, la_ref, gr_ref, sq_ref, sk_ref, sv_ref,
                   ma_ref, mb_ref):
    x = x_ref[...]
    ms = jnp.mean(x * x, axis=-1, keepdims=True)
    h = x * lax.rsqrt(ms + EPS) * gain_ref[...]
    h = h * (1.0 + sc_ref[...]) + sh_ref[...]
    hb = h.astype(BF16)

    def sec(lo, hi):
        return _dot(hb, w_ref[:, lo:hi])

    gq_ref[...] = (sec(_GQ, _GK) * (GLA_DK ** -0.5)).astype(BF16)
    gk_ref[...] = sec(_GK, _GV).astype(BF16)
    gv_ref[...] = sec(_GV, _GR).astype(BF16)
    r = sec(_GR, _SQ)
    gr_ref[...] = (r * jax.nn.sigmoid(r)).astype(BF16)
    sq_ref[...] = sec(_SQ, _SK).astype(BF16)
    sk_ref[...] = sec(_SK, _SV).astype(BF16)
    sv_ref[...] = sec(_SV, _MA).astype(BF16)
    ma_ref[...] = jax.nn.sigmoid(sec(_MA, _MB)).astype(BF16)
    mb_ref[...] = jax.nn.sigmoid(sec(_MB, _GA)).astype(BF16)
    a_low = sec(_GA, _WIN).astype(BF16)
    z = _dot(a_low, wa2_ref[...]) + ba_ref[...]
    log_sig = jnp.minimum(z, 0.0) - jnp.log1p(jnp.exp(-jnp.abs(z)))
    la_ref[...] = log_sig * (1.0 / GLA_TAU)


def _premix(x, gain, sc, sh, w_in_p, wa2_p, b_alpha, seq, tm):
    T, D = x.shape
    per_b = seq // tm
    tok = lambda i: (i, 0)
    row = lambda i: (0, 0)
    bat = lambda i: (i // per_b, 0, 0)
    widths = (256, 256, 512, 256, 512, 512, 128, 128, 1024, 1024)
    dts = (BF16, BF16, BF16, F32, BF16, BF16, BF16, BF16, BF16, BF16)
    return pl.pallas_call(
        _premix_kernel,
        grid=(T // tm,),
        in_specs=[
            pl.BlockSpec((tm, D), tok),
            pl.BlockSpec((1, D), row),
            pl.BlockSpec((None, 1, D), bat),
            pl.BlockSpec((None, 1, D), bat),
            pl.BlockSpec((D, _WIN), row),
            pl.BlockSpec((LANES, 256), row),
            pl.BlockSpec((1, 256), row),
        ],
        out_specs=[pl.BlockSpec((tm, w), tok) for w in widths],
        out_shape=[jax.ShapeDtypeStruct((T, w), dt) for w, dt in zip(widths, dts)],
        compiler_params=_params(("arbitrary",), VMEM_LIMIT),
        name="premix",
    )(x, gain, sc, sh, w_in_p, wa2_p, b_alpha)


def _gla_kernel(q_ref, k_ref, v_ref, la_ref, rs_ref, gn_ref, o_ref, s_ref, *, nchunk):
    C = GLA_CHUNK
    H = GLA_HEADS
    HK = H * GLA_DK

    @pl.when(pl.program_id(1) == 0)
    def _():
        s_ref[...] = jnp.zeros_like(s_ref)

    tri = jnp.where(_iota((C, C), 1) <= _iota((C, C), 0), 1.0, 0.0).astype(BF16)
    causal = (_iota((C, HK), 1) & (C - 1)) <= _iota((C, HK), 0)
    lane_head = _iota((1, HK), 1) >> 6
    ones = jnp.ones((C, LANES), BF16)
    zero_blk = jnp.zeros((C, GLA_DV), BF16)
    gain = gn_ref[...]
    tn = (((0,), (0,)), ((), ()))
    nt = (((1,), (1,)), ((), ()))

    def block_diag(blocks):
        rows = [jnp.concatenate([zero_blk] * h + [blk] + [zero_blk] * (H - 1 - h), axis=1)
                for h, blk in enumerate(blocks)]
        return jnp.concatenate(rows, axis=0)

    state = [s_ref[h] for h in range(H)]
    for c in range(nchunk):
        rows = slice(c * C, (c + 1) * C)
        la = la_ref[rows, :]
        la_hi = la.astype(BF16)
        la_lo = (la - la_hi.astype(F32)).astype(BF16)
        b = _dot(tri, la_hi) + _dot(tri, la_lo)
        d_col = (lax.dot_general(la_hi, ones, tn, preferred_element_type=F32)
                 + lax.dot_general(la_lo, ones, tn, preferred_element_type=F32))
        decay = jnp.exp(d_col)
        q = q_ref[rows, :].astype(F32)
        k = k_ref[rows, :].astype(F32)
        v = v_ref[rows, :]
        q_dec = (q * jnp.exp(b)).astype(BF16)
        k_inv = k * jnp.exp(-b)
        k_end = (k_inv * jnp.exp(b[C - 1:C, :])).astype(BF16)
        k_inv = k_inv.astype(BF16)
        k_bd = jnp.concatenate(
            [jnp.where(lane_head == h, k_inv, jnp.zeros_like(k_inv)) for h in range(H)], axis=0)
        att = lax.dot_general(q_dec, k_bd, nt, preferred_element_type=F32)
        att = jnp.where(causal, att, 0.0).astype(BF16)
        v_heads = [v[:, h * GLA_DV:(h + 1) * GLA_DV] for h in range(H)]
        rhs = jnp.concatenate(
            [block_diag(v_heads), block_diag([s.astype(BF16) for s in state])], axis=0)
        o = _dot(jnp.concatenate([att, q_dec], axis=1), rhs)
        for p in range(H // 2):
            kv = lax.dot_general(k_end[:, p * LANES:(p + 1) * LANES],
                                 v[:, 2 * p * GLA_DV:(2 * p + 2) * GLA_DV], tn,
                                 preferred_element_type=F32)
            for e in range(2):
                h = 2 * p + e
                state[h] = (decay[h * GLA_DK:(h + 1) * GLA_DK, :] * state[h]
                            + kv[e * GLA_DK:(e + 1) * GLA_DK, e * GLA_DV:(e + 1) * GLA_DV])
        outs = []
        for h in range(H):
            oh = o[:, h * GLA_DV:(h + 1) * GLA_DV]
            ms = jnp.mean(oh * oh, axis=-1, keepdims=True)
            outs.append(oh * lax.rsqrt(ms + EPS) * gain[:, h * GLA_DV:(h + 1) * GLA_DV])
        y = jnp.concatenate(outs, axis=1) * rs_ref[rows, :].astype(F32)
        o_ref[rows, :] = y.astype(BF16)
    for h in range(H):
        s_ref[h] = state[h]


def _gla(gq, gk, gv, la, rs, gn, batch, seq, ts):
    T = gq.shape[0]
    per_b = seq // ts
    tok = lambda b, i: (b * per_b + i, 0)
    HK = GLA_HEADS * GLA_DK
    HV = GLA_HEADS * GLA_DV
    return pl.pallas_call(
        functools.partial(_gla_kernel, nchunk=ts // GLA_CHUNK),
        grid=(batch, per_b),
        in_specs=[
            pl.BlockSpec((ts, HK), tok),
            pl.BlockSpec((ts, HK), tok),
            pl.BlockSpec((ts, HV), tok),
            pl.BlockSpec((ts, HK), tok),
            pl.BlockSpec((ts, HV), tok),
            pl.BlockSpec((1, HV), lambda b, i: (0, 0)),
        ],
        out_specs=pl.BlockSpec((ts, HV), tok),
        out_shape=jax.ShapeDtypeStruct((T, HV), BF16),
        scratch_shapes=[pltpu.VMEM((GLA_HEADS, GLA_DK, GLA_DV), F32)],
        compiler_params=_params(("arbitrary", "arbitrary")),
        name="gla",
    )(gq, gk, gv, la, rs, gn)


def _swa_kernel(q_ref, kc_ref, kp_ref, vc_ref, vp_ref, cc_ref, sc_ref, cp_ref, sp_ref,
                qg_ref, kg_ref, sink_ref, o_ref, *, tq):
    W = WINDOW
    nw = tq // W
    G = SWA_HEADS // SWA_KV_HEADS
    step = pl.program_id(1)

    lane = _iota((1, LANES), 1)
    lo_half = lane < SWA_HD
    first = (lane & (SWA_HD - 1)) < (ROPE_DIMS // 2)
    seg = ((_iota((LANES, LANES), 0) >> 6) == (_iota((LANES, LANES), 1) >> 6))
    seg_mean = jnp.where(seg, 1.0 / SWA_HD, 0.0).astype(BF16)

    def norm_rope(x, cos, sg, gain):
        x2 = x * x
        hi = x2.astype(BF16)
        lo = (x2 - hi.astype(F32)).astype(BF16)
        ms = _dot(hi, seg_mean) + _dot(lo, seg_mean)
        y = x * lax.rsqrt(ms + EPS) * gain
        partner = jnp.where(first, pltpu.roll(y, LANES - ROPE_DIMS // 2, 1),
                            pltpu.roll(y, ROPE_DIMS // 2, 1))
        return y * cos + partner * sg

    cos_c = cc_ref[...]
    sg_c = sc_ref[...]
    cos_k = jnp.concatenate([cp_ref[...], cos_c], axis=0)
    sg_k = jnp.concatenate([sp_ref[...], sg_c], axis=0)
    kb = jnp.concatenate([kp_ref[...], kc_ref[...]], axis=0).astype(F32)
    kb = norm_rope(kb, cos_k, sg_k, kg_ref[...])
    kb_r = pltpu.roll(kb, SWA_HD, 1)
    vb = jnp.concatenate([vp_ref[...], vc_ref[...]], axis=0).astype(F32)
    vb_r = pltpu.roll(vb, SWA_HD, 1)
    k_dup = (jnp.where(lo_half, kb, kb_r).astype(BF16), jnp.where(lo_half, kb_r, kb).astype(BF16))
    v_dup = (jnp.where(lo_half, vb, vb_r).astype(BF16), jnp.where(lo_half, vb_r, vb).astype(BF16))

    q_cols = []
    for c in range(SWA_HEADS // 2):
        qc = q_ref[:, c * LANES:(c + 1) * LANES].astype(F32)
        q_cols.append(norm_rope(qc, cos_c, sg_c, qg_ref[...]) * (SWA_HD ** -0.5))

    cur_side = _iota((W, W), 0) <= _iota((W, W), 1)
    cur_side4 = jnp.concatenate([cur_side] * G, axis=1)
    sinks = sink_ref[...]
    nt = (((1,), (1,)), ((), ()))
    tn = (((0,), (0,)), ((), ()))

    for w in range(nw):
        for kvh in range(SWA_KV_HEADS):
            parts = []
            for cc in range(G // 2):
                qw = q_cols[kvh * (G // 2) + cc][w * W:(w + 1) * W, :]
                parts.append(jnp.where(lo_half, qw, 0.0))
                parts.append(jnp.where(lo_half, 0.0, qw))
            q_stack = jnp.concatenate(parts, axis=0).astype(BF16)
            k_prev = k_dup[kvh][w * W:(w + 1) * W, :]
            k_cur = k_dup[kvh][(w + 1) * W:(w + 2) * W, :]
            v_prev = v_dup[kvh][w * W:(w + 1) * W, :]
            v_cur = v_dup[kvh][(w + 1) * W:(w + 2) * W, :]
            s_cur = lax.dot_general(k_cur, q_stack, nt, preferred_element_type=F32)
            s_prev = lax.dot_general(k_prev, q_stack, nt, preferred_element_type=F32)
            if w == 0:
                s_prev = jnp.where(step > 0, s_prev, -1e30)
            s = jnp.where(cur_side4, s_cur, s_prev)
            sink = jnp.concatenate(
                [sinks[kvh * G + g:kvh * G + g + 1, :] for g in range(G)], axis=1)
            m = jnp.maximum(jnp.max(s, axis=0, keepdims=True), sink)
            p = jnp.exp(s - m)
            den = jnp.sum(p, axis=0, keepdims=True) + jnp.exp(sink - m)
            p = (p * (1.0 / den)).astype(BF16)
            zero = jnp.zeros_like(p)
            o = (lax.dot_general(jnp.where(cur_side4, p, zero), v_cur, tn,
                                 preferred_element_type=F32)
                 + lax.dot_general(jnp.where(cur_side4, zero, p), v_prev, tn,
                                   preferred_element_type=F32))
            og = [o[g * W:(g + 1) * W, :] for g in range(G)]
            for cc in range(G // 2):
                col = jnp.where(lo_half, og[2 * cc], og[2 * cc + 1])
                c0 = (kvh * (G // 2) + cc) * LANES
                o_ref[w * W:(w + 1) * W, c0:c0 + LANES] = col.astype(BF16)


def _swa(sq, sk, sv, cos_t, sg_t, qg, kg, sinks_b, batch, seq, tq):
    T = sq.shape[0]
    per_b = seq // tq
    r = tq // WINDOW
    cur = lambda b, i: (b * per_b + i, 0)
    prev = lambda b, i: (jnp.maximum((b * per_b + i) * r - 1, 0), 0)
    const = lambda b, i: (0, 0)
    QW = SWA_HEADS * SWA_HD
    return pl.pallas_call(
        functools.partial(_swa_kernel, tq=tq),
        grid=(batch, per_b),
        in_specs=[
            pl.BlockSpec((tq, QW), cur),
            pl.BlockSpec((tq, LANES), cur),
            pl.BlockSpec((WINDOW, LANES), prev),
            pl.BlockSpec((tq, LANES), cur),
            pl.BlockSpec((WINDOW, LANES), prev),
            pl.BlockSpec((tq, LANES), cur),
            pl.BlockSpec((tq, LANES), cur),
            pl.BlockSpec((WINDOW, LANES), prev),
            pl.BlockSpec((WINDOW, LANES), prev),
            pl.BlockSpec((1, LANES), const),
            pl.BlockSpec((1, LANES), const),
            pl.BlockSpec((SWA_HEADS, LANES), const),
        ],
        out_specs=pl.BlockSpec((tq, QW), cur),
        out_shape=jax.ShapeDtypeStruct((T, QW), BF16),
        compiler_params=_params(("arbitrary", "arbitrary")),
        name="swa",
    )(sq, sk, sk, sv, sv, cos_t, sg_t, cos_t, sg_t, qg, kg, sinks_b)


def _second_largest(a, b, c, d):
    hi1, lo1 = jnp.maximum(a, b), jnp.minimum(a, b)
    hi2, lo2 = jnp.maximum(c, d), jnp.minimum(c, d)
    return jnp.maximum(hi1, hi2), jnp.maximum(jnp.minimum(hi1, hi2), jnp.maximum(lo1, lo2))


def _argmax4(vals):
    best, idx = vals[0], jnp.zeros(vals[0].shape, I32)
    for k in range(1, 4):
        upd = vals[k] > best
        idx = jnp.where(upd, k, idx)
        best = jnp.where(upd, vals[k], best)
    return idx, best


def _pick4(idx, vals):
    return jnp.where(idx == 0, vals[0],
                     jnp.where(idx == 1, vals[1], jnp.where(idx == 2, vals[2], vals[3])))


def _postmix_kernel(yg_ref, ys_ref, sa_ref, sb_ref, x_ref, g1_ref, wug_ref, wus_ref, wo_ref,
                    gain_ref, sc_ref, sh_ref, wr_ref, br_ref,
                    x1_ref, h2e_ref, meta_ref, cnt_ref, carry_ref, *, tm):
    @pl.when(pl.program_id(0) == 0)
    def _():
        carry_ref[...] = jnp.zeros_like(carry_ref)

    D = x_ref.shape[1]
    u = _dot(yg_ref[...], wug_ref[...])
    v = _dot(ys_ref[...], wus_ref[...])
    merged = sa_ref[...].astype(F32) * u + sb_ref[...].astype(F32) * v
    x1 = x_ref[...] + g1_ref[...] * _dot(merged.astype(BF16), wo_ref[...])
    x1_ref[...] = x1
    ms = jnp.mean(x1 * x1, axis=-1, keepdims=True)
    h2 = x1 * lax.rsqrt(ms + EPS) * gain_ref[...]
    h2 = h2 * (1.0 + sc_ref[...]) + sh_ref[...]
    h2e_ref[:, 0:D] = h2

    hi = h2.astype(BF16)
    lo = (h2 - hi.astype(F32)).astype(BF16)
    r = _dot(hi, wr_ref[...]) + _dot(lo, wr_ref[...])
    logits = r + pltpu.roll(r, LANES - N_EXPERTS, 1)
    lt = jnp.transpose(logits)[0:N_EXPERTS, :]
    aff = jax.nn.sigmoid(lt)
    sel = aff + br_ref[...]
    aff_r = [aff[e:e + 1, :] for e in range(N_EXPERTS)]
    sel_r = [sel[e:e + 1, :] for e in range(N_EXPERTS)]

    scores = []
    for g in range(N_GROUPS):
        m1, m2 = _second_largest(*sel_r[4 * g:4 * g + 4])
        scores.append(m1 + m2)
    grp, _ = _argmax4(scores)
    sel_g = [_pick4(grp, [sel_r[4 * g + k] for g in range(N_GROUPS)]) for k in range(4)]
    aff_g = [_pick4(grp, [aff_r[4 * g + k] for g in range(N_GROUPS)]) for k in range(4)]
    l1, _ = _argmax4(sel_g)
    masked = [jnp.where(l1 == k, -jnp.inf, sel_g[k]) for k in range(4)]
    l2, _ = _argmax4(masked)
    a1 = _pick4(l1, aff_g)
    a2 = _pick4(l2, aff_g)
    den = a1 + a2
    w1 = a1 / den
    w2 = a2 / den
    lo_e = jnp.minimum(l1, l2)
    hi_e = jnp.maximum(l1, l2)
    pair = jnp.where(lo_e == 0, hi_e - 1, jnp.where(lo_e == 1, hi_e + 1, 5))
    cls = grp * len(PAIRS) + pair
    w_lo = jnp.where(l1 < l2, w1, w2)
    w_hi = jnp.where(l1 < l2, w2, w1)

    onehot = _iota((CLASS_ROWS, tm), 0) == cls
    oh = jnp.where(onehot, 1.0, 0.0).astype(BF16)
    upper = jnp.where(_iota((tm, tm), 0) <= _iota((tm, tm), 1), 1.0, 0.0).astype(BF16)
    incl = _dot(oh, upper)
    total = _dot(oh, jnp.ones((tm, LANES), BF16))
    carry = carry_ref[...]
    base = jnp.concatenate([carry] * (tm // LANES), axis=1)
    rank = jnp.sum(jnp.where(onehot, base + incl, 0.0), axis=0, keepdims=True) - 1.0
    carry_ref[...] = carry + total
    cnt_ref[...] = carry + total

    meta_ref[...] = jnp.concatenate(
        [cls, rank.astype(I32), jnp.zeros((SUBLANES - 2, tm), I32)], axis=0)
    w_rows = jnp.concatenate([w_lo, w_hi, jnp.zeros((LANES - 2, tm), F32)], axis=0)
    h2e_ref[:, D:D + WCOLS] = jnp.transpose(w_rows)


def _postmix(yg, ys, sa, sb, x, g1, wug, wus, wo, gain, sc, sh, wr, br, seq, tm):
    T, D = x.shape
    per_b = seq // tm
    nt = T // tm
    tok = lambda i: (i, 0)
    row = lambda i: (0, 0)
    bat = lambda i: (i // per_b, 0, 0)
    return pl.pallas_call(
        functools.partial(_postmix_kernel, tm=tm),
        grid=(nt,),
        in_specs=[
            pl.BlockSpec((tm, yg.shape[1]), tok),
            pl.BlockSpec((tm, ys.shape[1]), tok),
            pl.BlockSpec((tm, D), tok),
            pl.BlockSpec((tm, D), tok),
            pl.BlockSpec((tm, D), tok),
            pl.BlockSpec((None, 1, D), bat),
            pl.BlockSpec(wug.shape, row),
            pl.BlockSpec(wus.shape, row),
            pl.BlockSpec(wo.shape, row),
            pl.BlockSpec((1, D), row),
            pl.BlockSpec((None, 1, D), bat),
            pl.BlockSpec((None, 1, D), bat),
            pl.BlockSpec((D, LANES), row),
            pl.BlockSpec((N_EXPERTS, 1), row),
        ],
        out_specs=[
            pl.BlockSpec((tm, D), tok),
            pl.BlockSpec((tm, D + WCOLS), tok),
            pl.BlockSpec((None, SUBLANES, tm), lambda i: (i, 0, 0)),
            pl.BlockSpec((CLASS_ROWS, LANES), row),
        ],
        out_shape=[
            jax.ShapeDtypeStruct((T, D), F32),
            jax.ShapeDtypeStruct((T, D + WCOLS), F32),
            jax.ShapeDtypeStruct((nt, SUBLANES, tm), I32),
            jax.ShapeDtypeStruct((CLASS_ROWS, LANES), F32),
        ],
        scratch_shapes=[pltpu.VMEM((CLASS_ROWS, LANES), F32)],
        compiler_params=_params(("arbitrary",), VMEM_LIMIT),
        name="postmix",
    )(yg, ys, sa, sb, x, g1, wug, wus, wo, gain, sc, sh, wr, br)


def _plan_kernel(start_ref, meta_ref, dest_ref):
    cls = meta_ref[0:1, :]
    rank = meta_ref[1:2, :]
    base = jnp.zeros(cls.shape, I32)
    for c in range(N_CLASSES):
        base = jnp.where(cls == c, start_ref[c], base)
    dest_ref[...] = base + rank


def _plan(start, meta):
    nt, _, tm = meta.shape
    return pl.pallas_call(
        _plan_kernel,
        grid_spec=pltpu.PrefetchScalarGridSpec(
            num_scalar_prefetch=1,
            grid=(nt,),
            in_specs=[pl.BlockSpec((None, SUBLANES, tm), lambda i, s: (i, 0, 0))],
            out_specs=pl.BlockSpec((None, 1, tm), lambda i, s: (i, 0, 0)),
        ),
        out_shape=jax.ShapeDtypeStruct((nt, 1, tm), I32),
        compiler_params=_params(("arbitrary",)),
        name="plan",
    )(start, meta)


ROW_UNROLL = 8


def _dispatch_kernel(dest_ref, h_ref, xs_in_ref, xs_ref, sem, *, tm):
    del xs_in_ref

    def row_copy(r, d):
        return pltpu.make_async_copy(h_ref.at[pl.ds(r, 1)], xs_ref.at[pl.ds(d, 1)], sem)

    def issue(i, carry):
        for u in range(ROW_UNROLL):
            r = i * ROW_UNROLL + u
            row_copy(r, dest_ref[0, r]).start()
        return carry

    def drain(i, carry):
        for u in range(ROW_UNROLL):
            row_copy(0, 0).wait()
        return carry

    lax.fori_loop(0, tm // ROW_UNROLL, issue, 0)
    lax.fori_loop(0, tm // ROW_UNROLL, drain, 0)


def _dispatch(dest, h2e, xs_zero):
    nt, _, tm = dest.shape
    width = h2e.shape[1]
    return pl.pallas_call(
        functools.partial(_dispatch_kernel, tm=tm),
        grid=(nt,),
        in_specs=[
            pl.BlockSpec((None, 1, tm), lambda i: (i, 0, 0), memory_space=pltpu.SMEM),
            pl.BlockSpec((tm, width), lambda i: (i, 0)),
            pl.BlockSpec(memory_space=pl.ANY),
        ],
        out_specs=pl.BlockSpec(memory_space=pl.ANY),
        out_shape=jax.ShapeDtypeStruct(xs_zero.shape, F32),
        scratch_shapes=[pltpu.SemaphoreType.DMA(())],
        input_output_aliases={2: 0},
        compiler_params=_params(("arbitrary",)),
        name="dispatch",
    )(dest, h2e, xs_zero)


def _moe_kernel(elo_ref, ehi_ref, nact_ref, xs_ref, wg0, wu0, wd0, wg1, wu1, wd1, o_ref):
    j = pl.program_id(0)
    D = o_ref.shape[1]

    @pl.when(j < nact_ref[0])
    def _():
        x = xs_ref[:, 0:D].astype(BF16)
        w_lo = xs_ref[:, D:D + 1]
        w_hi = xs_ref[:, D + 1:D + 2]

        def ffn(wg, wu, wd):
            g = _dot(x, wg[...])
            a = (g * jax.nn.sigmoid(g)) * _dot(x, wu[...])
            return _dot(a.astype(BF16), wd[...])

        o_ref[...] = w_lo * ffn(wg0, wu0, wd0) + w_hi * ffn(wg1, wu1, wd1)

    @pl.when(j >= nact_ref[0])
    def _():
        o_ref[...] = jnp.zeros_like(o_ref)


def _moe(elo, ehi, nact, xs, w_gate, w_up, w_down):
    ns, width = xs.shape
    D = width - WCOLS
    nb = ns // MOE_ROWS
    lo = lambda j, a, b, n: (a[j], 0, 0)
    hi = lambda j, a, b, n: (b[j], 0, 0)
    gu = (None, D, D_FF)
    dn = (None, D_FF, D)
    return pl.pallas_call(
        _moe_kernel,
        grid_spec=pltpu.PrefetchScalarGridSpec(
            num_scalar_prefetch=3,
            grid=(nb,),
            in_specs=[
                pl.BlockSpec((MOE_ROWS, width), lambda j, a, b, n: (j, 0)),
                pl.BlockSpec(gu, lo), pl.BlockSpec(gu, lo), pl.BlockSpec(dn, lo),
                pl.BlockSpec(gu, hi), pl.BlockSpec(gu, hi), pl.BlockSpec(dn, hi),
            ],
            out_specs=pl.BlockSpec((MOE_ROWS, D), lambda j, a, b, n: (j, 0)),
        ),
        out_shape=jax.ShapeDtypeStruct((ns, D), F32),
        compiler_params=_params(("arbitrary",), VMEM_LIMIT),
        name="moe",
    )(elo, ehi, nact, xs, w_gate, w_up, w_down, w_gate, w_up, w_down)


def _combine_kernel(dest_ref, x1_ref, g2_ref, ys_ref, o_ref, buf, sem, *, tm):
    def row_copy(r, d):
        return pltpu.make_async_copy(ys_ref.at[pl.ds(d, 1)], buf.at[pl.ds(r, 1)], sem)

    def issue(i, carry):
        for u in range(ROW_UNROLL):
            r = i * ROW_UNROLL + u
            row_copy(r, dest_ref[0, r]).start()
        return carry

    def drain(i, carry):
        for u in range(ROW_UNROLL):
            row_copy(0, 0).wait()
        return carry

    lax.fori_loop(0, tm // ROW_UNROLL, issue, 0)
    lax.fori_loop(0, tm // ROW_UNROLL, drain, 0)
    o_ref[...] = x1_ref[...] + g2_ref[...] * buf[...]


def _combine(dest, x1, g2, ys, seq):
    nt, _, tm = dest.shape
    T, D = x1.shape
    per_b = seq // tm
    return pl.pallas_call(
        functools.partial(_combine_kernel, tm=tm),
        grid=(nt,),
        in_specs=[
            pl.BlockSpec((None, 1, tm), lambda i: (i, 0, 0), memory_space=pltpu.SMEM),
            pl.BlockSpec((tm, D), lambda i: (i, 0)),
            pl.BlockSpec((None, 1, D), lambda i: (i // per_b, 0, 0)),
            pl.BlockSpec(memory_space=pl.ANY),
        ],
        out_specs=pl.BlockSpec((tm, D), lambda i: (i, 0)),
        out_shape=jax.ShapeDtypeStruct((T, D), F32),
        scratch_shapes=[pltpu.VMEM((tm, D), F32), pltpu.SemaphoreType.DMA(())],
        compiler_params=_params(("arbitrary",)),
        name="combine",
    )(dest, x1, g2, ys)


def _block_plan(counts, nb):
    cnt = counts[:N_CLASSES, 0].astype(I32)
    nblk = (cnt + MOE_ROWS - 1) // MOE_ROWS
    cum = jnp.cumsum(nblk)
    start = ((cum - nblk) * MOE_ROWS).astype(I32)
    nact = cum[-1:].astype(I32)
    blk = jnp.arange(nb, dtype=I32)
    cls = jnp.searchsorted(cum, jnp.minimum(blk, nact[0] - 1), side="right").astype(I32)
    cls = jnp.minimum(cls, N_CLASSES - 1)
    lo_tab = jnp.array([p[0] for p in PAIRS], I32)
    hi_tab = jnp.array([p[1] for p in PAIRS], I32)
    grp = cls // len(PAIRS)
    elo = grp * EXPERTS_PER_GROUP + lo_tab[cls % len(PAIRS)]
    ehi = grp * EXPERTS_PER_GROUP + hi_tab[cls % len(PAIRS)]
    start = jnp.concatenate([start, jnp.zeros((CLASS_ROWS - N_CLASSES,), I32)])
    return start, elo, ehi, nact


def kernel(x, c, positions, w_ada, b_ada, norm_mix, w_in, w_alpha2, b_alpha, gla_norm, q_norm,
           k_norm, sinks, w_up_gla, w_up_swa, w_out, norm_ffn, w_router, b_router, w_gate, w_up,
           w_down):
    B, S, D = x.shape
    L = w_ada.shape[0]
    T = B * S
    tm = min(512, S)

    mod = _modulation(c, w_ada, b_ada).reshape(L, B, 6, 1, D)
    cos_t, sg_t = _rope_tables(positions)

    sizes = (256, 256, 512, 16, 512, 512, 128, 128, 1024, 1024)
    offs = [0]
    for s in sizes:
        offs.append(offs[-1] + s)
    order = (0, 1, 2, 4, 5, 6, 7, 8, 9, 3)
    w_in_p = jnp.concatenate(
        [w_in[:, :, offs[i]:offs[i + 1]] for i in order]
        + [jnp.zeros((L, D, LANES - GLA_RANK), F32)], axis=-1).astype(BF16)
    wa2_p = jnp.concatenate(
        [w_alpha2, jnp.zeros((L, LANES - GLA_RANK, w_alpha2.shape[-1]), F32)], axis=1).astype(BF16)
    wr_hi = w_router.astype(BF16)
    wr_lo = (w_router - wr_hi.astype(F32)).astype(BF16)
    wr = jnp.concatenate([wr_hi, wr_lo, jnp.zeros((D, LANES - 2 * N_EXPERTS), BF16)], axis=1)
    br = b_router.reshape(N_EXPERTS, 1)
    wug = w_up_gla.astype(BF16)
    wus = w_up_swa.astype(BF16)
    wo = w_out.astype(BF16)
    wg = w_gate.astype(BF16)
    wu = w_up.astype(BF16)
    wd = w_down.astype(BF16)

    nb = T // MOE_ROWS + N_CLASSES
    xt = x.reshape(T, D)
    for l in range(L):
        sh1, sc1, g1, sh2, sc2, g2 = [mod[l, :, i] for i in range(6)]
        gq, gk, gv, la, rs, sq, sk, sv, sa, sb = _premix(
            xt, norm_mix[l].reshape(1, D), sc1, sh1, w_in_p[l], wa2_p[l],
            b_alpha[l].reshape(1, -1), S, tm)
        y_gla = _gla(gq, gk, gv, la, rs, gla_norm[l].reshape(1, -1), B, S, tm)
        qg = jnp.tile(q_norm[l], LANES // SWA_HD).reshape(1, LANES)
        kg = jnp.tile(k_norm[l], LANES // SWA_HD).reshape(1, LANES)
        sinks_b = jnp.broadcast_to(sinks[l][:, None], (SWA_HEADS, LANES))
        y_swa = _swa(sq, sk, sv, cos_t, sg_t, qg, kg, sinks_b, B, S, tm)
        x1, h2e, meta, counts = _postmix(
            y_gla, y_swa, sa, sb, xt, g1, wug[l], wus[l], wo[l],
            norm_ffn[l].reshape(1, D), sc2, sh2, wr, br, S, tm)
        start, elo, ehi, nact = _block_plan(counts, nb)
        dest = _plan(start, meta)
        xs = _dispatch(dest, h2e, jnp.zeros((nb * MOE_ROWS, D + WCOLS), F32))
        ys = _moe(elo, ehi, nact, xs, wg[l], wu[l], wd[l])
        xt = _combine(dest, x1, g2, ys, S)
    return xt.reshape(B, S, D)
```

```python
import functools

import jax
import jax.numpy as jnp
from jax import lax
from jax.experimental import pallas as pl
from jax.experimental.pallas import tpu as pltpu

F32 = jnp.float32
BF16 = jnp.bfloat16
I32 = jnp.int32
HIGHEST = lax.Precision.HIGHEST

GLA_HEADS = 4
GLA_DK = 64
GLA_DV = 128
GLA_RANK = 16
GLA_TAU = 16.0
GLA_CHUNK = 64
SWA_HEADS = 8
SWA_KV_HEADS = 2
SWA_HD = 64
WINDOW = 128
ROPE_DIMS = SWA_HD // 4
ROPE_THETA = 500000.0
N_EXPERTS = 16
N_GROUPS = 4
EXPERTS_PER_GROUP = 4
D_FF = 512
EPS = 1e-6

LANES = 128
SUBLANES = 8
VMEM_LIMIT = 56 * 1024 * 1024

PAIRS = ((0, 1), (0, 2), (0, 3), (1, 2), (1, 3), (2, 3))
N_CLASSES = N_GROUPS * len(PAIRS)
CLASS_ROWS = 32
MOE_ROWS = 512
POSTMIX_SUBTILES = 2
PREMIX_SUBTILES = 2
WCOLS = LANES


def _params(sem, vmem=None):
    return pltpu.CompilerParams(dimension_semantics=sem, vmem_limit_bytes=vmem)


def _dot(a, b):
    return jnp.dot(a, b, preferred_element_type=F32)


def _iota(shape, axis):
    return lax.broadcasted_iota(I32, shape, axis)


def _mod_kernel(c_ref, w_ref, b_ref, o_ref):
    c = c_ref[...]
    cond = c * jax.nn.sigmoid(c)
    o_ref[...] = jnp.dot(cond, w_ref[...], precision=HIGHEST,
                         preferred_element_type=F32) + b_ref[...]


def _modulation(c, w_ada, b_ada):
    L, D, D6 = w_ada.shape
    B = c.shape[0]
    nj = D6 // D
    return pl.pallas_call(
        _mod_kernel,
        grid=(L, nj),
        in_specs=[
            pl.BlockSpec((B, D), lambda l, j: (0, 0)),
            pl.BlockSpec((None, D, D), lambda l, j: (l, 0, j)),
            pl.BlockSpec((None, 1, D), lambda l, j: (l, 0, j)),
        ],
        out_specs=pl.BlockSpec((None, B, D), lambda l, j: (l, 0, j)),
        out_shape=jax.ShapeDtypeStruct((L, B, D6), F32),
        compiler_params=_params(("arbitrary", "arbitrary")),
        name="modulation",
    )(c, w_ada, b_ada.reshape(L, 1, D6))


def _rope_kernel(pos_ref, inv_ref, c_ref, s_ref):
    ang = pos_ref[...].astype(F32) * inv_ref[...]
    l64 = _iota(ang.shape, 1) & (SWA_HD - 1)
    half = ROPE_DIMS // 2
    cos = jnp.cos(ang)
    sin = jnp.sin(ang)
    c_ref[...] = jnp.where(l64 < ROPE_DIMS, cos, 1.0)
    s_ref[...] = jnp.where(l64 < half, -sin, jnp.where(l64 < ROPE_DIMS, sin, 0.0))


def _rope_tables(positions):
    T = positions.size
    half = ROPE_DIMS // 2
    inv_freq = jnp.power(ROPE_THETA, -jnp.arange(half, dtype=F32) / half)
    lane = jnp.arange(LANES)
    inv_lane = inv_freq[(lane % SWA_HD) % half].reshape(1, LANES)
    tm = min(1024, T)
    return pl.pallas_call(
        _rope_kernel,
        grid=(T // tm,),
        in_specs=[pl.BlockSpec((tm, 1), lambda i: (i, 0)),
                  pl.BlockSpec((1, LANES), lambda i: (0, 0))],
        out_specs=[pl.BlockSpec((tm, LANES), lambda i: (i, 0))] * 2,
        out_shape=[jax.ShapeDtypeStruct((T, LANES), F32)] * 2,
        compiler_params=_params(("arbitrary",)),
        name="rope_tables",
    )(positions.reshape(T, 1), inv_lane)


_GQ, _GK, _GV, _GR = 0, 256, 512, 1024
_SQ, _SK, _SV = 1536, 2048, 2176
_MA, _MB, _GA, _WIN = 2304, 3328, 4352, 4480


def _premix_kernel(x_ref, gain_ref, sc_ref, sh_ref, w_ref, wa2_ref, ba_ref,
                   gq_ref, gk_ref, gv_ref, la_ref, gr_ref, sq_ref, skv_ref,
                   ma_ref, mb_ref):
    sub = x_ref.shape[0] // PREMIX_SUBTILES
    half = (_MB - _MA) // 2
    for s in range(PREMIX_SUBTILES):
        rows = slice(s * sub, (s + 1) * sub)
        x = x_ref[rows, :]
        ms = jnp.mean(x * x, axis=-1, keepdims=True)
        h = x * lax.rsqrt(ms + EPS) * gain_ref[...]
        h = h * (1.0 + sc_ref[...]) + sh_ref[...]
        hb = h.astype(BF16)

        def sec(lo, hi):
            return _dot(hb, w_ref[:, lo:hi])

        gq_ref[rows, :] = (sec(_GQ, _GK) * (GLA_DK ** -0.5)).astype(BF16)
        gk_ref[rows, :] = sec(_GK, _GV).astype(BF16)
        gv_ref[rows, :] = sec(_GV, _GR).astype(BF16)
        r = sec(_GR, _SQ)
        gr_ref[rows, :] = (r * jax.nn.sigmoid(r)).astype(BF16)
        sq_ref[rows, :] = sec(_SQ, _SK).astype(BF16)
        skv_ref[rows, :] = sec(_SK, _MA).astype(BF16)
        for c in range(2):
            cols = slice(c * half, (c + 1) * half)
            ma_ref[rows, cols] = jax.nn.sigmoid(
                sec(_MA + c * half, _MA + (c + 1) * half)).astype(BF16)
            mb_ref[rows, cols] = jax.nn.sigmoid(
                sec(_MB + c * half, _MB + (c + 1) * half)).astype(BF16)
        a_low = sec(_GA, _WIN).astype(BF16)
        z = _dot(a_low, wa2_ref[...]) + ba_ref[...]
        log_sig = jnp.minimum(z, 0.0) - jnp.log1p(jnp.exp(-jnp.abs(z)))
        la_ref[rows, :] = log_sig * (1.0 / GLA_TAU)


def _premix(x, gain, sc, sh, w_in_p, wa2_p, b_alpha, seq, tm):
    T, D = x.shape
    per_b = seq // tm
    tok = lambda i: (i, 0)
    row = lambda i: (0, 0)
    bat = lambda i: (i // per_b, 0, 0)
    widths = (256, 256, 512, 256, 512, 512, 256, 1024, 1024)
    dts = (BF16, BF16, BF16, F32, BF16, BF16, BF16, BF16, BF16)
    return pl.pallas_call(
        _premix_kernel,
        grid=(T // tm,),
        in_specs=[
            pl.BlockSpec((tm, D), tok),
            pl.BlockSpec((1, D), row),
            pl.BlockSpec((None, 1, D), bat),
            pl.BlockSpec((None, 1, D), bat),
            pl.BlockSpec((D, _WIN), row),
            pl.BlockSpec((LANES, 256), row),
            pl.BlockSpec((1, 256), row),
        ],
        out_specs=[pl.BlockSpec((tm, w), tok) for w in widths],
        out_shape=[jax.ShapeDtypeStruct((T, w), dt) for w, dt in zip(widths, dts)],
        compiler_params=_params(("arbitrary",), VMEM_LIMIT),
        name="premix",
    )(x, gain, sc, sh, w_in_p, wa2_p, b_alpha)


def _gla_kernel(q_ref, k_ref, v_ref, la_ref, rs_ref, gn_ref, o_ref, s_ref, *, nchunk):
    C = GLA_CHUNK
    H = GLA_HEADS
    HK = H * GLA_DK

    @pl.when(pl.program_id(1) == 0)
    def _():
        s_ref[...] = jnp.zeros_like(s_ref)

    tri = jnp.where(_iota((C, C), 1) <= _iota((C, C), 0), 1.0, 0.0).astype(BF16)
    causal = (_iota((C, HK), 1) & (C - 1)) <= _iota((C, HK), 0)
    lane_head = _iota((1, HK), 1) >> 6
    ones = jnp.ones((C, LANES), BF16)
    zero_blk = jnp.zeros((C, GLA_DV), BF16)
    gain = gn_ref[...]
    tn = (((0,), (0,)), ((), ()))
    nt = (((1,), (1,)), ((), ()))

    def block_diag(blocks):
        rows = [jnp.concatenate([zero_blk] * h + [blk] + [zero_blk] * (H - 1 - h), axis=1)
                for h, blk in enumerate(blocks)]
        return jnp.concatenate(rows, axis=0)

    lhs, v_bds, kvs, decays = [], [], [], []
    for c in range(nchunk):
        rows = slice(c * C, (c + 1) * C)
        la = la_ref[rows, :]
        la_hi = la.astype(BF16)
        la_lo = (la - la_hi.astype(F32)).astype(BF16)
        b = _dot(tri, la_hi) + _dot(tri, la_lo)
        d_col = (lax.dot_general(la_hi, ones, tn, preferred_element_type=F32)
                 + lax.dot_general(la_lo, ones, tn, preferred_element_type=F32))
        decays.append(jnp.exp(d_col))
        q = q_ref[rows, :].astype(F32)
        k = k_ref[rows, :].astype(F32)
        v = v_ref[rows, :]
        q_dec = (q * jnp.exp(b)).astype(BF16)
        k_inv = k * jnp.exp(-b)
        k_end = (k_inv * jnp.exp(b[C - 1:C, :])).astype(BF16)
        k_inv = k_inv.astype(BF16)
        k_bd = jnp.concatenate(
            [jnp.where(lane_head == h, k_inv, jnp.zeros_like(k_inv)) for h in range(H)], axis=0)
        att = lax.dot_general(q_dec, k_bd, nt, preferred_element_type=F32)
        att = jnp.where(causal, att, 0.0).astype(BF16)
        lhs.append(jnp.concatenate([att, q_dec], axis=1))
        v_bds.append(block_diag([v[:, h * GLA_DV:(h + 1) * GLA_DV] for h in range(H)]))
        kvs.append([lax.dot_general(k_end[:, p * LANES:(p + 1) * LANES],
                                    v[:, 2 * p * GLA_DV:(2 * p + 2) * GLA_DV], tn,
                                    preferred_element_type=F32) for p in range(H // 2)])

    state = [s_ref[h] for h in range(H)]
    states = []
    for c in range(nchunk):
        states.append(state)
        state = [decays[c][h * GLA_DK:(h + 1) * GLA_DK, :] * state[h]
                 + kvs[c][h // 2][(h % 2) * GLA_DK:(h % 2 + 1) * GLA_DK,
                                  (h % 2) * GLA_DV:(h % 2 + 1) * GLA_DV]
                 for h in range(H)]
    for h in range(H):
        s_ref[h] = state[h]

    for c in range(nchunk):
        rows = slice(c * C, (c + 1) * C)
        rhs = jnp.concatenate(
            [v_bds[c], block_diag([s.astype(BF16) for s in states[c]])], axis=0)
        o = _dot(lhs[c], rhs)
        outs = []
        for h in range(H):
            oh = o[:, h * GLA_DV:(h + 1) * GLA_DV]
            ms = jnp.mean(oh * oh, axis=-1, keepdims=True)
            outs.append(oh * lax.rsqrt(ms + EPS) * gain[:, h * GLA_DV:(h + 1) * GLA_DV])
        y = jnp.concatenate(outs, axis=1) * rs_ref[rows, :].astype(F32)
        o_ref[rows, :] = y.astype(BF16)


def _gla(gq, gk, gv, la, rs, gn, batch, seq, ts):
    T = gq.shape[0]
    per_b = seq // ts
    tok = lambda b, i: (b * per_b + i, 0)
    HK = GLA_HEADS * GLA_DK
    HV = GLA_HEADS * GLA_DV
    return pl.pallas_call(
        functools.partial(_gla_kernel, nchunk=ts // GLA_CHUNK),
        grid=(batch, per_b),
        in_specs=[
            pl.BlockSpec((ts, HK), tok),
            pl.BlockSpec((ts, HK), tok),
            pl.BlockSpec((ts, HV), tok),
            pl.BlockSpec((ts, HK), tok),
            pl.BlockSpec((ts, HV), tok),
            pl.BlockSpec((1, HV), lambda b, i: (0, 0)),
        ],
        out_specs=pl.BlockSpec((ts, HV), tok),
        out_shape=jax.ShapeDtypeStruct((T, HV), BF16),
        scratch_shapes=[pltpu.VMEM((GLA_HEADS, GLA_DK, GLA_DV), F32)],
        compiler_params=_params(("arbitrary", "arbitrary")),
        name="gla",
    )(gq, gk, gv, la, rs, gn)


def _swa_kernel(q_ref, kc_ref, kp_ref, vc_ref, vp_ref, cc_ref, sc_ref, cp_ref, sp_ref,
                qg_ref, kg_ref, sink_ref, o_ref, *, tq):
    W = WINDOW
    nw = tq // W
    G = SWA_HEADS // SWA_KV_HEADS
    step = pl.program_id(1)

    lane = _iota((1, LANES), 1)
    lo_half = lane < SWA_HD
    first = (lane & (SWA_HD - 1)) < (ROPE_DIMS // 2)
    seg = ((_iota((LANES, LANES), 0) >> 6) == (_iota((LANES, LANES), 1) >> 6))
    seg_mean = jnp.where(seg, 1.0 / SWA_HD, 0.0).astype(BF16)

    def norm_rope(x, cos, sg, gain):
        x2 = x * x
        hi = x2.astype(BF16)
        lo = (x2 - hi.astype(F32)).astype(BF16)
        ms = _dot(hi, seg_mean) + _dot(lo, seg_mean)
        y = x * lax.rsqrt(ms + EPS) * gain
        partner = jnp.where(first, pltpu.roll(y, LANES - ROPE_DIMS // 2, 1),
                            pltpu.roll(y, ROPE_DIMS // 2, 1))
        return y * cos + partner * sg

    cos_c = cc_ref[...]
    sg_c = sc_ref[...]
    cos_k = jnp.concatenate([cp_ref[...], cos_c], axis=0)
    sg_k = jnp.concatenate([sp_ref[...], sg_c], axis=0)
    kb = jnp.concatenate([kp_ref[...], kc_ref[...]], axis=0).astype(F32)
    kb = norm_rope(kb, cos_k, sg_k, kg_ref[...])
    kb_r = pltpu.roll(kb, SWA_HD, 1)
    vb = jnp.concatenate([vp_ref[...], vc_ref[...]], axis=0).astype(F32)
    vb_r = pltpu.roll(vb, SWA_HD, 1)
    k_dup = (jnp.where(lo_half, kb, kb_r).astype(BF16), jnp.where(lo_half, kb_r, kb).astype(BF16))
    v_dup = (jnp.where(lo_half, vb, vb_r).astype(BF16), jnp.where(lo_half, vb_r, vb).astype(BF16))

    q_cols = []
    for c in range(SWA_HEADS // 2):
        qc = q_ref[:, c * LANES:(c + 1) * LANES].astype(F32)
        q_cols.append(norm_rope(qc, cos_c, sg_c, qg_ref[...]) * (SWA_HD ** -0.5))

    cur_side = _iota((W, W), 0) <= _iota((W, W), 1)
    cur_side4 = jnp.concatenate([cur_side] * G, axis=1)
    sinks = sink_ref[...]
    nt = (((1,), (1,)), ((), ()))
    tn = (((0,), (0,)), ((), ()))

    for w in range(nw):
        for kvh in range(SWA_KV_HEADS):
            parts = []
            for cc in range(G // 2):
                qw = q_cols[kvh * (G // 2) + cc][w * W:(w + 1) * W, :]
                parts.append(jnp.where(lo_half, qw, 0.0))
                parts.append(jnp.where(lo_half, 0.0, qw))
            q_stack = jnp.concatenate(parts, axis=0).astype(BF16)
            k_prev = k_dup[kvh][w * W:(w + 1) * W, :]
            k_cur = k_dup[kvh][(w + 1) * W:(w + 2) * W, :]
            v_prev = v_dup[kvh][w * W:(w + 1) * W, :]
            v_cur = v_dup[kvh][(w + 1) * W:(w + 2) * W, :]
            s_cur = lax.dot_general(k_cur, q_stack, nt, preferred_element_type=F32)
            s_prev = lax.dot_general(k_prev, q_stack, nt, preferred_element_type=F32)
            if w == 0:
                s_prev = jnp.where(step > 0, s_prev, -1e30)
            s = jnp.where(cur_side4, s_cur, s_prev)
            sink = jnp.concatenate(
                [sinks[kvh * G + g:kvh * G + g + 1, :] for g in range(G)], axis=1)
            m = jnp.maximum(jnp.max(s, axis=0, keepdims=True), sink)
            p = jnp.exp(s - m)
            den = jnp.sum(p, axis=0, keepdims=True) + jnp.exp(sink - m)
            p = (p * (1.0 / den)).astype(BF16)
            zero = jnp.zeros_like(p)
            o = (lax.dot_general(jnp.where(cur_side4, p, zero), v_cur, tn,
                                 preferred_element_type=F32)
                 + lax.dot_general(jnp.where(cur_side4, zero, p), v_prev, tn,
                                   preferred_element_type=F32))
            og = [o[g * W:(g + 1) * W, :] for g in range(G)]
            for cc in range(G // 2):
                col = jnp.where(lo_half, og[2 * cc], og[2 * cc + 1])
                c0 = (kvh * (G // 2) + cc) * LANES
                o_ref[w * W:(w + 1) * W, c0:c0 + LANES] = col.astype(BF16)


def _swa(sq, skv, cos_t, sg_t, qg, kg, sinks_b, batch, seq, tq):
    T = sq.shape[0]
    per_b = seq // tq
    r = tq // WINDOW
    cur = lambda b, i: (b * per_b + i, 0)
    prev = lambda b, i: (jnp.maximum((b * per_b + i) * r - 1, 0), 0)
    cur_v = lambda b, i: (b * per_b + i, 1)
    prev_v = lambda b, i: (jnp.maximum((b * per_b + i) * r - 1, 0), 1)
    const = lambda b, i: (0, 0)
    QW = SWA_HEADS * SWA_HD
    return pl.pallas_call(
        functools.partial(_swa_kernel, tq=tq),
        grid=(batch, per_b),
        in_specs=[
            pl.BlockSpec((tq, QW), cur),
            pl.BlockSpec((tq, LANES), cur),
            pl.BlockSpec((WINDOW, LANES), prev),
            pl.BlockSpec((tq, LANES), cur_v),
            pl.BlockSpec((WINDOW, LANES), prev_v),
            pl.BlockSpec((tq, LANES), cur),
            pl.BlockSpec((tq, LANES), cur),
            pl.BlockSpec((WINDOW, LANES), prev),
            pl.BlockSpec((WINDOW, LANES), prev),
            pl.BlockSpec((1, LANES), const),
            pl.BlockSpec((1, LANES), const),
            pl.BlockSpec((SWA_HEADS, LANES), const),
        ],
        out_specs=pl.BlockSpec((tq, QW), cur),
        out_shape=jax.ShapeDtypeStruct((T, QW), BF16),
        compiler_params=_params(("arbitrary", "arbitrary")),
        name="swa",
    )(sq, skv, skv, skv, skv, cos_t, sg_t, cos_t, sg_t, qg, kg, sinks_b)


def _second_largest(a, b, c, d):
    hi1, lo1 = jnp.maximum(a, b), jnp.minimum(a, b)
    hi2, lo2 = jnp.maximum(c, d), jnp.minimum(c, d)
    return jnp.maximum(hi1, hi2), jnp.maximum(jnp.minimum(hi1, hi2), jnp.maximum(lo1, lo2))


def _argmax4(vals):
    best, idx = vals[0], jnp.zeros(vals[0].shape, I32)
    for k in range(1, 4):
        upd = vals[k] > best
        idx = jnp.where(upd, k, idx)
        best = jnp.where(upd, vals[k], best)
    return idx, best


def _pick4(idx, vals):
    return jnp.where(idx == 0, vals[0],
                     jnp.where(idx == 1, vals[1], jnp.where(idx == 2, vals[2], vals[3])))


def _postmix_kernel(yg_ref, ys_ref, sa_ref, sb_ref, x_ref, g1_ref, wug_ref, wus_ref, wo_ref,
                    gain_ref, sc_ref, sh_ref, wr_ref, br_ref,
                    x1_ref, h2e_ref, meta_ref, cnt_ref, carry_ref, *, tm):
    @pl.when(pl.program_id(0) == 0)
    def _():
        carry_ref[...] = jnp.zeros_like(carry_ref)

    D = x_ref.shape[1]
    sub = tm // POSTMIX_SUBTILES
    upper = jnp.where(_iota((sub, sub), 0) <= _iota((sub, sub), 1), 1.0, 0.0).astype(BF16)
    carry = carry_ref[...]
    for s in range(POSTMIX_SUBTILES):
        rows = slice(s * sub, (s + 1) * sub)
        u = _dot(yg_ref[rows, :], wug_ref[...])
        v = _dot(ys_ref[rows, :], wus_ref[...])
        merged = sa_ref[rows, :].astype(F32) * u + sb_ref[rows, :].astype(F32) * v
        x1 = x_ref[rows, :] + g1_ref[...] * _dot(merged.astype(BF16), wo_ref[...])
        x1_ref[rows, :] = x1
        ms = jnp.mean(x1 * x1, axis=-1, keepdims=True)
        h2 = x1 * lax.rsqrt(ms + EPS) * gain_ref[...]
        h2 = h2 * (1.0 + sc_ref[...]) + sh_ref[...]
        h2e_ref[rows, 0:D] = h2

        hi = h2.astype(BF16)
        lo = (h2 - hi.astype(F32)).astype(BF16)
        r = _dot(hi, wr_ref[...]) + _dot(lo, wr_ref[...])
        logits = r + pltpu.roll(r, LANES - N_EXPERTS, 1)
        lt = jnp.transpose(logits)[0:N_EXPERTS, :]
        aff = jax.nn.sigmoid(lt)
        sel = aff + br_ref[...]
        aff_r = [aff[e:e + 1, :] for e in range(N_EXPERTS)]
        sel_r = [sel[e:e + 1, :] for e in range(N_EXPERTS)]

        scores = []
        for g in range(N_GROUPS):
            m1, m2 = _second_largest(*sel_r[4 * g:4 * g + 4])
            scores.append(m1 + m2)
        grp, _ = _argmax4(scores)
        sel_g = [_pick4(grp, [sel_r[4 * g + k] for g in range(N_GROUPS)]) for k in range(4)]
        aff_g = [_pick4(grp, [aff_r[4 * g + k] for g in range(N_GROUPS)]) for k in range(4)]
        l1, _ = _argmax4(sel_g)
        masked = [jnp.where(l1 == k, -jnp.inf, sel_g[k]) for k in range(4)]
        l2, _ = _argmax4(masked)
        a1 = _pick4(l1, aff_g)
        a2 = _pick4(l2, aff_g)
        den = a1 + a2
        w1 = a1 / den
        w2 = a2 / den
        lo_e = jnp.minimum(l1, l2)
        hi_e = jnp.maximum(l1, l2)
        pair = jnp.where(lo_e == 0, hi_e - 1, jnp.where(lo_e == 1, hi_e + 1, 5))
        cls = grp * len(PAIRS) + pair
        w_lo = jnp.where(l1 < l2, w1, w2)
        w_hi = jnp.where(l1 < l2, w2, w1)

        onehot = _iota((CLASS_ROWS, sub), 0) == cls
        oh = jnp.where(onehot, 1.0, 0.0).astype(BF16)
        incl = _dot(oh, upper)
        total = _dot(oh, jnp.ones((sub, LANES), BF16))
        base = jnp.concatenate([carry] * (sub // LANES), axis=1)
        rank = jnp.sum(jnp.where(onehot, base + incl, 0.0), axis=0, keepdims=True) - 1.0
        carry = carry + total

        meta_ref[:, rows] = jnp.concatenate(
            [cls, rank.astype(I32), jnp.zeros((SUBLANES - 2, sub), I32)], axis=0)
        w_rows = jnp.concatenate([w_lo, w_hi, jnp.zeros((LANES - 2, sub), F32)], axis=0)
        h2e_ref[rows, D:D + WCOLS] = jnp.transpose(w_rows)
    carry_ref[...] = carry
    cnt_ref[...] = carry


def _postmix(yg, ys, sa, sb, x, g1, wug, wus, wo, gain, sc, sh, wr, br, seq, tm):
    T, D = x.shape
    per_b = seq // tm
    nt = T // tm
    tok = lambda i: (i, 0)
    row = lambda i: (0, 0)
    bat = lambda i: (i // per_b, 0, 0)
    return pl.pallas_call(
        functools.partial(_postmix_kernel, tm=tm),
        grid=(nt,),
        in_specs=[
            pl.BlockSpec((tm, yg.shape[1]), tok),
            pl.BlockSpec((tm, ys.shape[1]), tok),
            pl.BlockSpec((tm, D), tok),
            pl.BlockSpec((tm, D), tok),
            pl.BlockSpec((tm, D), tok),
            pl.BlockSpec((None, 1, D), bat),
            pl.BlockSpec(wug.shape, row),
            pl.BlockSpec(wus.shape, row),
            pl.BlockSpec(wo.shape, row),
            pl.BlockSpec((1, D), row),
            pl.BlockSpec((None, 1, D), bat),
            pl.BlockSpec((None, 1, D), bat),
            pl.BlockSpec((D, LANES), row),
            pl.BlockSpec((N_EXPERTS, 1), row),
        ],
        out_specs=[
            pl.BlockSpec((tm, D), tok),
            pl.BlockSpec((tm, D + WCOLS), tok),
            pl.BlockSpec((None, SUBLANES, tm), lambda i: (i, 0, 0)),
            pl.BlockSpec((CLASS_ROWS, LANES), row),
        ],
        out_shape=[
            jax.ShapeDtypeStruct((T, D), F32),
            jax.ShapeDtypeStruct((T, D + WCOLS), F32),
            jax.ShapeDtypeStruct((nt, SUBLANES, tm), I32),
            jax.ShapeDtypeStruct((CLASS_ROWS, LANES), F32),
        ],
        scratch_shapes=[pltpu.VMEM((CLASS_ROWS, LANES), F32)],
        compiler_params=_params(("arbitrary",), VMEM_LIMIT),
        name="postmix",
    )(yg, ys, sa, sb, x, g1, wug, wus, wo, gain, sc, sh, wr, br)


def _plan_kernel(start_ref, meta_ref, dest_ref):
    cls = meta_ref[0:1, :]
    rank = meta_ref[1:2, :]
    base = jnp.zeros(cls.shape, I32)
    for c in range(N_CLASSES):
        base = jnp.where(cls == c, start_ref[c], base)
    dest_ref[...] = base + rank


def _plan(start, meta):
    nt, _, tm = meta.shape
    return pl.pallas_call(
        _plan_kernel,
        grid_spec=pltpu.PrefetchScalarGridSpec(
            num_scalar_prefetch=1,
            grid=(nt,),
            in_specs=[pl.BlockSpec((None, SUBLANES, tm), lambda i, s: (i, 0, 0))],
            out_specs=pl.BlockSpec((None, 1, tm), lambda i, s: (i, 0, 0)),
        ),
        out_shape=jax.ShapeDtypeStruct((nt, 1, tm), I32),
        compiler_params=_params(("arbitrary",)),
        name="plan",
    )(start, meta)


ROW_UNROLL = 8


def _dispatch_kernel(dest_ref, h_ref, xs_in_ref, xs_ref, sem, *, tm):
    del xs_in_ref

    def row_copy(r, d):
        return pltpu.make_async_copy(h_ref.at[pl.ds(r, 1)], xs_ref.at[pl.ds(d, 1)], sem)

    for r in range(tm):
        row_copy(r, dest_ref[0, r]).start(priority=r % 2)

    def drain(i, carry):
        for u in range(ROW_UNROLL):
            row_copy(0, 0).wait()
        return carry

    lax.fori_loop(0, tm // ROW_UNROLL, drain, 0)


def _dispatch(dest, h2e, xs_zero):
    nt, _, tm = dest.shape
    width = h2e.shape[1]
    return pl.pallas_call(
        functools.partial(_dispatch_kernel, tm=tm),
        grid=(nt,),
        in_specs=[
            pl.BlockSpec((None, 1, tm), lambda i: (i, 0, 0), memory_space=pltpu.SMEM),
            pl.BlockSpec((tm, width), lambda i: (i, 0)),
            pl.BlockSpec(memory_space=pl.ANY),
        ],
        out_specs=pl.BlockSpec(memory_space=pl.ANY),
        out_shape=jax.ShapeDtypeStruct(xs_zero.shape, F32),
        scratch_shapes=[pltpu.SemaphoreType.DMA(())],
        input_output_aliases={2: 0},
        compiler_params=_params(("arbitrary",)),
        name="dispatch",
    )(dest, h2e, xs_zero)


def _moe_kernel(elo_ref, ehi_ref, nact_ref, xs_ref, wg0, wu0, wd0, wg1, wu1, wd1, o_ref):
    j = pl.program_id(0)
    D = o_ref.shape[1]

    @pl.when(j < nact_ref[0])
    def _():
        x = xs_ref[:, 0:D].astype(BF16)
        w_lo = xs_ref[:, D:D + 1]
        w_hi = xs_ref[:, D + 1:D + 2]

        def ffn(wg, wu, wd):
            g = _dot(x, wg[...])
            a = (g * jax.nn.sigmoid(g)) * _dot(x, wu[...])
            return _dot(a.astype(BF16), wd[...])

        o_ref[...] = w_lo * ffn(wg0, wu0, wd0) + w_hi * ffn(wg1, wu1, wd1)

    @pl.when(j >= nact_ref[0])
    def _():
        o_ref[...] = jnp.zeros_like(o_ref)


def _moe(elo, ehi, nact, xs, w_gate, w_up, w_down):
    ns, width = xs.shape
    D = width - WCOLS
    nb = ns // MOE_ROWS
    lo = lambda j, a, b, n: (a[j], 0, 0)
    hi = lambda j, a, b, n: (b[j], 0, 0)
    gu = (None, D, D_FF)
    dn = (None, D_FF, D)
    return pl.pallas_call(
        _moe_kernel,
        grid_spec=pltpu.PrefetchScalarGridSpec(
            num_scalar_prefetch=3,
            grid=(nb,),
            in_specs=[
                pl.BlockSpec((MOE_ROWS, width), lambda j, a, b, n: (j, 0)),
                pl.BlockSpec(gu, lo), pl.BlockSpec(gu, lo), pl.BlockSpec(dn, lo),
                pl.BlockSpec(gu, hi), pl.BlockSpec(gu, hi), pl.BlockSpec(dn, hi),
            ],
            out_specs=pl.BlockSpec((MOE_ROWS, D), lambda j, a, b, n: (j, 0)),
        ),
        out_shape=jax.ShapeDtypeStruct((ns, D), F32),
        compiler_params=_params(("arbitrary",), VMEM_LIMIT),
        name="moe",
    )(elo, ehi, nact, xs, w_gate, w_up, w_down, w_gate, w_up, w_down)


def _combine_kernel(dest_ref, x1_ref, g2_ref, ys_ref, o_ref, buf, sem, *, tm, nt):
    i = pl.program_id(0)

    def row_copy(slot, r, d):
        return pltpu.make_async_copy(ys_ref.at[pl.ds(d, 1)], buf.at[slot, pl.ds(r, 1)],
                                     sem.at[slot])

    @pl.when(i < nt)
    def _():
        slot = lax.rem(i, 2)
        for r in range(tm):
            row_copy(slot, r, dest_ref[0, r]).start(priority=r % 2)

    @pl.when(i > 0)
    def _():
        slot = lax.rem(i + 1, 2)

        def drain(j, carry):
            for u in range(ROW_UNROLL):
                row_copy(slot, 0, 0).wait()
            return carry

        lax.fori_loop(0, tm // ROW_UNROLL, drain, 0)
        o_ref[...] = x1_ref[...] + g2_ref[...] * buf[slot]


def _combine(dest, x1, g2, ys, seq):
    nt, _, tm = dest.shape
    T, D = x1.shape
    per_b = seq // tm
    done = lambda i: jnp.maximum(i - 1, 0)
    return pl.pallas_call(
        functools.partial(_combine_kernel, tm=tm, nt=nt),
        grid=(nt + 1,),
        in_specs=[
            pl.BlockSpec((None, 1, tm), lambda i: (jnp.minimum(i, nt - 1), 0, 0),
                         memory_space=pltpu.SMEM),
            pl.BlockSpec((tm, D), lambda i: (done(i), 0)),
            pl.BlockSpec((None, 1, D), lambda i: (done(i) // per_b, 0, 0)),
            pl.BlockSpec(memory_space=pl.ANY),
        ],
        out_specs=pl.BlockSpec((tm, D), lambda i: (done(i), 0)),
        out_shape=jax.ShapeDtypeStruct((T, D), F32),
        scratch_shapes=[pltpu.VMEM((2, tm, D), F32), pltpu.SemaphoreType.DMA((2,))],
        compiler_params=_params(("arbitrary",)),
        name="combine",
    )(dest, x1, g2, ys)


def _block_plan(counts, nb):
    cnt = counts[:N_CLASSES, 0].astype(I32)
    nblk = (cnt + MOE_ROWS - 1) // MOE_ROWS
    cum = jnp.cumsum(nblk)
    start = ((cum - nblk) * MOE_ROWS).astype(I32)
    nact = cum[-1:].astype(I32)
    blk = jnp.arange(nb, dtype=I32)
    cls = jnp.searchsorted(cum, jnp.minimum(blk, nact[0] - 1), side="right").astype(I32)
    cls = jnp.minimum(cls, N_CLASSES - 1)
    lo_tab = jnp.array([p[0] for p in PAIRS], I32)
    hi_tab = jnp.array([p[1] for p in PAIRS], I32)
    grp = cls // len(PAIRS)
    elo = grp * EXPERTS_PER_GROUP + lo_tab[cls % len(PAIRS)]
    ehi = grp * EXPERTS_PER_GROUP + hi_tab[cls % len(PAIRS)]
    start = jnp.concatenate([start, jnp.zeros((CLASS_ROWS - N_CLASSES,), I32)])
    return start, elo, ehi, nact


def kernel(x, c, positions, w_ada, b_ada, norm_mix, w_in, w_alpha2, b_alpha, gla_norm, q_norm,
           k_norm, sinks, w_up_gla, w_up_swa, w_out, norm_ffn, w_router, b_router, w_gate, w_up,
           w_down):
    B, S, D = x.shape
    L = w_ada.shape[0]
    T = B * S
    tm = min(512, S)

    mod = _modulation(c, w_ada, b_ada).reshape(L, B, 6, 1, D)
    cos_t, sg_t = _rope_tables(positions)

    sizes = (256, 256, 512, 16, 512, 512, 128, 128, 1024, 1024)
    offs = [0]
    for s in sizes:
        offs.append(offs[-1] + s)
    order = (0, 1, 2, 4, 5, 6, 7, 8, 9, 3)
    w_in_p = jnp.concatenate(
        [w_in[:, :, offs[i]:offs[i + 1]] for i in order]
        + [jnp.zeros((L, D, LANES - GLA_RANK), F32)], axis=-1).astype(BF16)
    wa2_p = jnp.concatenate(
        [w_alpha2, jnp.zeros((L, LANES - GLA_RANK, w_alpha2.shape[-1]), F32)], axis=1).astype(BF16)
    wr_hi = w_router.astype(BF16)
    wr_lo = (w_router - wr_hi.astype(F32)).astype(BF16)
    wr = jnp.concatenate([wr_hi, wr_lo, jnp.zeros((D, LANES - 2 * N_EXPERTS), BF16)], axis=1)
    br = b_router.reshape(N_EXPERTS, 1)
    wug = w_up_gla.astype(BF16)
    wus = w_up_swa.astype(BF16)
    wo = w_out.astype(BF16)
    wg = w_gate.astype(BF16)
    wu = w_up.astype(BF16)
    wd = w_down.astype(BF16)

    nb = T // MOE_ROWS + N_CLASSES
    xt = x.reshape(T, D)
    for l in range(L):
        sh1, sc1, g1, sh2, sc2, g2 = [mod[l, :, i] for i in range(6)]
        gq, gk, gv, la, rs, sq, skv, sa, sb = _premix(
            xt, norm_mix[l].reshape(1, D), sc1, sh1, w_in_p[l], wa2_p[l],
            b_alpha[l].reshape(1, -1), S, tm)
        y_gla = _gla(gq, gk, gv, la, rs, gla_norm[l].reshape(1, -1), B, S, tm)
        qg = jnp.tile(q_norm[l], LANES // SWA_HD).reshape(1, LANES)
        kg = jnp.tile(k_norm[l], LANES // SWA_HD).reshape(1, LANES)
        sinks_b = jnp.broadcast_to(sinks[l][:, None], (SWA_HEADS, LANES))
        y_swa = _swa(sq, skv, cos_t, sg_t, qg, kg, sinks_b, B, S, tm)
        x1, h2e, meta, counts = _postmix(
            y_gla, y_swa, sa, sb, xt, g1, wug[l], wus[l], wo[l],
            norm_ffn[l].reshape(1, D), sc2, sh2, wr, br, S, tm)
        start, elo, ehi, nact = _block_plan(counts, nb)
        dest = _plan(start, meta)
        xs = _dispatch(dest, h2e, jnp.zeros((nb * MOE_ROWS, D + WCOLS), F32))
        ys = _moe(elo, ehi, nact, xs, wg[l], wu[l], wd[l])
        xt = _combine(dest, x1, g2, ys, S)
    return xt.reshape(B, S, D)
```

```python
import functools

import jax
import jax.numpy as jnp
from jax import lax
from jax.experimental import pallas as pl
from jax.experimental.pallas import tpu as pltpu

F32 = jnp.float32
BF16 = jnp.bfloat16
I32 = jnp.int32
HIGHEST = lax.Precision.HIGHEST

GLA_HEADS = 4
GLA_DK = 64
GLA_DV = 128
GLA_RANK = 16
GLA_TAU = 16.0
GLA_CHUNK = 64
SWA_HEADS = 8
SWA_KV_HEADS = 2
SWA_HD = 64
WINDOW = 128
ROPE_DIMS = SWA_HD // 4
ROPE_THETA = 500000.0
N_EXPERTS = 16
N_GROUPS = 4
EXPERTS_PER_GROUP = 4
D_FF = 512
EPS = 1e-6

LANES = 128
SUBLANES = 8
VMEM_LIMIT = 56 * 1024 * 1024

PAIRS = ((0, 1), (0, 2), (0, 3), (1, 2), (1, 3), (2, 3))
N_CLASSES = N_GROUPS * len(PAIRS)
CLASS_ROWS = 32
MOE_ROWS = 512
POSTMIX_SUBTILES = 2
PREMIX_SUBTILES = 2
PLAN_TILES = 16
WCOLS = LANES


def _params(sem, vmem=None):
    return pltpu.CompilerParams(dimension_semantics=sem, vmem_limit_bytes=vmem)


def _dot(a, b):
    return jnp.dot(a, b, preferred_element_type=F32)


def _iota(shape, axis):
    return lax.broadcasted_iota(I32, shape, axis)


def _mod_kernel(c_ref, w_ref, b_ref, o_ref):
    c = c_ref[...]
    cond = c * jax.nn.sigmoid(c)
    o_ref[...] = jnp.dot(cond, w_ref[...], precision=HIGHEST,
                         preferred_element_type=F32) + b_ref[...]


def _modulation(c, w_ada, b_ada):
    L, D, D6 = w_ada.shape
    B = c.shape[0]
    nj = D6 // D
    return pl.pallas_call(
        _mod_kernel,
        grid=(L, nj),
        in_specs=[
            pl.BlockSpec((B, D), lambda l, j: (0, 0)),
            pl.BlockSpec((None, D, D), lambda l, j: (l, 0, j)),
            pl.BlockSpec((None, 1, D), lambda l, j: (l, 0, j)),
        ],
        out_specs=pl.BlockSpec((None, B, D), lambda l, j: (l, 0, j)),
        out_shape=jax.ShapeDtypeStruct((L, B, D6), F32),
        compiler_params=_params(("arbitrary", "arbitrary")),
        name="modulation",
    )(c, w_ada, b_ada.reshape(L, 1, D6))


def _rope_kernel(pos_ref, inv_ref, c_ref, s_ref):
    ang = pos_ref[...].astype(F32) * inv_ref[...]
    l64 = _iota(ang.shape, 1) & (SWA_HD - 1)
    half = ROPE_DIMS // 2
    cos = jnp.cos(ang)
    sin = jnp.sin(ang)
    c_ref[...] = jnp.where(l64 < ROPE_DIMS, cos, 1.0)
    s_ref[...] = jnp.where(l64 < half, -sin, jnp.where(l64 < ROPE_DIMS, sin, 0.0))


def _rope_tables(positions):
    T = positions.size
    half = ROPE_DIMS // 2
    inv_freq = jnp.power(ROPE_THETA, -jnp.arange(half, dtype=F32) / half)
    lane = jnp.arange(LANES)
    inv_lane = inv_freq[(lane % SWA_HD) % half].reshape(1, LANES)
    tm = min(1024, T)
    return pl.pallas_call(
        _rope_kernel,
        grid=(T // tm,),
        in_specs=[pl.BlockSpec((tm, 1), lambda i: (i, 0)),
                  pl.BlockSpec((1, LANES), lambda i: (0, 0))],
        out_specs=[pl.BlockSpec((tm, LANES), lambda i: (i, 0))] * 2,
        out_shape=[jax.ShapeDtypeStruct((T, LANES), F32)] * 2,
        compiler_params=_params(("arbitrary",)),
        name="rope_tables",
    )(positions.reshape(T, 1), inv_lane)


_GQ, _GK, _GV, _GR = 0, 256, 512, 1024
_SQ, _SK, _SV = 1536, 2048, 2176
_MA, _MB, _GA, _WIN = 2304, 3328, 4352, 4480


def _premix_kernel(x_ref, gain_ref, sc_ref, sh_ref, w_ref, wa2_ref, ba_ref,
                   gq_ref, gk_ref, gv_ref, la_ref, gr_ref, sq_ref, skv_ref,
                   ma_ref, mb_ref):
    sub = x_ref.shape[0] // PREMIX_SUBTILES
    half = (_MB - _MA) // 2
    for s in range(PREMIX_SUBTILES):
        rows = slice(s * sub, (s + 1) * sub)
        x = x_ref[rows, :]
        ms = jnp.mean(x * x, axis=-1, keepdims=True)
        h = x * lax.rsqrt(ms + EPS) * gain_ref[...]
        h = h * (1.0 + sc_ref[...]) + sh_ref[...]
        hb = h.astype(BF16)

        def sec(lo, hi):
            return _dot(hb, w_ref[:, lo:hi])

        gq_ref[rows, :] = (sec(_GQ, _GK) * (GLA_DK ** -0.5)).astype(BF16)
        gk_ref[rows, :] = sec(_GK, _GV).astype(BF16)
        gv_ref[rows, :] = sec(_GV, _GR).astype(BF16)
        r = sec(_GR, _SQ)
        gr_ref[rows, :] = (r * jax.nn.sigmoid(r)).astype(BF16)
        sq_ref[rows, :] = sec(_SQ, _SK).astype(BF16)
        skv_ref[rows, :] = sec(_SK, _MA).astype(BF16)
        for c in range(2):
            cols = slice(c * half, (c + 1) * half)
            ma_ref[rows, cols] = jax.nn.sigmoid(
                sec(_MA + c * half, _MA + (c + 1) * half)).astype(BF16)
            mb_ref[rows, cols] = jax.nn.sigmoid(
                sec(_MB + c * half, _MB + (c + 1) * half)).astype(BF16)
        a_low = sec(_GA, _WIN).astype(BF16)
        z = _dot(a_low, wa2_ref[...]) + ba_ref[...]
        log_sig = jnp.minimum(z, 0.0) - jnp.log1p(jnp.exp(-jnp.abs(z)))
        la_ref[rows, :] = log_sig * (1.0 / GLA_TAU)


def _premix(x, gain, sc, sh, w_in_p, wa2_p, b_alpha, seq, tm):
    T, D = x.shape
    per_b = seq // tm
    tok = lambda i: (i, 0)
    row = lambda i: (0, 0)
    bat = lambda i: (i // per_b, 0, 0)
    widths = (256, 256, 512, 256, 512, 512, 256, 1024, 1024)
    dts = (BF16, BF16, BF16, F32, BF16, BF16, BF16, BF16, BF16)
    return pl.pallas_call(
        _premix_kernel,
        grid=(T // tm,),
        in_specs=[
            pl.BlockSpec((tm, D), tok),
            pl.BlockSpec((1, D), row),
            pl.BlockSpec((None, 1, D), bat),
            pl.BlockSpec((None, 1, D), bat),
            pl.BlockSpec((D, _WIN), row),
            pl.BlockSpec((LANES, 256), row),
            pl.BlockSpec((1, 256), row),
        ],
        out_specs=[pl.BlockSpec((tm, w), tok) for w in widths],
        out_shape=[jax.ShapeDtypeStruct((T, w), dt) for w, dt in zip(widths, dts)],
        compiler_params=_params(("arbitrary",), VMEM_LIMIT),
        name="premix",
    )(x, gain, sc, sh, w_in_p, wa2_p, b_alpha)


def _gla_kernel(q_ref, k_ref, v_ref, la_ref, rs_ref, gn_ref, o_ref, s_ref, *, nchunk):
    C = GLA_CHUNK
    H = GLA_HEADS
    HK = H * GLA_DK

    @pl.when(pl.program_id(1) == 0)
    def _():
        s_ref[...] = jnp.zeros_like(s_ref)

    tri = jnp.where(_iota((C, C), 1) <= _iota((C, C), 0), 1.0, 0.0).astype(BF16)
    causal = (_iota((C, HK), 1) & (C - 1)) <= _iota((C, HK), 0)
    lane_head = _iota((1, HK), 1) >> 6
    ones = jnp.ones((C, LANES), BF16)
    zero_blk = jnp.zeros((C, GLA_DV), BF16)
    gain = gn_ref[...]
    tn = (((0,), (0,)), ((), ()))
    nt = (((1,), (1,)), ((), ()))

    def block_diag(blocks):
        rows = [jnp.concatenate([zero_blk] * h + [blk] + [zero_blk] * (H - 1 - h), axis=1)
                for h, blk in enumerate(blocks)]
        return jnp.concatenate(rows, axis=0)

    lhs, v_bds, kvs, decays = [], [], [], []
    for c in range(nchunk):
        rows = slice(c * C, (c + 1) * C)
        la = la_ref[rows, :]
        la_hi = la.astype(BF16)
        la_lo = (la - la_hi.astype(F32)).astype(BF16)
        b = _dot(tri, la_hi) + _dot(tri, la_lo)
        d_col = (lax.dot_general(la_hi, ones, tn, preferred_element_type=F32)
                 + lax.dot_general(la_lo, ones, tn, preferred_element_type=F32))
        decays.append(jnp.exp(d_col))
        q = q_ref[rows, :].astype(F32)
        k = k_ref[rows, :].astype(F32)
        v = v_ref[rows, :]
        q_dec = (q * jnp.exp(b)).astype(BF16)
        k_inv = k * jnp.exp(-b)
        k_end = (k_inv * jnp.exp(b[C - 1:C, :])).astype(BF16)
        k_inv = k_inv.astype(BF16)
        k_bd = jnp.concatenate(
            [jnp.where(lane_head == h, k_inv, jnp.zeros_like(k_inv)) for h in range(H)], axis=0)
        att = lax.dot_general(q_dec, k_bd, nt, preferred_element_type=F32)
        att = jnp.where(causal, att, 0.0).astype(BF16)
        lhs.append(jnp.concatenate([att, q_dec], axis=1))
        v_bds.append(block_diag([v[:, h * GLA_DV:(h + 1) * GLA_DV] for h in range(H)]))
        kvs.append([lax.dot_general(k_end[:, p * LANES:(p + 1) * LANES],
                                    v[:, 2 * p * GLA_DV:(2 * p + 2) * GLA_DV], tn,
                                    preferred_element_type=F32) for p in range(H // 2)])

    state = [s_ref[h] for h in range(H)]
    states = []
    for c in range(nchunk):
        states.append(state)
        state = [decays[c][h * GLA_DK:(h + 1) * GLA_DK, :] * state[h]
                 + kvs[c][h // 2][(h % 2) * GLA_DK:(h % 2 + 1) * GLA_DK,
                                  (h % 2) * GLA_DV:(h % 2 + 1) * GLA_DV]
                 for h in range(H)]
    for h in range(H):
        s_ref[h] = state[h]

    for c in range(nchunk):
        rows = slice(c * C, (c + 1) * C)
        rhs = jnp.concatenate(
            [v_bds[c], block_diag([s.astype(BF16) for s in states[c]])], axis=0)
        o = _dot(lhs[c], rhs)
        outs = []
        for h in range(H):
            oh = o[:, h * GLA_DV:(h + 1) * GLA_DV]
            ms = jnp.mean(oh * oh, axis=-1, keepdims=True)
            outs.append(oh * lax.rsqrt(ms + EPS) * gain[:, h * GLA_DV:(h + 1) * GLA_DV])
        y = jnp.concatenate(outs, axis=1) * rs_ref[rows, :].astype(F32)
        o_ref[rows, :] = y.astype(BF16)


def _gla(gq, gk, gv, la, rs, gn, batch, seq, ts):
    T = gq.shape[0]
    per_b = seq // ts
    tok = lambda b, i: (b * per_b + i, 0)
    HK = GLA_HEADS * GLA_DK
    HV = GLA_HEADS * GLA_DV
    return pl.pallas_call(
        functools.partial(_gla_kernel, nchunk=ts // GLA_CHUNK),
        grid=(batch, per_b),
        in_specs=[
            pl.BlockSpec((ts, HK), tok),
            pl.BlockSpec((ts, HK), tok),
            pl.BlockSpec((ts, HV), tok),
            pl.BlockSpec((ts, HK), tok),
            pl.BlockSpec((ts, HV), tok),
            pl.BlockSpec((1, HV), lambda b, i: (0, 0)),
        ],
        out_specs=pl.BlockSpec((ts, HV), tok),
        out_shape=jax.ShapeDtypeStruct((T, HV), BF16),
        scratch_shapes=[pltpu.VMEM((GLA_HEADS, GLA_DK, GLA_DV), F32)],
        compiler_params=_params(("arbitrary", "arbitrary")),
        name="gla",
    )(gq, gk, gv, la, rs, gn)


def _swa_kernel(q_ref, kc_ref, kp_ref, vc_ref, vp_ref, cc_ref, sc_ref, cp_ref, sp_ref,
                qg_ref, kg_ref, sink_ref, o_ref, *, tq):
    W = WINDOW
    nw = tq // W
    G = SWA_HEADS // SWA_KV_HEADS
    step = pl.program_id(1)

    lane = _iota((1, LANES), 1)
    lo_half = lane < SWA_HD
    first = (lane & (SWA_HD - 1)) < (ROPE_DIMS // 2)
    seg = ((_iota((LANES, LANES), 0) >> 6) == (_iota((LANES, LANES), 1) >> 6))
    seg_mean = jnp.where(seg, 1.0 / SWA_HD, 0.0).astype(BF16)

    def norm_rope(x, cos, sg, gain):
        x2 = x * x
        hi = x2.astype(BF16)
        lo = (x2 - hi.astype(F32)).astype(BF16)
        ms = _dot(hi, seg_mean) + _dot(lo, seg_mean)
        y = x * lax.rsqrt(ms + EPS) * gain
        partner = jnp.where(first, pltpu.roll(y, LANES - ROPE_DIMS // 2, 1),
                            pltpu.roll(y, ROPE_DIMS // 2, 1))
        return y * cos + partner * sg

    cos_c = cc_ref[...]
    sg_c = sc_ref[...]
    cos_k = jnp.concatenate([cp_ref[...], cos_c], axis=0)
    sg_k = jnp.concatenate([sp_ref[...], sg_c], axis=0)
    kb = jnp.concatenate([kp_ref[...], kc_ref[...]], axis=0).astype(F32)
    kb = norm_rope(kb, cos_k, sg_k, kg_ref[...])
    kb_r = pltpu.roll(kb, SWA_HD, 1)
    vb = jnp.concatenate([vp_ref[...], vc_ref[...]], axis=0).astype(F32)
    vb_r = pltpu.roll(vb, SWA_HD, 1)
    k_dup = (jnp.where(lo_half, kb, kb_r).astype(BF16), jnp.where(lo_half, kb_r, kb).astype(BF16))
    v_dup = (jnp.where(lo_half, vb, vb_r).astype(BF16), jnp.where(lo_half, vb_r, vb).astype(BF16))

    q_cols = []
    for c in range(SWA_HEADS // 2):
        qc = q_ref[:, c * LANES:(c + 1) * LANES].astype(F32)
        q_cols.append(norm_rope(qc, cos_c, sg_c, qg_ref[...]) * (SWA_HD ** -0.5))

    cur_side = _iota((W, W), 0) <= _iota((W, W), 1)
    cur_side4 = jnp.concatenate([cur_side] * G, axis=1)
    sinks = sink_ref[...]
    nt = (((1,), (1,)), ((), ()))
    tn = (((0,), (0,)), ((), ()))

    for w in range(nw):
        for kvh in range(SWA_KV_HEADS):
            parts = []
            for cc in range(G // 2):
                qw = q_cols[kvh * (G // 2) + cc][w * W:(w + 1) * W, :]
                parts.append(jnp.where(lo_half, qw, 0.0))
                parts.append(jnp.where(lo_half, 0.0, qw))
            q_stack = jnp.concatenate(parts, axis=0).astype(BF16)
            k_prev = k_dup[kvh][w * W:(w + 1) * W, :]
            k_cur = k_dup[kvh][(w + 1) * W:(w + 2) * W, :]
            v_prev = v_dup[kvh][w * W:(w + 1) * W, :]
            v_cur = v_dup[kvh][(w + 1) * W:(w + 2) * W, :]
            s_cur = lax.dot_general(k_cur, q_stack, nt, preferred_element_type=F32)
            s_prev = lax.dot_general(k_prev, q_stack, nt, preferred_element_type=F32)
            if w == 0:
                s_prev = jnp.where(step > 0, s_prev, -1e30)
            s = jnp.where(cur_side4, s_cur, s_prev)
            sink = jnp.concatenate(
                [sinks[kvh * G + g:kvh * G + g + 1, :] for g in range(G)], axis=1)
            m = jnp.maximum(jnp.max(s, axis=0, keepdims=True), sink)
            p = jnp.exp(s - m)
            den = jnp.sum(p, axis=0, keepdims=True) + jnp.exp(sink - m)
            p = (p * (1.0 / den)).astype(BF16)
            zero = jnp.zeros_like(p)
            o = (lax.dot_general(jnp.where(cur_side4, p, zero), v_cur, tn,
                                 preferred_element_type=F32)
                 + lax.dot_general(jnp.where(cur_side4, zero, p), v_prev, tn,
                                   preferred_element_type=F32))
            og = [o[g * W:(g + 1) * W, :] for g in range(G)]
            for cc in range(G // 2):
                col = jnp.where(lo_half, og[2 * cc], og[2 * cc + 1])
                c0 = (kvh * (G // 2) + cc) * LANES
                o_ref[w * W:(w + 1) * W, c0:c0 + LANES] = col.astype(BF16)


def _swa(sq, skv, cos_t, sg_t, qg, kg, sinks_b, batch, seq, tq):
    T = sq.shape[0]
    per_b = seq // tq
    r = tq // WINDOW
    cur = lambda b, i: (b * per_b + i, 0)
    prev = lambda b, i: (jnp.maximum((b * per_b + i) * r - 1, 0), 0)
    cur_v = lambda b, i: (b * per_b + i, 1)
    prev_v = lambda b, i: (jnp.maximum((b * per_b + i) * r - 1, 0), 1)
    const = lambda b, i: (0, 0)
    QW = SWA_HEADS * SWA_HD
    return pl.pallas_call(
        functools.partial(_swa_kernel, tq=tq),
        grid=(batch, per_b),
        in_specs=[
            pl.BlockSpec((tq, QW), cur),
            pl.BlockSpec((tq, LANES), cur),
            pl.BlockSpec((WINDOW, LANES), prev),
            pl.BlockSpec((tq, LANES), cur_v),
            pl.BlockSpec((WINDOW, LANES), prev_v),
            pl.BlockSpec((tq, LANES), cur),
            pl.BlockSpec((tq, LANES), cur),
            pl.BlockSpec((WINDOW, LANES), prev),
            pl.BlockSpec((WINDOW, LANES), prev),
            pl.BlockSpec((1, LANES), const),
            pl.BlockSpec((1, LANES), const),
            pl.BlockSpec((SWA_HEADS, LANES), const),
        ],
        out_specs=pl.BlockSpec((tq, QW), cur),
        out_shape=jax.ShapeDtypeStruct((T, QW), BF16),
        compiler_params=_params(("arbitrary", "arbitrary")),
        name="swa",
    )(sq, skv, skv, skv, skv, cos_t, sg_t, cos_t, sg_t, qg, kg, sinks_b)


def _second_largest(a, b, c, d):
    hi1, lo1 = jnp.maximum(a, b), jnp.minimum(a, b)
    hi2, lo2 = jnp.maximum(c, d), jnp.minimum(c, d)
    return jnp.maximum(hi1, hi2), jnp.maximum(jnp.minimum(hi1, hi2), jnp.maximum(lo1, lo2))


def _argmax4(vals):
    best, idx = vals[0], jnp.zeros(vals[0].shape, I32)
    for k in range(1, 4):
        upd = vals[k] > best
        idx = jnp.where(upd, k, idx)
        best = jnp.where(upd, vals[k], best)
    return idx, best


def _pick4(idx, vals):
    return jnp.where(idx == 0, vals[0],
                     jnp.where(idx == 1, vals[1], jnp.where(idx == 2, vals[2], vals[3])))


def _postmix_kernel(yg_ref, ys_ref, sa_ref, sb_ref, x_ref, g1_ref, wug_ref, wus_ref, wo_ref,
                    gain_ref, sc_ref, sh_ref, wr_ref, br_ref,
                    x1_ref, h2e_ref, cls_ref, rank_ref, cnt_ref, carry_ref, *, tm):
    @pl.when(pl.program_id(0) == 0)
    def _():
        carry_ref[...] = jnp.zeros_like(carry_ref)

    D = x_ref.shape[1]
    sub = tm // POSTMIX_SUBTILES
    upper = jnp.where(_iota((sub, sub), 0) <= _iota((sub, sub), 1), 1.0, 0.0).astype(BF16)
    carry = carry_ref[...]
    scale = gain_ref[...] * (1.0 + sc_ref[...])
    for s in range(POSTMIX_SUBTILES):
        rows = slice(s * sub, (s + 1) * sub)
        u = _dot(yg_ref[rows, :], wug_ref[...]).astype(BF16)
        v = _dot(ys_ref[rows, :], wus_ref[...]).astype(BF16)
        merged = sa_ref[rows, :] * u + sb_ref[rows, :] * v
        x1 = x_ref[rows, :] + g1_ref[...] * _dot(merged, wo_ref[...])
        x1_ref[rows, :] = x1
        ms = jnp.mean(x1 * x1, axis=-1, keepdims=True)
        h2 = x1 * lax.rsqrt(ms + EPS) * scale + sh_ref[...]
        h2e_ref[rows, 0:D] = h2

        logits = _dot(h2.astype(BF16), wr_ref[...])
        aff = jnp.transpose(jax.nn.sigmoid(logits))[0:N_EXPERTS, :]
        sel = aff + br_ref[...]
        aff_r = [aff[e:e + 1, :] for e in range(N_EXPERTS)]
        sel_r = [sel[e:e + 1, :] for e in range(N_EXPERTS)]

        scores = []
        for g in range(N_GROUPS):
            m1, m2 = _second_largest(*sel_r[4 * g:4 * g + 4])
            scores.append(m1 + m2)
        grp, _ = _argmax4(scores)
        sel_g = [_pick4(grp, [sel_r[4 * g + k] for g in range(N_GROUPS)]) for k in range(4)]
        aff_g = [_pick4(grp, [aff_r[4 * g + k] for g in range(N_GROUPS)]) for k in range(4)]
        l1, _ = _argmax4(sel_g)
        masked = [jnp.where(l1 == k, -jnp.inf, sel_g[k]) for k in range(4)]
        l2, _ = _argmax4(masked)
        a1 = _pick4(l1, aff_g)
        a2 = _pick4(l2, aff_g)
        den = a1 + a2
        w1 = a1 / den
        w2 = a2 / den
        lo_e = jnp.minimum(l1, l2)
        hi_e = jnp.maximum(l1, l2)
        pair = jnp.where(lo_e == 0, hi_e - 1, jnp.where(lo_e == 1, hi_e + 1, 5))
        cls = grp * len(PAIRS) + pair
        w_lo = jnp.where(l1 < l2, w1, w2)
        w_hi = jnp.where(l1 < l2, w2, w1)

        onehot = _iota((CLASS_ROWS, sub), 0) == cls
        oh = jnp.where(onehot, 1.0, 0.0).astype(BF16)
        incl = _dot(oh, upper)
        total = _dot(oh, jnp.ones((sub, LANES), BF16))
        base = jnp.concatenate([carry] * (sub // LANES), axis=1)
        rank = jnp.sum(jnp.where(onehot, base + incl, 0.0), axis=0, keepdims=True) - 1.0
        carry = carry + total

        cls_ref[:, rows] = cls
        rank_ref[:, rows] = rank.astype(I32)
        w_rows = jnp.concatenate([w_lo, w_hi, jnp.zeros((LANES - 2, sub), F32)], axis=0)
        h2e_ref[rows, D:D + WCOLS] = jnp.transpose(w_rows)
    carry_ref[...] = carry
    cnt_ref[...] = carry


def _postmix(yg, ys, sa, sb, x, g1, wug, wus, wo, gain, sc, sh, wr, br, seq, tm):
    T, D = x.shape
    per_b = seq // tm
    nt = T // tm
    tok = lambda i: (i, 0)
    row = lambda i: (0, 0)
    bat = lambda i: (i // per_b, 0, 0)
    return pl.pallas_call(
        functools.partial(_postmix_kernel, tm=tm),
        grid=(nt,),
        in_specs=[
            pl.BlockSpec((tm, yg.shape[1]), tok),
            pl.BlockSpec((tm, ys.shape[1]), tok),
            pl.BlockSpec((tm, D), tok),
            pl.BlockSpec((tm, D), tok),
            pl.BlockSpec((tm, D), tok),
            pl.BlockSpec((None, 1, D), bat),
            pl.BlockSpec(wug.shape, row),
            pl.BlockSpec(wus.shape, row),
            pl.BlockSpec(wo.shape, row),
            pl.BlockSpec((1, D), row),
            pl.BlockSpec((None, 1, D), bat),
            pl.BlockSpec((None, 1, D), bat),
            pl.BlockSpec((D, LANES), row),
            pl.BlockSpec((N_EXPERTS, 1), row),
        ],
        out_specs=[
            pl.BlockSpec((tm, D), tok),
            pl.BlockSpec((tm, D + WCOLS), tok),
            pl.BlockSpec((None, 1, tm), lambda i: (i, 0, 0)),
            pl.BlockSpec((None, 1, tm), lambda i: (i, 0, 0)),
            pl.BlockSpec((CLASS_ROWS, LANES), row),
        ],
        out_shape=[
            jax.ShapeDtypeStruct((T, D), F32),
            jax.ShapeDtypeStruct((T, D + WCOLS), F32),
            jax.ShapeDtypeStruct((nt, 1, tm), I32),
            jax.ShapeDtypeStruct((nt, 1, tm), I32),
            jax.ShapeDtypeStruct((CLASS_ROWS, LANES), F32),
        ],
        scratch_shapes=[pltpu.VMEM((CLASS_ROWS, LANES), F32)],
        compiler_params=_params(("arbitrary",), VMEM_LIMIT),
        name="postmix",
    )(yg, ys, sa, sb, x, g1, wug, wus, wo, gain, sc, sh, wr, br)


def _plan_kernel(start_ref, cls_ref, rank_ref, dest_ref):
    cls = cls_ref[...]
    base = jnp.zeros(cls.shape, I32)
    for c in range(N_CLASSES):
        base = jnp.where(cls == c, start_ref[c], base)
    dest_ref[...] = base + rank_ref[...]


def _plan(start, cls, rank):
    nt, _, tm = cls.shape
    g = min(PLAN_TILES, nt)
    blk = pl.BlockSpec((g, 1, tm), lambda i, s: (i, 0, 0))
    return pl.pallas_call(
        _plan_kernel,
        grid_spec=pltpu.PrefetchScalarGridSpec(
            num_scalar_prefetch=1,
            grid=(nt // g,),
            in_specs=[blk, blk],
            out_specs=blk,
        ),
        out_shape=jax.ShapeDtypeStruct((nt, 1, tm), I32),
        compiler_params=_params(("arbitrary",)),
        name="plan",
    )(start, cls, rank)


ROW_UNROLL = 8


def _dispatch_kernel(dest_ref, h_ref, xs_in_ref, xs_ref, sem, *, tm):
    del xs_in_ref

    def row_copy(r, d):
        return pltpu.make_async_copy(h_ref.at[pl.ds(r, 1)], xs_ref.at[pl.ds(d, 1)], sem)

    for r in range(tm):
        row_copy(r, dest_ref[0, r]).start(priority=r % 2)

    def drain(i, carry):
        for u in range(ROW_UNROLL):
            row_copy(0, 0).wait()
        return carry

    lax.fori_loop(0, tm // ROW_UNROLL, drain, 0)


def _dispatch(dest, h2e, xs_zero):
    nt, _, tm = dest.shape
    width = h2e.shape[1]
    return pl.pallas_call(
        functools.partial(_dispatch_kernel, tm=tm),
        grid=(nt,),
        in_specs=[
            pl.BlockSpec((None, 1, tm), lambda i: (i, 0, 0), memory_space=pltpu.SMEM),
            pl.BlockSpec((tm, width), lambda i: (i, 0)),
            pl.BlockSpec(memory_space=pl.ANY),
        ],
        out_specs=pl.BlockSpec(memory_space=pl.ANY),
        out_shape=jax.ShapeDtypeStruct(xs_zero.shape, F32),
        scratch_shapes=[pltpu.SemaphoreType.DMA(())],
        input_output_aliases={2: 0},
        compiler_params=_params(("arbitrary",)),
        name="dispatch",
    )(dest, h2e, xs_zero)


def _moe_kernel(elo_ref, ehi_ref, nact_ref, xs_ref, wg0, wu0, wd0, wg1, wu1, wd1, o_ref):
    j = pl.program_id(0)
    D = o_ref.shape[1]

    @pl.when(j < nact_ref[0])
    def _():
        x = xs_ref[:, 0:D].astype(BF16)
        w_lo = xs_ref[:, D:D + 1]
        w_hi = xs_ref[:, D + 1:D + 2]

        def ffn(wg, wu, wd):
            g = _dot(x, wg[...])
            a = (g * jax.nn.sigmoid(g)) * _dot(x, wu[...])
            return _dot(a.astype(BF16), wd[...])

        o_ref[...] = w_lo * ffn(wg0, wu0, wd0) + w_hi * ffn(wg1, wu1, wd1)

    @pl.when(j >= nact_ref[0])
    def _():
        o_ref[...] = jnp.zeros_like(o_ref)


def _moe(elo, ehi, nact, xs, w_gate, w_up, w_down, layer):
    ns, width = xs.shape
    D = width - WCOLS
    nb = ns // MOE_ROWS
    lo = lambda j, a, b, n: (layer, a[j], 0, 0)
    hi = lambda j, a, b, n: (layer, b[j], 0, 0)
    gu = (None, None, D, D_FF)
    dn = (None, None, D_FF, D)
    return pl.pallas_call(
        _moe_kernel,
        grid_spec=pltpu.PrefetchScalarGridSpec(
            num_scalar_prefetch=3,
            grid=(nb,),
            in_specs=[
                pl.BlockSpec((MOE_ROWS, width), lambda j, a, b, n: (j, 0)),
                pl.BlockSpec(gu, lo), pl.BlockSpec(gu, lo), pl.BlockSpec(dn, lo),
                pl.BlockSpec(gu, hi), pl.BlockSpec(gu, hi), pl.BlockSpec(dn, hi),
            ],
            out_specs=pl.BlockSpec((MOE_ROWS, D), lambda j, a, b, n: (j, 0)),
        ),
        out_shape=jax.ShapeDtypeStruct((ns, D), F32),
        compiler_params=_params(("arbitrary",), VMEM_LIMIT),
        name="moe",
    )(elo, ehi, nact, xs, w_gate, w_up, w_down, w_gate, w_up, w_down)


def _combine_kernel(dest_ref, x1_ref, g2_ref, ys_ref, o_ref, buf, sem, *, tm, nt):
    i = pl.program_id(0)

    def row_copy(slot, r, d):
        return pltpu.make_async_copy(ys_ref.at[pl.ds(d, 1)], buf.at[slot, pl.ds(r, 1)],
                                     sem.at[slot])

    @pl.when(i < nt)
    def _():
        slot = lax.rem(i, 2)
        for r in range(tm):
            row_copy(slot, r, dest_ref[0, r]).start(priority=r % 2)

    @pl.when(i > 0)
    def _():
        slot = lax.rem(i + 1, 2)

        def drain(j, carry):
            for u in range(ROW_UNROLL):
                row_copy(slot, 0, 0).wait()
            return carry

        lax.fori_loop(0, tm // ROW_UNROLL, drain, 0)
        o_ref[...] = x1_ref[...] + g2_ref[...] * buf[slot]


def _combine(dest, x1, g2, ys, seq):
    nt, _, tm = dest.shape
    T, D = x1.shape
    per_b = seq // tm
    done = lambda i: jnp.maximum(i - 1, 0)
    return pl.pallas_call(
        functools.partial(_combine_kernel, tm=tm, nt=nt),
        grid=(nt + 1,),
        in_specs=[
            pl.BlockSpec((None, 1, tm), lambda i: (jnp.minimum(i, nt - 1), 0, 0),
                         memory_space=pltpu.SMEM),
            pl.BlockSpec((tm, D), lambda i: (done(i), 0)),
            pl.BlockSpec((None, 1, D), lambda i: (done(i) // per_b, 0, 0)),
            pl.BlockSpec(memory_space=pl.ANY),
        ],
        out_specs=pl.BlockSpec((tm, D), lambda i: (done(i), 0)),
        out_shape=jax.ShapeDtypeStruct((T, D), F32),
        scratch_shapes=[pltpu.VMEM((2, tm, D), F32), pltpu.SemaphoreType.DMA((2,))],
        compiler_params=_params(("arbitrary",)),
        name="combine",
    )(dest, x1, g2, ys)


def _block_plan(counts, nb):
    cnt = counts[:N_CLASSES, 0].astype(I32)
    nblk = (cnt + MOE_ROWS - 1) // MOE_ROWS
    cum = jnp.cumsum(nblk)
    start = ((cum - nblk) * MOE_ROWS).astype(I32)
    nact = cum[-1:].astype(I32)
    blk = jnp.arange(nb, dtype=I32)
    cls = jnp.sum(jnp.minimum(blk, nact[0] - 1)[:, None] >= cum[None, :], axis=1).astype(I32)
    cls = jnp.minimum(cls, N_CLASSES - 1)
    lo_tab = jnp.array([p[0] for p in PAIRS], I32)
    hi_tab = jnp.array([p[1] for p in PAIRS], I32)
    grp = cls // len(PAIRS)
    elo = grp * EXPERTS_PER_GROUP + lo_tab[cls % len(PAIRS)]
    ehi = grp * EXPERTS_PER_GROUP + hi_tab[cls % len(PAIRS)]
    start = jnp.concatenate([start, jnp.zeros((CLASS_ROWS - N_CLASSES,), I32)])
    return start, elo, ehi, nact


def kernel(x, c, positions, w_ada, b_ada, norm_mix, w_in, w_alpha2, b_alpha, gla_norm, q_norm,
           k_norm, sinks, w_up_gla, w_up_swa, w_out, norm_ffn, w_router, b_router, w_gate, w_up,
           w_down):
    B, S, D = x.shape
    L = w_ada.shape[0]
    T = B * S
    tm = min(512, S)

    mod = _modulation(c, w_ada, b_ada).reshape(L, B, 6, 1, D)
    cos_t, sg_t = _rope_tables(positions)

    sizes = (256, 256, 512, 16, 512, 512, 128, 128, 1024, 1024)
    offs = [0]
    for s in sizes:
        offs.append(offs[-1] + s)
    order = (0, 1, 2, 4, 5, 6, 7, 8, 9, 3)
    w_in_p = jnp.concatenate(
        [w_in[:, :, offs[i]:offs[i + 1]] for i in order]
        + [jnp.zeros((L, D, LANES - GLA_RANK), F32)], axis=-1).astype(BF16)
    wa2_p = jnp.concatenate(
        [w_alpha2, jnp.zeros((L, LANES - GLA_RANK, w_alpha2.shape[-1]), F32)], axis=1).astype(BF16)
    wr = jnp.concatenate(
        [w_router.astype(BF16), jnp.zeros((D, LANES - N_EXPERTS), BF16)], axis=1)
    br = b_router.reshape(N_EXPERTS, 1)
    wug = w_up_gla.astype(BF16)
    wus = w_up_swa.astype(BF16)
    wo = w_out.astype(BF16)
    wg = w_gate.astype(BF16)
    wu = w_up.astype(BF16)
    wd = w_down.astype(BF16)

    nb = T // MOE_ROWS + N_CLASSES
    xs = jnp.zeros((nb * MOE_ROWS, D + WCOLS), F32)
    xt = x.reshape(T, D)
    for l in range(L):
        sh1, sc1, g1, sh2, sc2, g2 = [mod[l, :, i] for i in range(6)]
        gq, gk, gv, la, rs, sq, skv, sa, sb = _premix(
            xt, norm_mix[l].reshape(1, D), sc1, sh1, w_in_p[l], wa2_p[l],
            b_alpha[l].reshape(1, -1), S, tm)
        y_gla = _gla(gq, gk, gv, la, rs, gla_norm[l].reshape(1, -1), B, S, tm)
        qg = jnp.tile(q_norm[l], LANES // SWA_HD).reshape(1, LANES)
        kg = jnp.tile(k_norm[l], LANES // SWA_HD).reshape(1, LANES)
        sinks_b = jnp.broadcast_to(sinks[l][:, None], (SWA_HEADS, LANES))
        y_swa = _swa(sq, skv, cos_t, sg_t, qg, kg, sinks_b, B, S, tm)
        x1, h2e, cls, rank, counts = _postmix(
            y_gla, y_swa, sa, sb, xt, g1, wug[l], wus[l], wo[l],
            norm_ffn[l].reshape(1, D), sc2, sh2, wr, br, S, tm)
        start, elo, ehi, nact = _block_plan(counts, nb)
        dest = _plan(start, cls, rank)
        xs = _dispatch(dest, h2e, xs)
        ys = _moe(elo, ehi, nact, xs, wg, wu, wd, l)
        xt = _combine(dest, x1, g2, ys, S)
    return xt.reshape(B, S, D)
```

```python
import functools

import jax
import jax.numpy as jnp
from jax import lax
from jax.experimental import pallas as pl
from jax.experimental.pallas import tpu as pltpu

F32 = jnp.float32
BF16 = jnp.bfloat16
I32 = jnp.int32
HIGHEST = lax.Precision.HIGHEST

GLA_HEADS = 4
GLA_DK = 64
GLA_DV = 128
GLA_RANK = 16
GLA_TAU = 16.0
GLA_CHUNK = 64
SWA_HEADS = 8
SWA_KV_HEADS = 2
SWA_HD = 64
WINDOW = 128
ROPE_DIMS = SWA_HD // 4
ROPE_THETA = 500000.0
N_EXPERTS = 16
N_GROUPS = 4
EXPERTS_PER_GROUP = 4
D_FF = 512
EPS = 1e-6

LANES = 128
SUBLANES = 8
VMEM_LIMIT = 56 * 1024 * 1024

PAIRS = ((0, 1), (0, 2), (0, 3), (1, 2), (1, 3), (2, 3))
N_CLASSES = N_GROUPS * len(PAIRS)
CLASS_ROWS = 32
MOE_ROWS = 512
POSTMIX_SUBTILES = 2
PREMIX_SUBTILES = 2
GLA_TILE = 1024
PLAN_TILES = 16
WCOLS = LANES


def _params(sem, vmem=None):
    return pltpu.CompilerParams(dimension_semantics=sem, vmem_limit_bytes=vmem)


def _dot(a, b):
    return jnp.dot(a, b, preferred_element_type=F32)


def _iota(shape, axis):
    return lax.broadcasted_iota(I32, shape, axis)


def _mod_kernel(c_ref, w_ref, b_ref, o_ref):
    c = c_ref[...]
    cond = c * jax.nn.sigmoid(c)
    o_ref[...] = jnp.dot(cond, w_ref[...], precision=HIGHEST,
                         preferred_element_type=F32) + b_ref[...]


def _modulation(c, w_ada, b_ada):
    L, D, D6 = w_ada.shape
    B = c.shape[0]
    nj = D6 // D
    return pl.pallas_call(
        _mod_kernel,
        grid=(L, nj),
        in_specs=[
            pl.BlockSpec((B, D), lambda l, j: (0, 0)),
            pl.BlockSpec((None, D, D), lambda l, j: (l, 0, j)),
            pl.BlockSpec((None, 1, D), lambda l, j: (l, 0, j)),
        ],
        out_specs=pl.BlockSpec((None, B, D), lambda l, j: (l, 0, j)),
        out_shape=jax.ShapeDtypeStruct((L, B, D6), F32),
        compiler_params=_params(("arbitrary", "arbitrary")),
        name="modulation",
    )(c, w_ada, b_ada.reshape(L, 1, D6))


def _rope_kernel(pos_ref, inv_ref, c_ref, s_ref):
    ang = pos_ref[...].astype(F32) * inv_ref[...]
    l64 = _iota(ang.shape, 1) & (SWA_HD - 1)
    half = ROPE_DIMS // 2
    cos = jnp.cos(ang)
    sin = jnp.sin(ang)
    c_ref[...] = jnp.where(l64 < ROPE_DIMS, cos, 1.0)
    s_ref[...] = jnp.where(l64 < half, -sin, jnp.where(l64 < ROPE_DIMS, sin, 0.0))


def _rope_tables(positions):
    T = positions.size
    half = ROPE_DIMS // 2
    inv_freq = jnp.power(ROPE_THETA, -jnp.arange(half, dtype=F32) / half)
    lane = jnp.arange(LANES)
    inv_lane = inv_freq[(lane % SWA_HD) % half].reshape(1, LANES)
    tm = min(1024, T)
    return pl.pallas_call(
        _rope_kernel,
        grid=(T // tm,),
        in_specs=[pl.BlockSpec((tm, 1), lambda i: (i, 0)),
                  pl.BlockSpec((1, LANES), lambda i: (0, 0))],
        out_specs=[pl.BlockSpec((tm, LANES), lambda i: (i, 0))] * 2,
        out_shape=[jax.ShapeDtypeStruct((T, LANES), F32)] * 2,
        compiler_params=_params(("arbitrary",)),
        name="rope_tables",
    )(positions.reshape(T, 1), inv_lane)


_GQ, _GK, _GV, _GR = 0, 256, 512, 1024
_SQ, _SK, _SV = 1536, 2048, 2176
_MA, _MB, _GA, _WIN = 2304, 3328, 4352, 4480


def _premix_rows(x, rows, gain_ref, sc_ref, sh_ref, w_ref, wa2_ref, ba_ref,
                 gq_ref, gk_ref, gv_ref, la_ref, gr_ref, sq_ref, skv_ref, ma_ref, mb_ref):
    half = (_MB - _MA) // 2
    ms = jnp.mean(x * x, axis=-1, keepdims=True)
    h = x * lax.rsqrt(ms + EPS) * gain_ref[...]
    h = h * (1.0 + sc_ref[...]) + sh_ref[...]
    hb = h.astype(BF16)

    def sec(lo, hi):
        return _dot(hb, w_ref[:, lo:hi])

    gq_ref[rows, :] = (sec(_GQ, _GK) * (GLA_DK ** -0.5)).astype(BF16)
    gk_ref[rows, :] = sec(_GK, _GV).astype(BF16)
    gv_ref[rows, :] = sec(_GV, _GR).astype(BF16)
    r = sec(_GR, _SQ)
    gr_ref[rows, :] = (r * jax.nn.sigmoid(r)).astype(BF16)
    sq_ref[rows, :] = sec(_SQ, _SK).astype(BF16)
    skv_ref[rows, :] = sec(_SK, _MA).astype(BF16)
    for c in range(2):
        cols = slice(c * half, (c + 1) * half)
        ma_ref[rows, cols] = jax.nn.sigmoid(
            sec(_MA + c * half, _MA + (c + 1) * half)).astype(BF16)
        mb_ref[rows, cols] = jax.nn.sigmoid(
            sec(_MB + c * half, _MB + (c + 1) * half)).astype(BF16)
    a_low = sec(_GA, _WIN).astype(BF16)
    z = _dot(a_low, wa2_ref[...]) + ba_ref[...]
    log_sig = jnp.minimum(z, 0.0) - jnp.log1p(jnp.exp(-jnp.abs(z)))
    la_ref[rows, :] = log_sig * (1.0 / GLA_TAU)


def _premix_kernel(x_ref, *refs):
    sub = x_ref.shape[0] // PREMIX_SUBTILES
    for s in range(PREMIX_SUBTILES):
        rows = slice(s * sub, (s + 1) * sub)
        _premix_rows(x_ref[rows, :], rows, *refs)


def _premix_combine_kernel(dnext_ref, dfirst_ref, x1_ref, g2_ref, ys_ref, *refs, nt):
    refs, x_ref, buf, sem = refs[:-3], refs[-3], refs[-2], refs[-1]
    i = pl.program_id(0)
    tm = x1_ref.shape[0]
    slot = lax.rem(i, 2)

    def row_copy(s_, r, d):
        return pltpu.make_async_copy(ys_ref.at[pl.ds(d, 1)], buf.at[s_, pl.ds(r, 1)],
                                     sem.at[s_])

    def issue(dest_ref, s_):
        for r in range(tm):
            row_copy(s_, r, dest_ref[0, r]).start(priority=r % 2)

    def drain(s_):
        def body(j, carry):
            for u in range(ROW_UNROLL):
                row_copy(s_, 0, 0).wait()
            return carry
        lax.fori_loop(0, tm // ROW_UNROLL, body, 0)

    @pl.when(i == 0)
    def _():
        issue(dfirst_ref, 0)

    drain(slot)
    x_ref[...] = x1_ref[...] + g2_ref[...] * buf[slot]
    issue(dnext_ref, 1 - slot)
    sub = tm // PREMIX_SUBTILES
    for s in range(PREMIX_SUBTILES):
        rows = slice(s * sub, (s + 1) * sub)
        _premix_rows(x_ref[rows, :], rows, *refs)

    @pl.when(i == nt - 1)
    def _():
        drain(1 - slot)


_PREMIX_WIDTHS = (256, 256, 512, 256, 512, 512, 256, 1024, 1024)
_PREMIX_DTYPES = (BF16, BF16, BF16, F32, BF16, BF16, BF16, BF16, BF16)


def _premix_weight_specs(D):
    row = lambda i: (0, 0)
    return [pl.BlockSpec((D, _WIN), row), pl.BlockSpec((LANES, 256), row),
            pl.BlockSpec((1, 256), row)]


def _premix(x, gain, sc, sh, w_in_p, wa2_p, b_alpha, seq, tm):
    T, D = x.shape
    per_b = seq // tm
    tok = lambda i: (i, 0)
    bat = lambda i: (i // per_b, 0, 0)
    return pl.pallas_call(
        _premix_kernel,
        grid=(T // tm,),
        in_specs=[
            pl.BlockSpec((tm, D), tok),
            pl.BlockSpec((1, D), lambda i: (0, 0)),
            pl.BlockSpec((None, 1, D), bat),
            pl.BlockSpec((None, 1, D), bat),
        ] + _premix_weight_specs(D),
        out_specs=[pl.BlockSpec((tm, w), tok) for w in _PREMIX_WIDTHS],
        out_shape=[jax.ShapeDtypeStruct((T, w), dt)
                   for w, dt in zip(_PREMIX_WIDTHS, _PREMIX_DTYPES)],
        compiler_params=_params(("arbitrary",), VMEM_LIMIT),
        name="premix",
    )(x, gain, sc, sh, w_in_p, wa2_p, b_alpha)


def _premix_combine(dest, x1, g2, ys, gain, sc, sh, w_in_p, wa2_p, b_alpha, seq):
    nt, _, tm = dest.shape
    T, D = x1.shape
    per_b = seq // tm
    tok = lambda i: (i, 0)
    bat = lambda i: (i // per_b, 0, 0)
    smem = pltpu.SMEM
    widths = _PREMIX_WIDTHS + (D,)
    dtypes = _PREMIX_DTYPES + (F32,)
    return pl.pallas_call(
        functools.partial(_premix_combine_kernel, nt=nt),
        grid=(nt,),
        in_specs=[
            pl.BlockSpec((None, 1, tm), lambda i: (jnp.minimum(i + 1, nt - 1), 0, 0),
                         memory_space=smem),
            pl.BlockSpec((None, 1, tm), lambda i: (0, 0, 0), memory_space=smem),
            pl.BlockSpec((tm, D), tok),
            pl.BlockSpec((None, 1, D), bat),
            pl.BlockSpec(memory_space=pl.ANY),
            pl.BlockSpec((1, D), lambda i: (0, 0)),
            pl.BlockSpec((None, 1, D), bat),
            pl.BlockSpec((None, 1, D), bat),
        ] + _premix_weight_specs(D),
        out_specs=[pl.BlockSpec((tm, w), tok) for w in widths],
        out_shape=[jax.ShapeDtypeStruct((T, w), dt) for w, dt in zip(widths, dtypes)],
        scratch_shapes=[pltpu.VMEM((2, tm, D), F32), pltpu.SemaphoreType.DMA((2,))],
        compiler_params=_params(("arbitrary",), VMEM_LIMIT),
        name="premix_combine",
    )(dest, dest, x1, g2, ys, gain, sc, sh, w_in_p, wa2_p, b_alpha)


def _gla_kernel(q_ref, k_ref, v_ref, la_ref, rs_ref, gn_ref, o_ref, s_ref, *, nchunk):
    C = GLA_CHUNK
    H = GLA_HEADS
    HK = H * GLA_DK

    @pl.when(pl.program_id(1) == 0)
    def _():
        s_ref[...] = jnp.zeros_like(s_ref)

    tri = jnp.where(_iota((C, C), 1) <= _iota((C, C), 0), 1.0, 0.0).astype(BF16)
    causal = (_iota((C, HK), 1) & (C - 1)) <= _iota((C, HK), 0)
    lane_head = _iota((1, HK), 1) >> 6
    ones = jnp.ones((C, LANES), BF16)
    zero_blk = jnp.zeros((C, GLA_DV), BF16)
    gain = gn_ref[...]
    tn = (((0,), (0,)), ((), ()))
    nt = (((1,), (1,)), ((), ()))

    def block_diag(blocks):
        rows = [jnp.concatenate([zero_blk] * h + [blk] + [zero_blk] * (H - 1 - h), axis=1)
                for h, blk in enumerate(blocks)]
        return jnp.concatenate(rows, axis=0)

    lhs, v_bds, kvs, decays = [], [], [], []
    for c in range(nchunk):
        rows = slice(c * C, (c + 1) * C)
        la = la_ref[rows, :]
        la_hi = la.astype(BF16)
        la_lo = (la - la_hi.astype(F32)).astype(BF16)
        b = _dot(tri, la_hi) + _dot(tri, la_lo)
        d_col = (lax.dot_general(la_hi, ones, tn, preferred_element_type=F32)
                 + lax.dot_general(la_lo, ones, tn, preferred_element_type=F32))
        decays.append(jnp.exp(d_col))
        q = q_ref[rows, :].astype(F32)
        k = k_ref[rows, :].astype(F32)
        v = v_ref[rows, :]
        q_dec = (q * jnp.exp(b)).astype(BF16)
        k_inv = k * jnp.exp(-b)
        k_end = (k_inv * jnp.exp(b[C - 1:C, :])).astype(BF16)
        k_inv = k_inv.astype(BF16)
        k_bd = jnp.concatenate(
            [jnp.where(lane_head == h, k_inv, jnp.zeros_like(k_inv)) for h in range(H)], axis=0)
        att = lax.dot_general(q_dec, k_bd, nt, preferred_element_type=F32)
        att = jnp.where(causal, att, 0.0).astype(BF16)
        lhs.append(jnp.concatenate([att, q_dec], axis=1))
        v_bds.append(block_diag([v[:, h * GLA_DV:(h + 1) * GLA_DV] for h in range(H)]))
        kvs.append([lax.dot_general(k_end[:, p * LANES:(p + 1) * LANES],
                                    v[:, 2 * p * GLA_DV:(2 * p + 2) * GLA_DV], tn,
                                    preferred_element_type=F32) for p in range(H // 2)])

    state = [s_ref[h] for h in range(H)]
    states = []
    for c in range(nchunk):
        states.append(state)
        state = [decays[c][h * GLA_DK:(h + 1) * GLA_DK, :] * state[h]
                 + kvs[c][h // 2][(h % 2) * GLA_DK:(h % 2 + 1) * GLA_DK,
                                  (h % 2) * GLA_DV:(h % 2 + 1) * GLA_DV]
                 for h in range(H)]
    for h in range(H):
        s_ref[h] = state[h]

    for c in range(nchunk):
        rows = slice(c * C, (c + 1) * C)
        rhs = jnp.concatenate(
            [v_bds[c], block_diag([s.astype(BF16) for s in states[c]])], axis=0)
        o = _dot(lhs[c], rhs)
        outs = []
        for h in range(H):
            oh = o[:, h * GLA_DV:(h + 1) * GLA_DV]
            ms = jnp.mean(oh * oh, axis=-1, keepdims=True)
            outs.append(oh * lax.rsqrt(ms + EPS) * gain[:, h * GLA_DV:(h + 1) * GLA_DV])
        y = jnp.concatenate(outs, axis=1) * rs_ref[rows, :].astype(F32)
        o_ref[rows, :] = y.astype(BF16)


def _gla(gq, gk, gv, la, rs, gn, batch, seq, ts):
    T = gq.shape[0]
    per_b = seq // ts
    tok = lambda b, i: (b * per_b + i, 0)
    HK = GLA_HEADS * GLA_DK
    HV = GLA_HEADS * GLA_DV
    return pl.pallas_call(
        functools.partial(_gla_kernel, nchunk=ts // GLA_CHUNK),
        grid=(batch, per_b),
        in_specs=[
            pl.BlockSpec((ts, HK), tok),
            pl.BlockSpec((ts, HK), tok),
            pl.BlockSpec((ts, HV), tok),
            pl.BlockSpec((ts, HK), tok),
            pl.BlockSpec((ts, HV), tok),
            pl.BlockSpec((1, HV), lambda b, i: (0, 0)),
        ],
        out_specs=pl.BlockSpec((ts, HV), tok),
        out_shape=jax.ShapeDtypeStruct((T, HV), BF16),
        scratch_shapes=[pltpu.VMEM((GLA_HEADS, GLA_DK, GLA_DV), F32)],
        compiler_params=_params(("arbitrary", "arbitrary")),
        name="gla",
    )(gq, gk, gv, la, rs, gn)


def _swa_kernel(q_ref, kc_ref, kp_ref, vc_ref, vp_ref, cc_ref, sc_ref, cp_ref, sp_ref,
                qg_ref, kg_ref, sink_ref, o_ref, *, tq):
    W = WINDOW
    nw = tq // W
    G = SWA_HEADS // SWA_KV_HEADS
    step = pl.program_id(1)

    lane = _iota((1, LANES), 1)
    lo_half = lane < SWA_HD
    first = (lane & (SWA_HD - 1)) < (ROPE_DIMS // 2)
    seg = ((_iota((LANES, LANES), 0) >> 6) == (_iota((LANES, LANES), 1) >> 6))
    seg_mean = jnp.where(seg, 1.0 / SWA_HD, 0.0).astype(BF16)

    def norm_rope(x, cos, sg, gain):
        ms = _dot((x * x).astype(BF16), seg_mean)
        y = x * lax.rsqrt(ms + EPS) * gain
        partner = jnp.where(first, pltpu.roll(y, LANES - ROPE_DIMS // 2, 1),
                            pltpu.roll(y, ROPE_DIMS // 2, 1))
        return y * cos + partner * sg

    cos_c = cc_ref[...]
    sg_c = sc_ref[...]
    cos_k = jnp.concatenate([cp_ref[...], cos_c], axis=0)
    sg_k = jnp.concatenate([sp_ref[...], sg_c], axis=0)
    kb = jnp.concatenate([kp_ref[...], kc_ref[...]], axis=0).astype(F32)
    kb = norm_rope(kb, cos_k, sg_k, kg_ref[...])
    kb_r = pltpu.roll(kb, SWA_HD, 1)
    vb = jnp.concatenate([vp_ref[...], vc_ref[...]], axis=0).astype(F32)
    vb_r = pltpu.roll(vb, SWA_HD, 1)
    k_dup = (jnp.where(lo_half, kb, kb_r).astype(BF16), jnp.where(lo_half, kb_r, kb).astype(BF16))
    v_dup = (jnp.where(lo_half, vb, vb_r).astype(BF16), jnp.where(lo_half, vb_r, vb).astype(BF16))

    q_cols = []
    for c in range(SWA_HEADS // 2):
        qc = q_ref[:, c * LANES:(c + 1) * LANES].astype(F32)
        q_cols.append(norm_rope(qc, cos_c, sg_c, qg_ref[...]) * (SWA_HD ** -0.5))

    cur_side = _iota((W, W), 0) <= _iota((W, W), 1)
    cur_side4 = jnp.concatenate([cur_side] * G, axis=1)
    sinks = sink_ref[...]
    nt = (((1,), (1,)), ((), ()))
    tn = (((0,), (0,)), ((), ()))

    for w in range(nw):
        for kvh in range(SWA_KV_HEADS):
            parts = []
            for cc in range(G // 2):
                qw = q_cols[kvh * (G // 2) + cc][w * W:(w + 1) * W, :]
                parts.append(jnp.where(lo_half, qw, 0.0))
                parts.append(jnp.where(lo_half, 0.0, qw))
            q_stack = jnp.concatenate(parts, axis=0).astype(BF16)
            k_prev = k_dup[kvh][w * W:(w + 1) * W, :]
            k_cur = k_dup[kvh][(w + 1) * W:(w + 2) * W, :]
            v_prev = v_dup[kvh][w * W:(w + 1) * W, :]
            v_cur = v_dup[kvh][(w + 1) * W:(w + 2) * W, :]
            s_cur = lax.dot_general(k_cur, q_stack, nt, preferred_element_type=F32)
            s_prev = lax.dot_general(k_prev, q_stack, nt, preferred_element_type=F32)
            if w == 0:
                s_prev = jnp.where(step > 0, s_prev, -1e30)
            s = jnp.where(cur_side4, s_cur, s_prev)
            sink = jnp.concatenate(
                [sinks[kvh * G + g:kvh * G + g + 1, :] for g in range(G)], axis=1)
            m = jnp.maximum(jnp.max(s, axis=0, keepdims=True), sink)
            p = jnp.exp(s - m)
            den = jnp.sum(p, axis=0, keepdims=True) + jnp.exp(sink - m)
            p = (p * (1.0 / den)).astype(BF16)
            zero = jnp.zeros_like(p)
            o = (lax.dot_general(jnp.where(cur_side4, p, zero), v_cur, tn,
                                 preferred_element_type=F32)
                 + lax.dot_general(jnp.where(cur_side4, zero, p), v_prev, tn,
                                   preferred_element_type=F32))
            og = [o[g * W:(g + 1) * W, :] for g in range(G)]
            for cc in range(G // 2):
                col = jnp.where(lo_half, og[2 * cc], og[2 * cc + 1])
                c0 = (kvh * (G // 2) + cc) * LANES
                o_ref[w * W:(w + 1) * W, c0:c0 + LANES] = col.astype(BF16)


def _swa(sq, skv, cos_t, sg_t, qg, kg, sinks_b, batch, seq, tq):
    T = sq.shape[0]
    per_b = seq // tq
    r = tq // WINDOW
    cur = lambda b, i: (b * per_b + i, 0)
    prev = lambda b, i: (jnp.maximum((b * per_b + i) * r - 1, 0), 0)
    cur_v = lambda b, i: (b * per_b + i, 1)
    prev_v = lambda b, i: (jnp.maximum((b * per_b + i) * r - 1, 0), 1)
    const = lambda b, i: (0, 0)
    QW = SWA_HEADS * SWA_HD
    return pl.pallas_call(
        functools.partial(_swa_kernel, tq=tq),
        grid=(batch, per_b),
        in_specs=[
            pl.BlockSpec((tq, QW), cur),
            pl.BlockSpec((tq, LANES), cur),
            pl.BlockSpec((WINDOW, LANES), prev),
            pl.BlockSpec((tq, LANES), cur_v),
            pl.BlockSpec((WINDOW, LANES), prev_v),
            pl.BlockSpec((tq, LANES), cur),
            pl.BlockSpec((tq, LANES), cur),
            pl.BlockSpec((WINDOW, LANES), prev),
            pl.BlockSpec((WINDOW, LANES), prev),
            pl.BlockSpec((1, LANES), const),
            pl.BlockSpec((1, LANES), const),
            pl.BlockSpec((SWA_HEADS, LANES), const),
        ],
        out_specs=pl.BlockSpec((tq, QW), cur),
        out_shape=jax.ShapeDtypeStruct((T, QW), BF16),
        compiler_params=_params(("arbitrary", "arbitrary")),
        name="swa",
    )(sq, skv, skv, skv, skv, cos_t, sg_t, cos_t, sg_t, qg, kg, sinks_b)


def _second_largest(a, b, c, d):
    hi1, lo1 = jnp.maximum(a, b), jnp.minimum(a, b)
    hi2, lo2 = jnp.maximum(c, d), jnp.minimum(c, d)
    return jnp.maximum(hi1, hi2), jnp.maximum(jnp.minimum(hi1, hi2), jnp.maximum(lo1, lo2))


def _argmax4(vals):
    best, idx = vals[0], jnp.zeros(vals[0].shape, I32)
    for k in range(1, 4):
        upd = vals[k] > best
        idx = jnp.where(upd, k, idx)
        best = jnp.where(upd, vals[k], best)
    return idx, best


def _pick4(idx, vals):
    return jnp.where(idx == 0, vals[0],
                     jnp.where(idx == 1, vals[1], jnp.where(idx == 2, vals[2], vals[3])))


def _postmix_kernel(yg_ref, ys_ref, sa_ref, sb_ref, x_ref, g1_ref, wug_ref, wus_ref, wo_ref,
                    gain_ref, sc_ref, sh_ref, wr_ref, br_ref,
                    x1_ref, h2e_ref, cls_ref, rank_ref, cnt_ref, carry_ref, *, tm):
    @pl.when(pl.program_id(0) == 0)
    def _():
        carry_ref[...] = jnp.zeros_like(carry_ref)

    D = x_ref.shape[1]
    sub = tm // POSTMIX_SUBTILES
    upper = jnp.where(_iota((sub, sub), 0) <= _iota((sub, sub), 1), 1.0, 0.0).astype(BF16)
    carry = carry_ref[...]
    scale = gain_ref[...] * (1.0 + sc_ref[...])
    for s in range(POSTMIX_SUBTILES):
        rows = slice(s * sub, (s + 1) * sub)
        u = _dot(yg_ref[rows, :], wug_ref[...]).astype(BF16)
        v = _dot(ys_ref[rows, :], wus_ref[...]).astype(BF16)
        merged = sa_ref[rows, :] * u + sb_ref[rows, :] * v
        x1 = x_ref[rows, :] + g1_ref[...] * _dot(merged, wo_ref[...])
        x1_ref[rows, :] = x1
        ms = jnp.mean(x1 * x1, axis=-1, keepdims=True)
        h2 = x1 * lax.rsqrt(ms + EPS) * scale + sh_ref[...]
        h2e_ref[rows, 0:D] = h2

        logits = _dot(h2.astype(BF16), wr_ref[...])
        aff = jnp.transpose(jax.nn.sigmoid(logits))[0:N_EXPERTS, :]
        sel = aff + br_ref[...]
        aff_r = [aff[e:e + 1, :] for e in range(N_EXPERTS)]
        sel_r = [sel[e:e + 1, :] for e in range(N_EXPERTS)]

        scores = []
        for g in range(N_GROUPS):
            m1, m2 = _second_largest(*sel_r[4 * g:4 * g + 4])
            scores.append(m1 + m2)
        grp, _ = _argmax4(scores)
        sel_g = [_pick4(grp, [sel_r[4 * g + k] for g in range(N_GROUPS)]) for k in range(4)]
        aff_g = [_pick4(grp, [aff_r[4 * g + k] for g in range(N_GROUPS)]) for k in range(4)]
        l1, _ = _argmax4(sel_g)
        masked = [jnp.where(l1 == k, -jnp.inf, sel_g[k]) for k in range(4)]
        l2, _ = _argmax4(masked)
        a1 = _pick4(l1, aff_g)
        a2 = _pick4(l2, aff_g)
        den = a1 + a2
        w1 = a1 / den
        w2 = a2 / den
        lo_e = jnp.minimum(l1, l2)
        hi_e = jnp.maximum(l1, l2)
        pair = jnp.where(lo_e == 0, hi_e - 1, jnp.where(lo_e == 1, hi_e + 1, 5))
        cls = grp * len(PAIRS) + pair
        w_lo = jnp.where(l1 < l2, w1, w2)
        w_hi = jnp.where(l1 < l2, w2, w1)

        onehot = _iota((CLASS_ROWS, sub), 0) == cls
        oh = jnp.where(onehot, 1.0, 0.0).astype(BF16)
        incl = _dot(oh, upper)
        total = _dot(oh, jnp.ones((sub, LANES), BF16))
        base = jnp.concatenate([carry] * (sub // LANES), axis=1)
        rank = jnp.sum(jnp.where(onehot, base + incl, 0.0), axis=0, keepdims=True) - 1.0
        carry = carry + total

        cls_ref[:, rows] = cls
        rank_ref[:, rows] = rank.astype(I32)
        w_rows = jnp.concatenate([w_lo, w_hi, jnp.zeros((LANES - 2, sub), F32)], axis=0)
        h2e_ref[rows, D:D + WCOLS] = jnp.transpose(w_rows)
    carry_ref[...] = carry
    cnt_ref[...] = carry


def _postmix(yg, ys, sa, sb, x, g1, wug, wus, wo, gain, sc, sh, wr, br, seq, tm):
    T, D = x.shape
    per_b = seq // tm
    nt = T // tm
    tok = lambda i: (i, 0)
    row = lambda i: (0, 0)
    bat = lambda i: (i // per_b, 0, 0)
    return pl.pallas_call(
        functools.partial(_postmix_kernel, tm=tm),
        grid=(nt,),
        in_specs=[
            pl.BlockSpec((tm, yg.shape[1]), tok),
            pl.BlockSpec((tm, ys.shape[1]), tok),
            pl.BlockSpec((tm, D), tok),
            pl.BlockSpec((tm, D), tok),
            pl.BlockSpec((tm, D), tok),
            pl.BlockSpec((None, 1, D), bat),
            pl.BlockSpec(wug.shape, row),
            pl.BlockSpec(wus.shape, row),
            pl.BlockSpec(wo.shape, row),
            pl.BlockSpec((1, D), row),
            pl.BlockSpec((None, 1, D), bat),
            pl.BlockSpec((None, 1, D), bat),
            pl.BlockSpec((D, LANES), row),
            pl.BlockSpec((N_EXPERTS, 1), row),
        ],
        out_specs=[
            pl.BlockSpec((tm, D), tok),
            pl.BlockSpec((tm, D + WCOLS), tok),
            pl.BlockSpec((None, 1, tm), lambda i: (i, 0, 0)),
            pl.BlockSpec((None, 1, tm), lambda i: (i, 0, 0)),
            pl.BlockSpec((CLASS_ROWS, LANES), row),
        ],
        out_shape=[
            jax.ShapeDtypeStruct((T, D), F32),
            jax.ShapeDtypeStruct((T, D + WCOLS), F32),
            jax.ShapeDtypeStruct((nt, 1, tm), I32),
            jax.ShapeDtypeStruct((nt, 1, tm), I32),
            jax.ShapeDtypeStruct((CLASS_ROWS, LANES), F32),
        ],
        scratch_shapes=[pltpu.VMEM((CLASS_ROWS, LANES), F32)],
        compiler_params=_params(("arbitrary",), VMEM_LIMIT),
        name="postmix",
    )(yg, ys, sa, sb, x, g1, wug, wus, wo, gain, sc, sh, wr, br)


def _plan_kernel(start_ref, cls_ref, rank_ref, dest_ref):
    cls = cls_ref[...]
    base = jnp.zeros(cls.shape, I32)
    for c in range(N_CLASSES):
        base = jnp.where(cls == c, start_ref[c], base)
    dest_ref[...] = base + rank_ref[...]


def _plan(start, cls, rank):
    nt, _, tm = cls.shape
    g = min(PLAN_TILES, nt)
    blk = pl.BlockSpec((g, 1, tm), lambda i, s: (i, 0, 0))
    return pl.pallas_call(
        _plan_kernel,
        grid_spec=pltpu.PrefetchScalarGridSpec(
            num_scalar_prefetch=1,
            grid=(nt // g,),
            in_specs=[blk, blk],
            out_specs=blk,
        ),
        out_shape=jax.ShapeDtypeStruct((nt, 1, tm), I32),
        compiler_params=_params(("arbitrary",)),
        name="plan",
    )(start, cls, rank)


ROW_UNROLL = 8


def _dispatch_kernel(dest_ref, h_ref, xs_in_ref, xs_ref, sem, *, tm):
    del xs_in_ref

    def row_copy(r, d):
        return pltpu.make_async_copy(h_ref.at[pl.ds(r, 1)], xs_ref.at[pl.ds(d, 1)], sem)

    for r in range(tm):
        row_copy(r, dest_ref[0, r]).start(priority=r % 2)

    def drain(i, carry):
        for u in range(ROW_UNROLL):
            row_copy(0, 0).wait()
        return carry

    lax.fori_loop(0, tm // ROW_UNROLL, drain, 0)


def _dispatch(dest, h2e, xs_zero):
    nt, _, tm = dest.shape
    width = h2e.shape[1]
    return pl.pallas_call(
        functools.partial(_dispatch_kernel, tm=tm),
        grid=(nt,),
        in_specs=[
            pl.BlockSpec((None, 1, tm), lambda i: (i, 0, 0), memory_space=pltpu.SMEM),
            pl.BlockSpec((tm, width), lambda i: (i, 0)),
            pl.BlockSpec(memory_space=pl.ANY),
        ],
        out_specs=pl.BlockSpec(memory_space=pl.ANY),
        out_shape=jax.ShapeDtypeStruct(xs_zero.shape, F32),
        scratch_shapes=[pltpu.SemaphoreType.DMA(())],
        input_output_aliases={2: 0},
        compiler_params=_params(("arbitrary",)),
        name="dispatch",
    )(dest, h2e, xs_zero)


def _moe_kernel(elo_ref, ehi_ref, nact_ref, xs_ref, wg0, wu0, wd0, wg1, wu1, wd1, o_ref):
    j = pl.program_id(0)
    D = o_ref.shape[1]

    @pl.when(j < nact_ref[0])
    def _():
        x = xs_ref[:, 0:D].astype(BF16)
        w_lo = xs_ref[:, D:D + 1]
        w_hi = xs_ref[:, D + 1:D + 2]

        def ffn(wg, wu, wd):
            g = _dot(x, wg[...])
            a = (g * jax.nn.sigmoid(g)) * _dot(x, wu[...])
            return _dot(a.astype(BF16), wd[...])

        o_ref[...] = w_lo * ffn(wg0, wu0, wd0) + w_hi * ffn(wg1, wu1, wd1)

    @pl.when(j >= nact_ref[0])
    def _():
        o_ref[...] = jnp.zeros_like(o_ref)


def _moe(elo, ehi, nact, xs, w_gate, w_up, w_down, layer):
    ns, width = xs.shape
    D = width - WCOLS
    nb = ns // MOE_ROWS
    lo = lambda j, a, b, n: (layer, a[j], 0, 0)
    hi = lambda j, a, b, n: (layer, b[j], 0, 0)
    gu = (None, None, D, D_FF)
    dn = (None, None, D_FF, D)
    return pl.pallas_call(
        _moe_kernel,
        grid_spec=pltpu.PrefetchScalarGridSpec(
            num_scalar_prefetch=3,
            grid=(nb,),
            in_specs=[
                pl.BlockSpec((MOE_ROWS, width), lambda j, a, b, n: (j, 0)),
                pl.BlockSpec(gu, lo), pl.BlockSpec(gu, lo), pl.BlockSpec(dn, lo),
                pl.BlockSpec(gu, hi), pl.BlockSpec(gu, hi), pl.BlockSpec(dn, hi),
            ],
            out_specs=pl.BlockSpec((MOE_ROWS, D), lambda j, a, b, n: (j, 0)),
        ),
        out_shape=jax.ShapeDtypeStruct((ns, D), F32),
        compiler_params=_params(("arbitrary",), VMEM_LIMIT),
        name="moe",
    )(elo, ehi, nact, xs, w_gate, w_up, w_down, w_gate, w_up, w_down)


def _combine_kernel(dest_ref, x1_ref, g2_ref, ys_ref, o_ref, buf, sem, *, tm, nt):
    i = pl.program_id(0)

    def row_copy(slot, r, d):
        return pltpu.make_async_copy(ys_ref.at[pl.ds(d, 1)], buf.at[slot, pl.ds(r, 1)],
                                     sem.at[slot])

    @pl.when(i < nt)
    def _():
        slot = lax.rem(i, 2)
        for r in range(tm):
            row_copy(slot, r, dest_ref[0, r]).start(priority=r % 2)

    @pl.when(i > 0)
    def _():
        slot = lax.rem(i + 1, 2)

        def drain(j, carry):
            for u in range(ROW_UNROLL):
                row_copy(slot, 0, 0).wait()
            return carry

        lax.fori_loop(0, tm // ROW_UNROLL, drain, 0)
        o_ref[...] = x1_ref[...] + g2_ref[...] * buf[slot]


def _combine(dest, x1, g2, ys, seq):
    nt, _, tm = dest.shape
    T, D = x1.shape
    per_b = seq // tm
    done = lambda i: jnp.maximum(i - 1, 0)
    return pl.pallas_call(
        functools.partial(_combine_kernel, tm=tm, nt=nt),
        grid=(nt + 1,),
        in_specs=[
            pl.BlockSpec((None, 1, tm), lambda i: (jnp.minimum(i, nt - 1), 0, 0),
                         memory_space=pltpu.SMEM),
            pl.BlockSpec((tm, D), lambda i: (done(i), 0)),
            pl.BlockSpec((None, 1, D), lambda i: (done(i) // per_b, 0, 0)),
            pl.BlockSpec(memory_space=pl.ANY),
        ],
        out_specs=pl.BlockSpec((tm, D), lambda i: (done(i), 0)),
        out_shape=jax.ShapeDtypeStruct((T, D), F32),
        scratch_shapes=[pltpu.VMEM((2, tm, D), F32), pltpu.SemaphoreType.DMA((2,))],
        compiler_params=_params(("arbitrary",)),
        name="combine",
    )(dest, x1, g2, ys)


def _block_plan(counts, nb):
    cnt = counts[:N_CLASSES, 0].astype(I32)
    nblk = (cnt + MOE_ROWS - 1) // MOE_ROWS
    cum = jnp.cumsum(nblk)
    start = ((cum - nblk) * MOE_ROWS).astype(I32)
    nact = cum[-1:].astype(I32)
    blk = jnp.arange(nb, dtype=I32)
    cls = jnp.sum(jnp.minimum(blk, nact[0] - 1)[:, None] >= cum[None, :], axis=1).astype(I32)
    cls = jnp.minimum(cls, N_CLASSES - 1)
    lo_tab = jnp.array([p[0] for p in PAIRS], I32)
    hi_tab = jnp.array([p[1] for p in PAIRS], I32)
    grp = cls // len(PAIRS)
    elo = grp * EXPERTS_PER_GROUP + lo_tab[cls % len(PAIRS)]
    ehi = grp * EXPERTS_PER_GROUP + hi_tab[cls % len(PAIRS)]
    start = jnp.concatenate([start, jnp.zeros((CLASS_ROWS - N_CLASSES,), I32)])
    return start, elo, ehi, nact


def kernel(x, c, positions, w_ada, b_ada, norm_mix, w_in, w_alpha2, b_alpha, gla_norm, q_norm,
           k_norm, sinks, w_up_gla, w_up_swa, w_out, norm_ffn, w_router, b_router, w_gate, w_up,
           w_down):
    B, S, D = x.shape
    L = w_ada.shape[0]
    T = B * S
    tm = min(512, S)

    mod = _modulation(c, w_ada, b_ada).reshape(L, B, 6, 1, D)
    cos_t, sg_t = _rope_tables(positions)

    sizes = (256, 256, 512, 16, 512, 512, 128, 128, 1024, 1024)
    offs = [0]
    for s in sizes:
        offs.append(offs[-1] + s)
    order = (0, 1, 2, 4, 5, 6, 7, 8, 9, 3)
    w_in_p = jnp.concatenate(
        [w_in[:, :, offs[i]:offs[i + 1]] for i in order]
        + [jnp.zeros((L, D, LANES - GLA_RANK), F32)], axis=-1).astype(BF16)
    wa2_p = jnp.concatenate(
        [w_alpha2, jnp.zeros((L, LANES - GLA_RANK, w_alpha2.shape[-1]), F32)], axis=1).astype(BF16)
    wr = jnp.concatenate(
        [w_router.astype(BF16), jnp.zeros((D, LANES - N_EXPERTS), BF16)], axis=1)
    br = b_router.reshape(N_EXPERTS, 1)
    wug = w_up_gla.astype(BF16)
    wus = w_up_swa.astype(BF16)
    wo = w_out.astype(BF16)
    wg = w_gate.astype(BF16)
    wu = w_up.astype(BF16)
    wd = w_down.astype(BF16)

    nb = T // MOE_ROWS + N_CLASSES
    xs = jnp.zeros((nb * MOE_ROWS, D + WCOLS), F32)
    xt = x.reshape(T, D)
    pending = None
    for l in range(L):
        sh1, sc1, g1, sh2, sc2, g2 = [mod[l, :, i] for i in range(6)]
        premix_args = (norm_mix[l].reshape(1, D), sc1, sh1, w_in_p[l], wa2_p[l],
                       b_alpha[l].reshape(1, -1))
        if pending is None:
            gq, gk, gv, la, rs, sq, skv, sa, sb = _premix(xt, *premix_args, S, tm)
        else:
            gq, gk, gv, la, rs, sq, skv, sa, sb, xt = _premix_combine(*pending, *premix_args, S)
        y_gla = _gla(gq, gk, gv, la, rs, gla_norm[l].reshape(1, -1), B, S, min(GLA_TILE, S))
        qg = jnp.tile(q_norm[l], LANES // SWA_HD).reshape(1, LANES)
        kg = jnp.tile(k_norm[l], LANES // SWA_HD).reshape(1, LANES)
        sinks_b = jnp.broadcast_to(sinks[l][:, None], (SWA_HEADS, LANES))
        y_swa = _swa(sq, skv, cos_t, sg_t, qg, kg, sinks_b, B, S, tm)
        x1, h2e, cls, rank, counts = _postmix(
            y_gla, y_swa, sa, sb, xt, g1, wug[l], wus[l], wo[l],
            norm_ffn[l].reshape(1, D), sc2, sh2, wr, br, S, tm)
        start, elo, ehi, nact = _block_plan(counts, nb)
        dest = _plan(start, cls, rank)
        xs = _dispatch(dest, h2e, xs)
        ys = _moe(elo, ehi, nact, xs, wg, wu, wd, l)
        pending = (dest, x1, g2, ys)
    xt = _combine(*pending, S)
    return xt.reshape(B, S, D)
```

```python
import functools

import jax
import jax.numpy as jnp
from jax import lax
from jax.experimental import pallas as pl
from jax.experimental.pallas import tpu as pltpu

F32 = jnp.float32
BF16 = jnp.bfloat16
I32 = jnp.int32
HIGHEST = lax.Precision.HIGHEST

GLA_HEADS = 4
GLA_DK = 64
GLA_DV = 128
GLA_RANK = 16
GLA_TAU = 16.0
GLA_CHUNK = 64
SWA_HEADS = 8
SWA_KV_HEADS = 2
SWA_HD = 64
WINDOW = 128
ROPE_DIMS = SWA_HD // 4
ROPE_THETA = 500000.0
N_EXPERTS = 16
N_GROUPS = 4
EXPERTS_PER_GROUP = 4
D_FF = 512
EPS = 1e-6

LANES = 128
SUBLANES = 8
VMEM_LIMIT = 56 * 1024 * 1024

PAIRS = ((0, 1), (0, 2), (0, 3), (1, 2), (1, 3), (2, 3))
N_CLASSES = N_GROUPS * len(PAIRS)
CLASS_ROWS = 32
MOE_ROWS = 512
POSTMIX_SUBTILES = 2
PREMIX_SUBTILES = 2
GLA_TILE = 1024
PLAN_TILES = 16
WCOLS = LANES


def _params(sem, vmem=None):
    return pltpu.CompilerParams(dimension_semantics=sem, vmem_limit_bytes=vmem)


def _dot(a, b):
    return jnp.dot(a, b, preferred_element_type=F32)


def _iota(shape, axis):
    return lax.broadcasted_iota(I32, shape, axis)


def _mod_kernel(c_ref, w_ref, b_ref, o_ref):
    c = c_ref[...]
    cond = c * jax.nn.sigmoid(c)
    o_ref[...] = jnp.dot(cond, w_ref[...], precision=HIGHEST,
                         preferred_element_type=F32) + b_ref[...]


def _modulation(c, w_ada, b_ada):
    L, D, D6 = w_ada.shape
    B = c.shape[0]
    nj = D6 // D
    return pl.pallas_call(
        _mod_kernel,
        grid=(L, nj),
        in_specs=[
            pl.BlockSpec((B, D), lambda l, j: (0, 0)),
            pl.BlockSpec((None, D, D), lambda l, j: (l, 0, j)),
            pl.BlockSpec((None, 1, D), lambda l, j: (l, 0, j)),
        ],
        out_specs=pl.BlockSpec((None, B, D), lambda l, j: (l, 0, j)),
        out_shape=jax.ShapeDtypeStruct((L, B, D6), F32),
        compiler_params=_params(("arbitrary", "arbitrary")),
        name="modulation",
    )(c, w_ada, b_ada.reshape(L, 1, D6))


def _rope_kernel(pos_ref, inv_ref, c_ref, s_ref):
    ang = pos_ref[...].astype(F32) * inv_ref[...]
    l64 = _iota(ang.shape, 1) & (SWA_HD - 1)
    half = ROPE_DIMS // 2
    cos = jnp.cos(ang)
    sin = jnp.sin(ang)
    c_ref[...] = jnp.where(l64 < ROPE_DIMS, cos, 1.0)
    s_ref[...] = jnp.where(l64 < half, -sin, jnp.where(l64 < ROPE_DIMS, sin, 0.0))


def _rope_tables(positions):
    T = positions.size
    half = ROPE_DIMS // 2
    inv_freq = jnp.power(ROPE_THETA, -jnp.arange(half, dtype=F32) / half)
    lane = jnp.arange(LANES)
    inv_lane = inv_freq[(lane % SWA_HD) % half].reshape(1, LANES)
    tm = min(1024, T)
    return pl.pallas_call(
        _rope_kernel,
        grid=(T // tm,),
        in_specs=[pl.BlockSpec((tm, 1), lambda i: (i, 0)),
                  pl.BlockSpec((1, LANES), lambda i: (0, 0))],
        out_specs=[pl.BlockSpec((tm, LANES), lambda i: (i, 0))] * 2,
        out_shape=[jax.ShapeDtypeStruct((T, LANES), F32)] * 2,
        compiler_params=_params(("arbitrary",)),
        name="rope_tables",
    )(positions.reshape(T, 1), inv_lane)


_GQ, _GK, _GV, _GR = 0, 256, 512, 1024
_SQ, _SK, _SV = 1536, 2048, 2176
_MA, _MB, _GA, _WIN = 2304, 3328, 4352, 4480


def _premix_rows(x, rows, gain_ref, sc_ref, sh_ref, w_ref, wa2_ref, ba_ref,
                 gq_ref, gk_ref, gv_ref, la_ref, gr_ref, sq_ref, skv_ref, ma_ref, mb_ref):
    half = (_MB - _MA) // 2
    ms = jnp.mean(x * x, axis=-1, keepdims=True)
    h = x * lax.rsqrt(ms + EPS) * gain_ref[...]
    h = h * (1.0 + sc_ref[...]) + sh_ref[...]
    hb = h.astype(BF16)

    def sec(lo, hi):
        return _dot(hb, w_ref[:, lo:hi])

    gq_ref[rows, :] = (sec(_GQ, _GK) * (GLA_DK ** -0.5)).astype(BF16)
    gk_ref[rows, :] = sec(_GK, _GV).astype(BF16)
    gv_ref[rows, :] = sec(_GV, _GR).astype(BF16)
    r = sec(_GR, _SQ)
    gr_ref[rows, :] = (r * jax.nn.sigmoid(r)).astype(BF16)
    sq_ref[rows, :] = sec(_SQ, _SK).astype(BF16)
    skv_ref[rows, :] = sec(_SK, _MA).astype(BF16)
    for c in range(2):
        cols = slice(c * half, (c + 1) * half)
        ma_ref[rows, cols] = jax.nn.sigmoid(
            sec(_MA + c * half, _MA + (c + 1) * half)).astype(BF16)
        mb_ref[rows, cols] = jax.nn.sigmoid(
            sec(_MB + c * half, _MB + (c + 1) * half)).astype(BF16)
    a_low = sec(_GA, _WIN).astype(BF16)
    z = _dot(a_low, wa2_ref[...]) + ba_ref[...]
    log_sig = jnp.minimum(z, 0.0) - jnp.log1p(jnp.exp(-jnp.abs(z)))
    la_ref[rows, :] = log_sig * (1.0 / GLA_TAU)


def _premix_kernel(x_ref, *refs):
    sub = x_ref.shape[0] // PREMIX_SUBTILES
    for s in range(PREMIX_SUBTILES):
        rows = slice(s * sub, (s + 1) * sub)
        _premix_rows(x_ref[rows, :], rows, *refs)


def _premix_combine_kernel(dnext_ref, dfirst_ref, x1_ref, g2_ref, ys_ref, *refs, nt):
    refs, x_ref, buf, sem = refs[:-3], refs[-3], refs[-2], refs[-1]
    i = pl.program_id(0)
    tm = x1_ref.shape[0]
    slot = lax.rem(i, 2)

    def row_copy(s_, r, d):
        return pltpu.make_async_copy(ys_ref.at[pl.ds(d, 1)], buf.at[s_, pl.ds(r, 1)],
                                     sem.at[s_])

    def issue(dest_ref, s_):
        for r in range(tm):
            row_copy(s_, r, dest_ref[0, r]).start(priority=r % 2)

    def drain(s_):
        def body(j, carry):
            for u in range(ROW_UNROLL):
                row_copy(s_, 0, 0).wait()
            return carry
        lax.fori_loop(0, tm // ROW_UNROLL, body, 0)

    @pl.when(i == 0)
    def _():
        issue(dfirst_ref, 0)

    drain(slot)
    for s_ in range(2):
        @pl.when(slot != s_)
        def _():
            issue(dnext_ref, s_)
    x_ref[...] = x1_ref[...] + g2_ref[...] * buf[slot]
    sub = tm // PREMIX_SUBTILES
    for s in range(PREMIX_SUBTILES):
        rows = slice(s * sub, (s + 1) * sub)
        _premix_rows(x_ref[rows, :], rows, *refs)

    @pl.when(i == nt - 1)
    def _():
        drain(1 - slot)


_PREMIX_WIDTHS = (256, 256, 512, 256, 512, 512, 256, 1024, 1024)
_PREMIX_DTYPES = (BF16, BF16, BF16, F32, BF16, BF16, BF16, BF16, BF16)


def _premix_weight_specs(D):
    row = lambda i: (0, 0)
    return [pl.BlockSpec((D, _WIN), row), pl.BlockSpec((LANES, 256), row),
            pl.BlockSpec((1, 256), row)]


def _premix(x, gain, sc, sh, w_in_p, wa2_p, b_alpha, seq, tm):
    T, D = x.shape
    per_b = seq // tm
    tok = lambda i: (i, 0)
    bat = lambda i: (i // per_b, 0, 0)
    return pl.pallas_call(
        _premix_kernel,
        grid=(T // tm,),
        in_specs=[
            pl.BlockSpec((tm, D), tok),
            pl.BlockSpec((1, D), lambda i: (0, 0)),
            pl.BlockSpec((None, 1, D), bat),
            pl.BlockSpec((None, 1, D), bat),
        ] + _premix_weight_specs(D),
        out_specs=[pl.BlockSpec((tm, w), tok) for w in _PREMIX_WIDTHS],
        out_shape=[jax.ShapeDtypeStruct((T, w), dt)
                   for w, dt in zip(_PREMIX_WIDTHS, _PREMIX_DTYPES)],
        compiler_params=_params(("arbitrary",), VMEM_LIMIT),
        name="premix",
    )(x, gain, sc, sh, w_in_p, wa2_p, b_alpha)


def _premix_combine(dest, x1, g2, ys, gain, sc, sh, w_in_p, wa2_p, b_alpha, seq):
    nt, _, tm = dest.shape
    T, D = x1.shape
    per_b = seq // tm
    tok = lambda i: (i, 0)
    bat = lambda i: (i // per_b, 0, 0)
    smem = pltpu.SMEM
    widths = _PREMIX_WIDTHS + (D,)
    dtypes = _PREMIX_DTYPES + (F32,)
    return pl.pallas_call(
        functools.partial(_premix_combine_kernel, nt=nt),
        grid=(nt,),
        in_specs=[
            pl.BlockSpec((None, 1, tm), lambda i: (jnp.minimum(i + 1, nt - 1), 0, 0),
                         memory_space=smem),
            pl.BlockSpec((None, 1, tm), lambda i: (0, 0, 0), memory_space=smem),
            pl.BlockSpec((tm, D), tok),
            pl.BlockSpec((None, 1, D), bat),
            pl.BlockSpec(memory_space=pl.ANY),
            pl.BlockSpec((1, D), lambda i: (0, 0)),
            pl.BlockSpec((None, 1, D), bat),
            pl.BlockSpec((None, 1, D), bat),
        ] + _premix_weight_specs(D),
        out_specs=[pl.BlockSpec((tm, w), tok) for w in widths],
        out_shape=[jax.ShapeDtypeStruct((T, w), dt) for w, dt in zip(widths, dtypes)],
        scratch_shapes=[pltpu.VMEM((2, tm, D), F32), pltpu.SemaphoreType.DMA((2,))],
        compiler_params=_params(("arbitrary",), VMEM_LIMIT),
        name="premix_combine",
    )(dest, dest, x1, g2, ys, gain, sc, sh, w_in_p, wa2_p, b_alpha)


def _gla_kernel(q_ref, k_ref, v_ref, la_ref, rs_ref, gn_ref, o_ref, s_ref, *, nchunk):
    C = GLA_CHUNK
    H = GLA_HEADS
    HK = H * GLA_DK

    @pl.when(pl.program_id(1) == 0)
    def _():
        s_ref[...] = jnp.zeros_like(s_ref)

    tri = jnp.where(_iota((C, C), 1) <= _iota((C, C), 0), 1.0, 0.0).astype(BF16)
    causal = (_iota((C, HK), 1) & (C - 1)) <= _iota((C, HK), 0)
    lane_head = _iota((1, HK), 1) >> 6
    ones = jnp.ones((C, LANES), BF16)
    zero_blk = jnp.zeros((C, GLA_DV), BF16)
    gain = gn_ref[...]
    tn = (((0,), (0,)), ((), ()))
    nt = (((1,), (1,)), ((), ()))

    def block_diag(blocks):
        rows = [jnp.concatenate([zero_blk] * h + [blk] + [zero_blk] * (H - 1 - h), axis=1)
                for h, blk in enumerate(blocks)]
        return jnp.concatenate(rows, axis=0)

    lhs, v_bds, kvs, decays = [], [], [], []
    for c in range(nchunk):
        rows = slice(c * C, (c + 1) * C)
        la = la_ref[rows, :]
        la_hi = la.astype(BF16)
        la_lo = (la - la_hi.astype(F32)).astype(BF16)
        b = _dot(tri, la_hi) + _dot(tri, la_lo)
        d_col = (lax.dot_general(la_hi, ones, tn, preferred_element_type=F32)
                 + lax.dot_general(la_lo, ones, tn, preferred_element_type=F32))
        decays.append(jnp.exp(d_col))
        q = q_ref[rows, :].astype(F32)
        k = k_ref[rows, :].astype(F32)
        v = v_ref[rows, :]
        q_dec = (q * jnp.exp(b)).astype(BF16)
        k_inv = k * jnp.exp(-b)
        k_end = (k_inv * jnp.exp(b[C - 1:C, :])).astype(BF16)
        k_inv = k_inv.astype(BF16)
        k_bd = jnp.concatenate(
            [jnp.where(lane_head == h, k_inv, jnp.zeros_like(k_inv)) for h in range(H)], axis=0)
        att = lax.dot_general(q_dec, k_bd, nt, preferred_element_type=F32)
        att = jnp.where(causal, att, 0.0).astype(BF16)
        lhs.append(jnp.concatenate([att, q_dec], axis=1))
        v_bds.append(block_diag([v[:, h * GLA_DV:(h + 1) * GLA_DV] for h in range(H)]))
        kvs.append([lax.dot_general(k_end[:, p * LANES:(p + 1) * LANES],
                                    v[:, 2 * p * GLA_DV:(2 * p + 2) * GLA_DV], tn,
                                    preferred_element_type=F32) for p in range(H // 2)])

    state = [s_ref[h] for h in range(H)]
    states = []
    for c in range(nchunk):
        states.append(state)
        state = [decays[c][h * GLA_DK:(h + 1) * GLA_DK, :] * state[h]
                 + kvs[c][h // 2][(h % 2) * GLA_DK:(h % 2 + 1) * GLA_DK,
                                  (h % 2) * GLA_DV:(h % 2 + 1) * GLA_DV]
                 for h in range(H)]
    for h in range(H):
        s_ref[h] = state[h]

    for c in range(nchunk):
        rows = slice(c * C, (c + 1) * C)
        rhs = jnp.concatenate(
            [v_bds[c], block_diag([s.astype(BF16) for s in states[c]])], axis=0)
        o = _dot(lhs[c], rhs)
        outs = []
        for h in range(H):
            oh = o[:, h * GLA_DV:(h + 1) * GLA_DV]
            ms = jnp.mean(oh * oh, axis=-1, keepdims=True)
            outs.append(oh * lax.rsqrt(ms + EPS) * gain[:, h * GLA_DV:(h + 1) * GLA_DV])
        y = jnp.concatenate(outs, axis=1) * rs_ref[rows, :].astype(F32)
        o_ref[rows, :] = y.astype(BF16)


def _gla(gq, gk, gv, la, rs, gn, batch, seq, ts):
    T = gq.shape[0]
    per_b = seq // ts
    tok = lambda b, i: (b * per_b + i, 0)
    HK = GLA_HEADS * GLA_DK
    HV = GLA_HEADS * GLA_DV
    return pl.pallas_call(
        functools.partial(_gla_kernel, nchunk=ts // GLA_CHUNK),
        grid=(batch, per_b),
        in_specs=[
            pl.BlockSpec((ts, HK), tok),
            pl.BlockSpec((ts, HK), tok),
            pl.BlockSpec((ts, HV), tok),
            pl.BlockSpec((ts, HK), tok),
            pl.BlockSpec((ts, HV), tok),
            pl.BlockSpec((1, HV), lambda b, i: (0, 0)),
        ],
        out_specs=pl.BlockSpec((ts, HV), tok),
        out_shape=jax.ShapeDtypeStruct((T, HV), BF16),
        scratch_shapes=[pltpu.VMEM((GLA_HEADS, GLA_DK, GLA_DV), F32)],
        compiler_params=_params(("arbitrary", "arbitrary")),
        name="gla",
    )(gq, gk, gv, la, rs, gn)


def _swa_kernel(q_ref, kc_ref, kp_ref, vc_ref, vp_ref, cc_ref, sc_ref, cp_ref, sp_ref,
                qg_ref, kg_ref, sink_ref, o_ref, *, tq):
    W = WINDOW
    nw = tq // W
    G = SWA_HEADS // SWA_KV_HEADS
    step = pl.program_id(1)

    lane = _iota((1, LANES), 1)
    lo_half = lane < SWA_HD
    first = (lane & (SWA_HD - 1)) < (ROPE_DIMS // 2)
    seg = ((_iota((LANES, LANES), 0) >> 6) == (_iota((LANES, LANES), 1) >> 6))
    seg_mean = jnp.where(seg, 1.0 / SWA_HD, 0.0).astype(BF16)

    def norm_rope(x, cos, sg, gain):
        ms = _dot((x * x).astype(BF16), seg_mean)
        y = x * lax.rsqrt(ms + EPS) * gain
        partner = jnp.where(first, pltpu.roll(y, LANES - ROPE_DIMS // 2, 1),
                            pltpu.roll(y, ROPE_DIMS // 2, 1))
        return y * cos + partner * sg

    cos_c = cc_ref[...]
    sg_c = sc_ref[...]
    cos_k = jnp.concatenate([cp_ref[...], cos_c], axis=0)
    sg_k = jnp.concatenate([sp_ref[...], sg_c], axis=0)
    kb = jnp.concatenate([kp_ref[...], kc_ref[...]], axis=0).astype(F32)
    kb = norm_rope(kb, cos_k, sg_k, kg_ref[...])
    kb_r = pltpu.roll(kb, SWA_HD, 1)
    vb = jnp.concatenate([vp_ref[...], vc_ref[...]], axis=0).astype(F32)
    vb_r = pltpu.roll(vb, SWA_HD, 1)
    k_dup = (jnp.where(lo_half, kb, kb_r).astype(BF16), jnp.where(lo_half, kb_r, kb).astype(BF16))
    v_dup = (jnp.where(lo_half, vb, vb_r).astype(BF16), jnp.where(lo_half, vb_r, vb).astype(BF16))

    q_cols = []
    for c in range(SWA_HEADS // 2):
        qc = q_ref[:, c * LANES:(c + 1) * LANES].astype(F32)
        q_cols.append(norm_rope(qc, cos_c, sg_c, qg_ref[...]) * (SWA_HD ** -0.5))

    cur_side = _iota((W, W), 0) <= _iota((W, W), 1)
    cur_side4 = jnp.concatenate([cur_side] * G, axis=1)
    sinks = sink_ref[...]
    nt = (((1,), (1,)), ((), ()))
    tn = (((0,), (0,)), ((), ()))

    for w in range(nw):
        for kvh in range(SWA_KV_HEADS):
            parts = []
            for cc in range(G // 2):
                qw = q_cols[kvh * (G // 2) + cc][w * W:(w + 1) * W, :]
                parts.append(jnp.where(lo_half, qw, 0.0))
                parts.append(jnp.where(lo_half, 0.0, qw))
            q_stack = jnp.concatenate(parts, axis=0).astype(BF16)
            k_prev = k_dup[kvh][w * W:(w + 1) * W, :]
            k_cur = k_dup[kvh][(w + 1) * W:(w + 2) * W, :]
            v_prev = v_dup[kvh][w * W:(w + 1) * W, :]
            v_cur = v_dup[kvh][(w + 1) * W:(w + 2) * W, :]
            s_cur = lax.dot_general(k_cur, q_stack, nt, preferred_element_type=F32)
            s_prev = lax.dot_general(k_prev, q_stack, nt, preferred_element_type=F32)
            if w == 0:
                s_prev = jnp.where(step > 0, s_prev, -1e30)
            s = jnp.where(cur_side4, s_cur, s_prev)
            sink = jnp.concatenate(
                [sinks[kvh * G + g:kvh * G + g + 1, :] for g in range(G)], axis=1)
            m = jnp.maximum(jnp.max(s, axis=0, keepdims=True), sink)
            p = jnp.exp(s - m)
            den = jnp.sum(p, axis=0, keepdims=True) + jnp.exp(sink - m)
            p = (p * (1.0 / den)).astype(BF16)
            zero = jnp.zeros_like(p)
            o = (lax.dot_general(jnp.where(cur_side4, p, zero), v_cur, tn,
                                 preferred_element_type=F32)
                 + lax.dot_general(jnp.where(cur_side4, zero, p), v_prev, tn,
                                   preferred_element_type=F32))
            og = [o[g * W:(g + 1) * W, :] for g in range(G)]
            for cc in range(G // 2):
                col = jnp.where(lo_half, og[2 * cc], og[2 * cc + 1])
                c0 = (kvh * (G // 2) + cc) * LANES
                o_ref[w * W:(w + 1) * W, c0:c0 + LANES] = col.astype(BF16)


def _swa(sq, skv, cos_t, sg_t, qg, kg, sinks_b, batch, seq, tq):
    T = sq.shape[0]
    per_b = seq // tq
    r = tq // WINDOW
    cur = lambda b, i: (b * per_b + i, 0)
    prev = lambda b, i: (jnp.maximum((b * per_b + i) * r - 1, 0), 0)
    cur_v = lambda b, i: (b * per_b + i, 1)
    prev_v = lambda b, i: (jnp.maximum((b * per_b + i) * r - 1, 0), 1)
    const = lambda b, i: (0, 0)
    QW = SWA_HEADS * SWA_HD
    return pl.pallas_call(
        functools.partial(_swa_kernel, tq=tq),
        grid=(batch, per_b),
        in_specs=[
            pl.BlockSpec((tq, QW), cur),
            pl.BlockSpec((tq, LANES), cur),
            pl.BlockSpec((WINDOW, LANES), prev),
            pl.BlockSpec((tq, LANES), cur_v),
            pl.BlockSpec((WINDOW, LANES), prev_v),
            pl.BlockSpec((tq, LANES), cur),
            pl.BlockSpec((tq, LANES), cur),
            pl.BlockSpec((WINDOW, LANES), prev),
            pl.BlockSpec((WINDOW, LANES), prev),
            pl.BlockSpec((1, LANES), const),
            pl.BlockSpec((1, LANES), const),
            pl.BlockSpec((SWA_HEADS, LANES), const),
        ],
        out_specs=pl.BlockSpec((tq, QW), cur),
        out_shape=jax.ShapeDtypeStruct((T, QW), BF16),
        compiler_params=_params(("arbitrary", "arbitrary")),
        name="swa",
    )(sq, skv, skv, skv, skv, cos_t, sg_t, cos_t, sg_t, qg, kg, sinks_b)


def _second_largest(a, b, c, d):
    hi1, lo1 = jnp.maximum(a, b), jnp.minimum(a, b)
    hi2, lo2 = jnp.maximum(c, d), jnp.minimum(c, d)
    return jnp.maximum(hi1, hi2), jnp.maximum(jnp.minimum(hi1, hi2), jnp.maximum(lo1, lo2))


def _argmax4(vals):
    best, idx = vals[0], jnp.zeros(vals[0].shape, I32)
    for k in range(1, 4):
        upd = vals[k] > best
        idx = jnp.where(upd, k, idx)
        best = jnp.where(upd, vals[k], best)
    return idx, best


def _pick4(idx, vals):
    return jnp.where(idx == 0, vals[0],
                     jnp.where(idx == 1, vals[1], jnp.where(idx == 2, vals[2], vals[3])))


def _postmix_kernel(yg_ref, ys_ref, sa_ref, sb_ref, x_ref, g1_ref, wug_ref, wus_ref, wo_ref,
                    gain_ref, sc_ref, sh_ref, wr_ref, br_ref,
                    x1_ref, h2e_ref, cls_ref, rank_ref, cnt_ref, carry_ref, *, tm):
    @pl.when(pl.program_id(0) == 0)
    def _():
        carry_ref[...] = jnp.zeros_like(carry_ref)

    D = x_ref.shape[1]
    sub = tm // POSTMIX_SUBTILES
    upper = jnp.where(_iota((sub, sub), 0) <= _iota((sub, sub), 1), 1.0, 0.0).astype(BF16)
    carry = carry_ref[...]
    scale = gain_ref[...] * (1.0 + sc_ref[...])
    for s in range(POSTMIX_SUBTILES):
        rows = slice(s * sub, (s + 1) * sub)
        u = _dot(yg_ref[rows, :], wug_ref[...]).astype(BF16)
        v = _dot(ys_ref[rows, :], wus_ref[...]).astype(BF16)
        merged = sa_ref[rows, :] * u + sb_ref[rows, :] * v
        x1 = x_ref[rows, :] + g1_ref[...] * _dot(merged, wo_ref[...])
        x1_ref[rows, :] = x1
        ms = jnp.mean(x1 * x1, axis=-1, keepdims=True)
        h2 = x1 * lax.rsqrt(ms + EPS) * scale + sh_ref[...]
        h2e_ref[rows, 0:D] = h2

        logits = _dot(h2.astype(BF16), wr_ref[...])
        aff = jnp.transpose(jax.nn.sigmoid(logits))[0:N_EXPERTS, :]
        sel = aff + br_ref[...]
        aff_r = [aff[e:e + 1, :] for e in range(N_EXPERTS)]
        sel_r = [sel[e:e + 1, :] for e in range(N_EXPERTS)]

        scores = []
        for g in range(N_GROUPS):
            m1, m2 = _second_largest(*sel_r[4 * g:4 * g + 4])
            scores.append(m1 + m2)
        grp, _ = _argmax4(scores)
        sel_g = [_pick4(grp, [sel_r[4 * g + k] for g in range(N_GROUPS)]) for k in range(4)]
        aff_g = [_pick4(grp, [aff_r[4 * g + k] for g in range(N_GROUPS)]) for k in range(4)]
        l1, _ = _argmax4(sel_g)
        masked = [jnp.where(l1 == k, -jnp.inf, sel_g[k]) for k in range(4)]
        l2, _ = _argmax4(masked)
        a1 = _pick4(l1, aff_g)
        a2 = _pick4(l2, aff_g)
        den = a1 + a2
        w1 = a1 / den
        w2 = a2 / den
        lo_e = jnp.minimum(l1, l2)
        hi_e = jnp.maximum(l1, l2)
        pair = jnp.where(lo_e == 0, hi_e - 1, jnp.where(lo_e == 1, hi_e + 1, 5))
        cls = grp * len(PAIRS) + pair
        w_lo = jnp.where(l1 < l2, w1, w2)
        w_hi = jnp.where(l1 < l2, w2, w1)

        onehot = _iota((CLASS_ROWS, sub), 0) == cls
        oh = jnp.where(onehot, 1.0, 0.0).astype(BF16)
        incl = _dot(oh, upper)
        total = _dot(oh, jnp.ones((sub, LANES), BF16))
        base = jnp.concatenate([carry] * (sub // LANES), axis=1)
        rank = jnp.sum(jnp.where(onehot, base + incl, 0.0), axis=0, keepdims=True) - 1.0
        carry = carry + total

        cls_ref[:, rows] = cls
        rank_ref[:, rows] = rank.astype(I32)
        w_rows = jnp.concatenate([w_lo, w_hi, jnp.zeros((LANES - 2, sub), F32)], axis=0)
        h2e_ref[rows, D:D + WCOLS] = jnp.transpose(w_rows)
    carry_ref[...] = carry
    cnt_ref[...] = carry


def _postmix(yg, ys, sa, sb, x, g1, wug, wus, wo, gain, sc, sh, wr, br, seq, tm):
    T, D = x.shape
    per_b = seq // tm
    nt = T // tm
    tok = lambda i: (i, 0)
    row = lambda i: (0, 0)
    bat = lambda i: (i // per_b, 0, 0)
    return pl.pallas_call(
        functools.partial(_postmix_kernel, tm=tm),
        grid=(nt,),
        in_specs=[
            pl.BlockSpec((tm, yg.shape[1]), tok),
            pl.BlockSpec((tm, ys.shape[1]), tok),
            pl.BlockSpec((tm, D), tok),
            pl.BlockSpec((tm, D), tok),
            pl.BlockSpec((tm, D), tok),
            pl.BlockSpec((None, 1, D), bat),
            pl.BlockSpec(wug.shape, row),
            pl.BlockSpec(wus.shape, row),
            pl.BlockSpec(wo.shape, row),
            pl.BlockSpec((1, D), row),
            pl.BlockSpec((None, 1, D), bat),
            pl.BlockSpec((None, 1, D), bat),
            pl.BlockSpec((D, LANES), row),
            pl.BlockSpec((N_EXPERTS, 1), row),
        ],
        out_specs=[
            pl.BlockSpec((tm, D), tok),
            pl.BlockSpec((tm, D + WCOLS), tok),
            pl.BlockSpec((None, 1, tm), lambda i: (i, 0, 0)),
            pl.BlockSpec((None, 1, tm), lambda i: (i, 0, 0)),
            pl.BlockSpec((CLASS_ROWS, LANES), row),
        ],
        out_shape=[
            jax.ShapeDtypeStruct((T, D), F32),
            jax.ShapeDtypeStruct((T, D + WCOLS), F32),
            jax.ShapeDtypeStruct((nt, 1, tm), I32),
            jax.ShapeDtypeStruct((nt, 1, tm), I32),
            jax.ShapeDtypeStruct((CLASS_ROWS, LANES), F32),
        ],
        scratch_shapes=[pltpu.VMEM((CLASS_ROWS, LANES), F32)],
        compiler_params=_params(("arbitrary",), VMEM_LIMIT),
        name="postmix",
    )(yg, ys, sa, sb, x, g1, wug, wus, wo, gain, sc, sh, wr, br)


def _plan_kernel(start_ref, cls_ref, rank_ref, dest_ref):
    cls = cls_ref[...]
    base = jnp.zeros(cls.shape, I32)
    for c in range(N_CLASSES):
        base = jnp.where(cls == c, start_ref[c], base)
    dest_ref[...] = base + rank_ref[...]


def _plan(start, cls, rank):
    nt, _, tm = cls.shape
    g = min(PLAN_TILES, nt)
    blk = pl.BlockSpec((g, 1, tm), lambda i, s: (i, 0, 0))
    return pl.pallas_call(
        _plan_kernel,
        grid_spec=pltpu.PrefetchScalarGridSpec(
            num_scalar_prefetch=1,
            grid=(nt // g,),
            in_specs=[blk, blk],
            out_specs=blk,
        ),
        out_shape=jax.ShapeDtypeStruct((nt, 1, tm), I32),
        compiler_params=_params(("arbitrary",)),
        name="plan",
    )(start, cls, rank)


ROW_UNROLL = 8


DISPATCH_SLOTS = 3


def _dispatch_kernel(dest_ref, h_ref, xs_in_ref, xs_ref, buf, load_sem, scat_sem, *, tm, nt):
    del xs_in_ref
    i = pl.program_id(0)

    def load(t, slot):
        rows = pl.ds(pl.multiple_of(t * tm, tm), tm)
        return pltpu.make_async_copy(h_ref.at[rows], buf.at[slot], load_sem.at[slot])

    def row_copy(slot, r, d):
        return pltpu.make_async_copy(buf.at[slot, pl.ds(r, 1)], xs_ref.at[pl.ds(d, 1)],
                                     scat_sem.at[slot])

    @pl.when(i == 0)
    def _():
        for t in range(min(DISPATCH_SLOTS - 1, nt)):
            load(t, t).start()

    for slot in range(DISPATCH_SLOTS):
        @pl.when((i < nt) & (lax.rem(i, DISPATCH_SLOTS) == slot))
        def _():
            load(i, slot).wait()
            for r in range(tm):
                row_copy(slot, r, dest_ref[0, r]).start(priority=r % 2)

    @pl.when(i > 0)
    def _():
        slot = lax.rem(i + DISPATCH_SLOTS - 1, DISPATCH_SLOTS)

        def drain(j, carry):
            for u in range(ROW_UNROLL):
                row_copy(slot, 0, 0).wait()
            return carry

        lax.fori_loop(0, tm // ROW_UNROLL, drain, 0)

    @pl.when(i + DISPATCH_SLOTS - 1 < nt)
    def _():
        t = i + DISPATCH_SLOTS - 1
        load(t, lax.rem(t, DISPATCH_SLOTS)).start()


def _dispatch(dest, h2e, xs_init):
    nt, _, tm = dest.shape
    width = h2e.shape[1]
    return pl.pallas_call(
        functools.partial(_dispatch_kernel, tm=tm, nt=nt),
        grid=(nt + 1,),
        in_specs=[
            pl.BlockSpec((None, 1, tm), lambda i: (jnp.minimum(i, nt - 1), 0, 0),
                         memory_space=pltpu.SMEM),
            pl.BlockSpec(memory_space=pl.ANY),
            pl.BlockSpec(memory_space=pl.ANY),
        ],
        out_specs=pl.BlockSpec(memory_space=pl.ANY),
        out_shape=jax.ShapeDtypeStruct(xs_init.shape, F32),
        scratch_shapes=[pltpu.VMEM((DISPATCH_SLOTS, tm, width), F32),
                        pltpu.SemaphoreType.DMA((DISPATCH_SLOTS,)),
                        pltpu.SemaphoreType.DMA((DISPATCH_SLOTS,))],
        input_output_aliases={2: 0},
        compiler_params=_params(("arbitrary",)),
        name="dispatch",
    )(dest, h2e, xs_init)


def _zero_kernel(last_ref, o_ref):
    del last_ref
    o_ref[...] = jnp.zeros_like(o_ref)


def _zero_class_tails(last, ns, width):
    return pl.pallas_call(
        _zero_kernel,
        grid_spec=pltpu.PrefetchScalarGridSpec(
            num_scalar_prefetch=1,
            grid=(N_CLASSES,),
            in_specs=[],
            out_specs=pl.BlockSpec((MOE_ROWS, width), lambda c, last: (last[c], 0)),
        ),
        out_shape=jax.ShapeDtypeStruct((ns, width), F32),
        compiler_params=_params(("arbitrary",)),
        name="zero_class_tails",
    )(last)


def _moe_kernel(elo_ref, ehi_ref, nact_ref, xs_ref, wg0, wu0, wd0, wg1, wu1, wd1, o_ref):
    j = pl.program_id(0)
    D = o_ref.shape[1]

    @pl.when(j < nact_ref[0])
    def _():
        x = xs_ref[:, 0:D].astype(BF16)
        w_lo = xs_ref[:, D:D + 1]
        w_hi = xs_ref[:, D + 1:D + 2]

        def ffn(wg, wu, wd):
            g = _dot(x, wg[...])
            a = (g * jax.nn.sigmoid(g)) * _dot(x, wu[...])
            return _dot(a.astype(BF16), wd[...])

        o_ref[...] = w_lo * ffn(wg0, wu0, wd0) + w_hi * ffn(wg1, wu1, wd1)

    @pl.when(j >= nact_ref[0])
    def _():
        o_ref[...] = jnp.zeros_like(o_ref)


def _moe(elo, ehi, nact, xs, w_gate, w_up, w_down, layer):
    ns, width = xs.shape
    D = width - WCOLS
    nb = ns // MOE_ROWS
    lo = lambda j, a, b, n: (layer, a[j], 0, 0)
    hi = lambda j, a, b, n: (layer, b[j], 0, 0)
    gu = (None, None, D, D_FF)
    dn = (None, None, D_FF, D)
    return pl.pallas_call(
        _moe_kernel,
        grid_spec=pltpu.PrefetchScalarGridSpec(
            num_scalar_prefetch=3,
            grid=(nb,),
            in_specs=[
                pl.BlockSpec((MOE_ROWS, width), lambda j, a, b, n: (jnp.minimum(j, n[0] - 1), 0)),
                pl.BlockSpec(gu, lo), pl.BlockSpec(gu, lo), pl.BlockSpec(dn, lo),
                pl.BlockSpec(gu, hi), pl.BlockSpec(gu, hi), pl.BlockSpec(dn, hi),
            ],
            out_specs=pl.BlockSpec((MOE_ROWS, D), lambda j, a, b, n: (j, 0)),
        ),
        out_shape=jax.ShapeDtypeStruct((ns, D), F32),
        compiler_params=_params(("arbitrary",), VMEM_LIMIT),
        name="moe",
    )(elo, ehi, nact, xs, w_gate, w_up, w_down, w_gate, w_up, w_down)


def _combine_kernel(dest_ref, x1_ref, g2_ref, ys_ref, o_ref, buf, sem, *, tm, nt):
    i = pl.program_id(0)

    def row_copy(slot, r, d):
        return pltpu.make_async_copy(ys_ref.at[pl.ds(d, 1)], buf.at[slot, pl.ds(r, 1)],
                                     sem.at[slot])

    for slot in range(2):
        @pl.when((i < nt) & (lax.rem(i, 2) == slot))
        def _():
            for r in range(tm):
                row_copy(slot, r, dest_ref[0, r]).start(priority=r % 2)

    @pl.when(i > 0)
    def _():
        slot = lax.rem(i + 1, 2)

        def drain(j, carry):
            for u in range(ROW_UNROLL):
                row_copy(slot, 0, 0).wait()
            return carry

        lax.fori_loop(0, tm // ROW_UNROLL, drain, 0)
        o_ref[...] = x1_ref[...] + g2_ref[...] * buf[slot]


def _combine(dest, x1, g2, ys, seq):
    nt, _, tm = dest.shape
    T, D = x1.shape
    per_b = seq // tm
    done = lambda i: jnp.maximum(i - 1, 0)
    return pl.pallas_call(
        functools.partial(_combine_kernel, tm=tm, nt=nt),
        grid=(nt + 1,),
        in_specs=[
            pl.BlockSpec((None, 1, tm), lambda i: (jnp.minimum(i, nt - 1), 0, 0),
                         memory_space=pltpu.SMEM),
            pl.BlockSpec((tm, D), lambda i: (done(i), 0)),
            pl.BlockSpec((None, 1, D), lambda i: (done(i) // per_b, 0, 0)),
            pl.BlockSpec(memory_space=pl.ANY),
        ],
        out_specs=pl.BlockSpec((tm, D), lambda i: (done(i), 0)),
        out_shape=jax.ShapeDtypeStruct((T, D), F32),
        scratch_shapes=[pltpu.VMEM((2, tm, D), F32), pltpu.SemaphoreType.DMA((2,))],
        compiler_params=_params(("arbitrary",)),
        name="combine",
    )(dest, x1, g2, ys)


def _block_plan(counts, nb):
    cnt = counts[:N_CLASSES, 0].astype(I32)
    nblk = (cnt + MOE_ROWS - 1) // MOE_ROWS
    cum = jnp.cumsum(nblk)
    start = ((cum - nblk) * MOE_ROWS).astype(I32)
    nact = cum[-1:].astype(I32)
    blk = jnp.arange(nb, dtype=I32)
    cls = jnp.sum(jnp.minimum(blk, nact[0] - 1)[:, None] >= cum[None, :], axis=1).astype(I32)
    cls = jnp.minimum(cls, N_CLASSES - 1)
    lo_tab = jnp.array([p[0] for p in PAIRS], I32)
    hi_tab = jnp.array([p[1] for p in PAIRS], I32)
    grp = cls // len(PAIRS)
    elo = grp * EXPERTS_PER_GROUP + lo_tab[cls % len(PAIRS)]
    ehi = grp * EXPERTS_PER_GROUP + hi_tab[cls % len(PAIRS)]
    start = jnp.concatenate([start, jnp.zeros((CLASS_ROWS - N_CLASSES,), I32)])
    last = jnp.maximum(cum - 1, 0).astype(I32)
    return start, elo, ehi, nact, last


def kernel(x, c, positions, w_ada, b_ada, norm_mix, w_in, w_alpha2, b_alpha, gla_norm, q_norm,
           k_norm, sinks, w_up_gla, w_up_swa, w_out, norm_ffn, w_router, b_router, w_gate, w_up,
           w_down):
    B, S, D = x.shape
    L = w_ada.shape[0]
    T = B * S
    tm = min(512, S)

    mod = _modulation(c, w_ada, b_ada).reshape(L, B, 6, 1, D)
    cos_t, sg_t = _rope_tables(positions)

    sizes = (256, 256, 512, 16, 512, 512, 128, 128, 1024, 1024)
    offs = [0]
    for s in sizes:
        offs.append(offs[-1] + s)
    order = (0, 1, 2, 4, 5, 6, 7, 8, 9, 3)
    w_in_p = jnp.concatenate(
        [w_in[:, :, offs[i]:offs[i + 1]] for i in order]
        + [jnp.zeros((L, D, LANES - GLA_RANK), F32)], axis=-1).astype(BF16)
    wa2_p = jnp.concatenate(
        [w_alpha2, jnp.zeros((L, LANES - GLA_RANK, w_alpha2.shape[-1]), F32)], axis=1).astype(BF16)
    wr = jnp.concatenate(
        [w_router.astype(BF16), jnp.zeros((D, LANES - N_EXPERTS), BF16)], axis=1)
    br = b_router.reshape(N_EXPERTS, 1)
    wug = w_up_gla.astype(BF16)
    wus = w_up_swa.astype(BF16)
    wo = w_out.astype(BF16)
    wg = w_gate.astype(BF16)
    wu = w_up.astype(BF16)
    wd = w_down.astype(BF16)

    nb = T // MOE_ROWS + N_CLASSES
    xt = x.reshape(T, D)
    pending = None
    for l in range(L):
        sh1, sc1, g1, sh2, sc2, g2 = [mod[l, :, i] for i in range(6)]
        premix_args = (norm_mix[l].reshape(1, D), sc1, sh1, w_in_p[l], wa2_p[l],
                       b_alpha[l].reshape(1, -1))
        if pending is None:
            gq, gk, gv, la, rs, sq, skv, sa, sb = _premix(xt, *premix_args, S, tm)
        else:
            gq, gk, gv, la, rs, sq, skv, sa, sb, xt = _premix_combine(*pending, *premix_args, S)
        y_gla = _gla(gq, gk, gv, la, rs, gla_norm[l].reshape(1, -1), B, S, min(GLA_TILE, S))
        qg = jnp.tile(q_norm[l], LANES // SWA_HD).reshape(1, LANES)
        kg = jnp.tile(k_norm[l], LANES // SWA_HD).reshape(1, LANES)
        sinks_b = jnp.broadcast_to(sinks[l][:, None], (SWA_HEADS, LANES))
        y_swa = _swa(sq, skv, cos_t, sg_t, qg, kg, sinks_b, B, S, tm)
        x1, h2e, cls, rank, counts = _postmix(
            y_gla, y_swa, sa, sb, xt, g1, wug[l], wus[l], wo[l],
            norm_ffn[l].reshape(1, D), sc2, sh2, wr, br, S, tm)
        start, elo, ehi, nact, last = _block_plan(counts, nb)
        dest = _plan(start, cls, rank)
        xs = _dispatch(dest, h2e, _zero_class_tails(last, nb * MOE_ROWS, D + WCOLS))
        ys = _moe(elo, ehi, nact, xs, wg, wu, wd, l)
        pending = (dest, x1, g2, ys)
    xt = _combine(*pending, S)
    return xt.reshape(B, S, D)
```

```python
import functools

import jax
import jax.numpy as jnp
from jax import lax
from jax.experimental import pallas as pl
from jax.experimental.pallas import tpu as pltpu

F32 = jnp.float32
BF16 = jnp.bfloat16
I32 = jnp.int32
HIGHEST = lax.Precision.HIGHEST

GLA_HEADS = 4
GLA_DK = 64
GLA_DV = 128
GLA_RANK = 16
GLA_TAU = 16.0
GLA_CHUNK = 64
SWA_HEADS = 8
SWA_KV_HEADS = 2
SWA_HD = 64
WINDOW = 128
ROPE_DIMS = SWA_HD // 4
ROPE_THETA = 500000.0
N_EXPERTS = 16
N_GROUPS = 4
EXPERTS_PER_GROUP = 4
D_FF = 512
EPS = 1e-6

LANES = 128
SUBLANES = 8
VMEM_LIMIT = 56 * 1024 * 1024

PAIRS = ((0, 1), (0, 2), (0, 3), (1, 2), (1, 3), (2, 3))
N_CLASSES = N_GROUPS * len(PAIRS)
CLASS_ROWS = 32
MOE_ROWS = 512
POSTMIX_SUBTILES = 2
PREMIX_SUBTILES = 2
GLA_TILE = 1024
PLAN_TILES = 16
WCOLS = LANES


def _params(sem, vmem=None):
    return pltpu.CompilerParams(dimension_semantics=sem, vmem_limit_bytes=vmem)


def _dot(a, b):
    return jnp.dot(a, b, preferred_element_type=F32)


def _iota(shape, axis):
    return lax.broadcasted_iota(I32, shape, axis)


def _mod_kernel(c_ref, w_ref, b_ref, o_ref):
    c = c_ref[...]
    cond = c * jax.nn.sigmoid(c)
    o_ref[...] = jnp.dot(cond, w_ref[...], precision=HIGHEST,
                         preferred_element_type=F32) + b_ref[...]


def _modulation(c, w_ada, b_ada):
    L, D, D6 = w_ada.shape
    B = c.shape[0]
    nj = D6 // D
    return pl.pallas_call(
        _mod_kernel,
        grid=(L, nj),
        in_specs=[
            pl.BlockSpec((B, D), lambda l, j: (0, 0)),
            pl.BlockSpec((None, D, D), lambda l, j: (l, 0, j)),
            pl.BlockSpec((None, 1, D), lambda l, j: (l, 0, j)),
        ],
        out_specs=pl.BlockSpec((None, B, D), lambda l, j: (l, 0, j)),
        out_shape=jax.ShapeDtypeStruct((L, B, D6), F32),
        compiler_params=_params(("arbitrary", "arbitrary")),
        name="modulation",
    )(c, w_ada, b_ada.reshape(L, 1, D6))


def _rope_kernel(pos_ref, inv_ref, c_ref, s_ref):
    ang = pos_ref[...].astype(F32) * inv_ref[...]
    l64 = _iota(ang.shape, 1) & (SWA_HD - 1)
    half = ROPE_DIMS // 2
    cos = jnp.cos(ang)
    sin = jnp.sin(ang)
    c_ref[...] = jnp.where(l64 < ROPE_DIMS, cos, 1.0)
    s_ref[...] = jnp.where(l64 < half, -sin, jnp.where(l64 < ROPE_DIMS, sin, 0.0))


def _rope_tables(positions):
    T = positions.size
    half = ROPE_DIMS // 2
    inv_freq = jnp.power(ROPE_THETA, -jnp.arange(half, dtype=F32) / half)
    lane = jnp.arange(LANES)
    inv_lane = inv_freq[(lane % SWA_HD) % half].reshape(1, LANES)
    tm = min(1024, T)
    return pl.pallas_call(
        _rope_kernel,
        grid=(T // tm,),
        in_specs=[pl.BlockSpec((tm, 1), lambda i: (i, 0)),
                  pl.BlockSpec((1, LANES), lambda i: (0, 0))],
        out_specs=[pl.BlockSpec((tm, LANES), lambda i: (i, 0))] * 2,
        out_shape=[jax.ShapeDtypeStruct((T, LANES), F32)] * 2,
        compiler_params=_params(("arbitrary",)),
        name="rope_tables",
    )(positions.reshape(T, 1), inv_lane)


_GQ, _GK, _GV, _GR = 0, 256, 512, 1024
_SQ, _SK, _SV = 1536, 2048, 2176
_MA, _MB, _GA, _WIN = 2304, 3328, 4352, 4480


def _premix_rows(x, rows, gain_ref, sc_ref, sh_ref, w_ref, wa2_ref, ba_ref,
                 gq_ref, gk_ref, gv_ref, la_ref, gr_ref, sq_ref, skv_ref, ma_ref, mb_ref):
    half = (_MB - _MA) // 2
    ms = jnp.mean(x * x, axis=-1, keepdims=True)
    h = x * lax.rsqrt(ms + EPS) * gain_ref[...]
    h = h * (1.0 + sc_ref[...]) + sh_ref[...]
    hb = h.astype(BF16)

    def sec(lo, hi):
        return _dot(hb, w_ref[:, lo:hi])

    gq_ref[rows, :] = (sec(_GQ, _GK) * (GLA_DK ** -0.5)).astype(BF16)
    gk_ref[rows, :] = sec(_GK, _GV).astype(BF16)
    gv_ref[rows, :] = sec(_GV, _GR).astype(BF16)
    r = sec(_GR, _SQ)
    gr_ref[rows, :] = (r * jax.nn.sigmoid(r)).astype(BF16)
    sq_ref[rows, :] = sec(_SQ, _SK).astype(BF16)
    skv_ref[rows, :] = sec(_SK, _MA).astype(BF16)
    for c in range(2):
        cols = slice(c * half, (c + 1) * half)
        ma_ref[rows, cols] = jax.nn.sigmoid(
            sec(_MA + c * half, _MA + (c + 1) * half)).astype(BF16)
        mb_ref[rows, cols] = jax.nn.sigmoid(
            sec(_MB + c * half, _MB + (c + 1) * half)).astype(BF16)
    a_low = sec(_GA, _WIN).astype(BF16)
    z = _dot(a_low, wa2_ref[...]) + ba_ref[...]
    log_sig = jnp.minimum(z, 0.0) - jnp.log1p(jnp.exp(-jnp.abs(z)))
    la_ref[rows, :] = log_sig * (1.0 / GLA_TAU)


def _premix_kernel(x_ref, *refs):
    sub = x_ref.shape[0] // PREMIX_SUBTILES
    for s in range(PREMIX_SUBTILES):
        rows = slice(s * sub, (s + 1) * sub)
        _premix_rows(x_ref[rows, :], rows, *refs)


def _premix_combine_kernel(dnext_ref, dfirst_ref, x1_ref, g2_ref, ys_ref, *refs, nt):
    refs, x_ref, buf, sem = refs[:-3], refs[-3], refs[-2], refs[-1]
    i = pl.program_id(0)
    tm = x1_ref.shape[0]
    slot = lax.rem(i, 2)

    def row_copy(s_, r, d):
        return pltpu.make_async_copy(ys_ref.at[pl.ds(d, 1)], buf.at[s_, pl.ds(r, 1)],
                                     sem.at[s_])

    def issue(dest_ref, s_):
        for r in range(tm):
            row_copy(s_, r, dest_ref[0, r]).start(priority=r % 2)

    def drain(s_):
        def body(j, carry):
            for u in range(ROW_UNROLL):
                row_copy(s_, 0, 0).wait()
            return carry
        lax.fori_loop(0, tm // ROW_UNROLL, body, 0)

    @pl.when(i == 0)
    def _():
        issue(dfirst_ref, 0)

    drain(slot)
    for s_ in range(2):
        @pl.when(slot != s_)
        def _():
            issue(dnext_ref, s_)
    x_ref[...] = x1_ref[...] + g2_ref[...] * buf[slot]
    sub = tm // PREMIX_SUBTILES
    for s in range(PREMIX_SUBTILES):
        rows = slice(s * sub, (s + 1) * sub)
        _premix_rows(x_ref[rows, :], rows, *refs)

    @pl.when(i == nt - 1)
    def _():
        drain(1 - slot)


_PREMIX_WIDTHS = (256, 256, 512, 256, 512, 512, 256, 1024, 1024)
_PREMIX_DTYPES = (BF16, BF16, BF16, F32, BF16, BF16, BF16, BF16, BF16)


def _premix_weight_specs(D):
    row = lambda i: (0, 0)
    return [pl.BlockSpec((D, _WIN), row), pl.BlockSpec((LANES, 256), row),
            pl.BlockSpec((1, 256), row)]


def _premix(x, gain, sc, sh, w_in_p, wa2_p, b_alpha, seq, tm):
    T, D = x.shape
    per_b = seq // tm
    tok = lambda i: (i, 0)
    bat = lambda i: (i // per_b, 0, 0)
    return pl.pallas_call(
        _premix_kernel,
        grid=(T // tm,),
        in_specs=[
            pl.BlockSpec((tm, D), tok),
            pl.BlockSpec((1, D), lambda i: (0, 0)),
            pl.BlockSpec((None, 1, D), bat),
            pl.BlockSpec((None, 1, D), bat),
        ] + _premix_weight_specs(D),
        out_specs=[pl.BlockSpec((tm, w), tok) for w in _PREMIX_WIDTHS],
        out_shape=[jax.ShapeDtypeStruct((T, w), dt)
                   for w, dt in zip(_PREMIX_WIDTHS, _PREMIX_DTYPES)],
        compiler_params=_params(("arbitrary",), VMEM_LIMIT),
        name="premix",
    )(x, gain, sc, sh, w_in_p, wa2_p, b_alpha)


def _premix_combine(dest, x1, g2, ys, gain, sc, sh, w_in_p, wa2_p, b_alpha, seq):
    nt, _, tm = dest.shape
    T, D = x1.shape
    per_b = seq // tm
    tok = lambda i: (i, 0)
    bat = lambda i: (i // per_b, 0, 0)
    smem = pltpu.SMEM
    widths = _PREMIX_WIDTHS + (D,)
    dtypes = _PREMIX_DTYPES + (F32,)
    return pl.pallas_call(
        functools.partial(_premix_combine_kernel, nt=nt),
        grid=(nt,),
        in_specs=[
            pl.BlockSpec((None, 1, tm), lambda i: (jnp.minimum(i + 1, nt - 1), 0, 0),
                         memory_space=smem),
            pl.BlockSpec((None, 1, tm), lambda i: (0, 0, 0), memory_space=smem),
            pl.BlockSpec((tm, D), tok),
            pl.BlockSpec((None, 1, D), bat),
            pl.BlockSpec(memory_space=pl.ANY),
            pl.BlockSpec((1, D), lambda i: (0, 0)),
            pl.BlockSpec((None, 1, D), bat),
            pl.BlockSpec((None, 1, D), bat),
        ] + _premix_weight_specs(D),
        out_specs=[pl.BlockSpec((tm, w), tok) for w in widths],
        out_shape=[jax.ShapeDtypeStruct((T, w), dt) for w, dt in zip(widths, dtypes)],
        scratch_shapes=[pltpu.VMEM((2, tm, D), F32), pltpu.SemaphoreType.DMA((2,))],
        compiler_params=_params(("arbitrary",), VMEM_LIMIT),
        name="premix_combine",
    )(dest, dest, x1, g2, ys, gain, sc, sh, w_in_p, wa2_p, b_alpha)


def _gla_body(q_ref, k_ref, v_ref, la_ref, rs_ref, gn_ref, o_ref, s_ref, nchunk):
    C = GLA_CHUNK
    H = GLA_HEADS
    HK = H * GLA_DK

    @pl.when(pl.program_id(1) == 0)
    def _():
        s_ref[...] = jnp.zeros_like(s_ref)

    tri = jnp.where(_iota((C, C), 1) <= _iota((C, C), 0), 1.0, 0.0).astype(BF16)
    causal = (_iota((C, HK), 1) & (C - 1)) <= _iota((C, HK), 0)
    lane_head = _iota((1, HK), 1) >> 6
    ones = jnp.ones((C, LANES), BF16)
    zero_blk = jnp.zeros((C, GLA_DV), BF16)
    gain = gn_ref[...]
    tn = (((0,), (0,)), ((), ()))
    nt = (((1,), (1,)), ((), ()))

    def block_diag(blocks):
        rows = [jnp.concatenate([zero_blk] * h + [blk] + [zero_blk] * (H - 1 - h), axis=1)
                for h, blk in enumerate(blocks)]
        return jnp.concatenate(rows, axis=0)

    lhs, v_bds, kvs, decays = [], [], [], []
    for c in range(nchunk):
        rows = slice(c * C, (c + 1) * C)
        la = la_ref[rows, :]
        la_hi = la.astype(BF16)
        la_lo = (la - la_hi.astype(F32)).astype(BF16)
        b = _dot(tri, la_hi) + _dot(tri, la_lo)
        d_col = (lax.dot_general(la_hi, ones, tn, preferred_element_type=F32)
                 + lax.dot_general(la_lo, ones, tn, preferred_element_type=F32))
        decays.append(jnp.exp(d_col))
        q = q_ref[rows, :].astype(F32)
        k = k_ref[rows, :].astype(F32)
        v = v_ref[rows, :]
        q_dec = (q * jnp.exp(b)).astype(BF16)
        k_inv = k * jnp.exp(-b)
        k_end = (k_inv * jnp.exp(b[C - 1:C, :])).astype(BF16)
        k_inv = k_inv.astype(BF16)
        k_bd = jnp.concatenate(
            [jnp.where(lane_head == h, k_inv, jnp.zeros_like(k_inv)) for h in range(H)], axis=0)
        att = lax.dot_general(q_dec, k_bd, nt, preferred_element_type=F32)
        att = jnp.where(causal, att, 0.0).astype(BF16)
        lhs.append(jnp.concatenate([att, q_dec], axis=1))
        v_bds.append(block_diag([v[:, h * GLA_DV:(h + 1) * GLA_DV] for h in range(H)]))
        kvs.append([lax.dot_general(k_end[:, p * LANES:(p + 1) * LANES],
                                    v[:, 2 * p * GLA_DV:(2 * p + 2) * GLA_DV], tn,
                                    preferred_element_type=F32) for p in range(H // 2)])
        yield

    state = [s_ref[h] for h in range(H)]
    states = []
    for c in range(nchunk):
        states.append(state)
        state = [decays[c][h * GLA_DK:(h + 1) * GLA_DK, :] * state[h]
                 + kvs[c][h // 2][(h % 2) * GLA_DK:(h % 2 + 1) * GLA_DK,
                                  (h % 2) * GLA_DV:(h % 2 + 1) * GLA_DV]
                 for h in range(H)]
    for h in range(H):
        s_ref[h] = state[h]
    yield

    for c in range(nchunk):
        rows = slice(c * C, (c + 1) * C)
        rhs = jnp.concatenate(
            [v_bds[c], block_diag([s.astype(BF16) for s in states[c]])], axis=0)
        o = _dot(lhs[c], rhs)
        outs = []
        for h in range(H):
            oh = o[:, h * GLA_DV:(h + 1) * GLA_DV]
            ms = jnp.mean(oh * oh, axis=-1, keepdims=True)
            outs.append(oh * lax.rsqrt(ms + EPS) * gain[:, h * GLA_DV:(h + 1) * GLA_DV])
        y = jnp.concatenate(outs, axis=1) * rs_ref[rows, :].astype(F32)
        o_ref[rows, :] = y.astype(BF16)
        yield


def _swa_body(q_ref, kc_ref, kp_ref, vc_ref, vp_ref, cc_ref, sc_ref, cp_ref, sp_ref,
              qg_ref, kg_ref, sink_ref, o_ref, tq):
    W = WINDOW
    nw = tq // W
    G = SWA_HEADS // SWA_KV_HEADS
    step = pl.program_id(1)

    lane = _iota((1, LANES), 1)
    lo_half = lane < SWA_HD
    first = (lane & (SWA_HD - 1)) < (ROPE_DIMS // 2)
    seg = ((_iota((LANES, LANES), 0) >> 6) == (_iota((LANES, LANES), 1) >> 6))
    seg_mean = jnp.where(seg, 1.0 / SWA_HD, 0.0).astype(BF16)

    def norm_rope(x, cos, sg, gain):
        ms = _dot((x * x).astype(BF16), seg_mean)
        y = x * lax.rsqrt(ms + EPS) * gain
        partner = jnp.where(first, pltpu.roll(y, LANES - ROPE_DIMS // 2, 1),
                            pltpu.roll(y, ROPE_DIMS // 2, 1))
        return y * cos + partner * sg

    cos_c = cc_ref[...]
    sg_c = sc_ref[...]
    cos_k = jnp.concatenate([cp_ref[...], cos_c], axis=0)
    sg_k = jnp.concatenate([sp_ref[...], sg_c], axis=0)
    kb = jnp.concatenate([kp_ref[...], kc_ref[...]], axis=0).astype(F32)
    kb = norm_rope(kb, cos_k, sg_k, kg_ref[...])
    kb_r = pltpu.roll(kb, SWA_HD, 1)
    vb = jnp.concatenate([vp_ref[...], vc_ref[...]], axis=0).astype(F32)
    vb_r = pltpu.roll(vb, SWA_HD, 1)
    k_dup = (jnp.where(lo_half, kb, kb_r).astype(BF16), jnp.where(lo_half, kb_r, kb).astype(BF16))
    v_dup = (jnp.where(lo_half, vb, vb_r).astype(BF16), jnp.where(lo_half, vb_r, vb).astype(BF16))

    q_cols = []
    for c in range(SWA_HEADS // 2):
        qc = q_ref[:, c * LANES:(c + 1) * LANES].astype(F32)
        q_cols.append(norm_rope(qc, cos_c, sg_c, qg_ref[...]) * (SWA_HD ** -0.5))
    yield

    cur_side = _iota((W, W), 0) <= _iota((W, W), 1)
    cur_side4 = jnp.concatenate([cur_side] * G, axis=1)
    sinks = sink_ref[...]
    nt = (((1,), (1,)), ((), ()))
    tn = (((0,), (0,)), ((), ()))

    for w in range(nw):
        for kvh in range(SWA_KV_HEADS):
            parts = []
            for cc in range(G // 2):
                qw = q_cols[kvh * (G // 2) + cc][w * W:(w + 1) * W, :]
                parts.append(jnp.where(lo_half, qw, 0.0))
                parts.append(jnp.where(lo_half, 0.0, qw))
            q_stack = jnp.concatenate(parts, axis=0).astype(BF16)
            k_prev = k_dup[kvh][w * W:(w + 1) * W, :]
            k_cur = k_dup[kvh][(w + 1) * W:(w + 2) * W, :]
            v_prev = v_dup[kvh][w * W:(w + 1) * W, :]
            v_cur = v_dup[kvh][(w + 1) * W:(w + 2) * W, :]
            s_cur = lax.dot_general(k_cur, q_stack, nt, preferred_element_type=F32)
            s_prev = lax.dot_general(k_prev, q_stack, nt, preferred_element_type=F32)
            if w == 0:
                s_prev = jnp.where(step > 0, s_prev, -1e30)
            s = jnp.where(cur_side4, s_cur, s_prev)
            sink = jnp.concatenate(
                [sinks[kvh * G + g:kvh * G + g + 1, :] for g in range(G)], axis=1)
            m = jnp.maximum(jnp.max(s, axis=0, keepdims=True), sink)
            p = jnp.exp(s - m)
            den = jnp.sum(p, axis=0, keepdims=True) + jnp.exp(sink - m)
            p = (p * (1.0 / den)).astype(BF16)
            zero = jnp.zeros_like(p)
            o = (lax.dot_general(jnp.where(cur_side4, p, zero), v_cur, tn,
                                 preferred_element_type=F32)
                 + lax.dot_general(jnp.where(cur_side4, zero, p), v_prev, tn,
                                   preferred_element_type=F32))
            og = [o[g * W:(g + 1) * W, :] for g in range(G)]
            for cc in range(G // 2):
                col = jnp.where(lo_half, og[2 * cc], og[2 * cc + 1])
                c0 = (kvh * (G // 2) + cc) * LANES
                o_ref[w * W:(w + 1) * W, c0:c0 + LANES] = col.astype(BF16)
            yield


def _run_body(body, *refs, **static):
    for _ in body(*refs, **static):
        pass


def _gla(gq, gk, gv, la, rs, gn, batch, seq, ts):
    T = gq.shape[0]
    per_b = seq // ts
    tok = lambda b, i: (b * per_b + i, 0)
    HK = GLA_HEADS * GLA_DK
    HV = GLA_HEADS * GLA_DV
    return pl.pallas_call(
        functools.partial(_run_body, _gla_body, nchunk=ts // GLA_CHUNK),
        grid=(batch, per_b),
        in_specs=[
            pl.BlockSpec((ts, HK), tok),
            pl.BlockSpec((ts, HK), tok),
            pl.BlockSpec((ts, HV), tok),
            pl.BlockSpec((ts, HK), tok),
            pl.BlockSpec((ts, HV), tok),
            pl.BlockSpec((1, HV), lambda b, i: (0, 0)),
        ],
        out_specs=pl.BlockSpec((ts, HV), tok),
        out_shape=jax.ShapeDtypeStruct((T, HV), BF16),
        scratch_shapes=[pltpu.VMEM((GLA_HEADS, GLA_DK, GLA_DV), F32)],
        compiler_params=_params(("arbitrary", "arbitrary")),
        name="gla",
    )(gq, gk, gv, la, rs, gn)


def _swa(sq, skv, cos_t, sg_t, qg, kg, sinks_b, batch, seq, tq):
    T = sq.shape[0]
    per_b = seq // tq
    r = tq // WINDOW
    cur = lambda b, i: (b * per_b + i, 0)
    prev = lambda b, i: (jnp.maximum((b * per_b + i) * r - 1, 0), 0)
    cur_v = lambda b, i: (b * per_b + i, 1)
    prev_v = lambda b, i: (jnp.maximum((b * per_b + i) * r - 1, 0), 1)
    const = lambda b, i: (0, 0)
    QW = SWA_HEADS * SWA_HD
    return pl.pallas_call(
        functools.partial(_run_body, _swa_body, tq=tq),
        grid=(batch, per_b),
        in_specs=[
            pl.BlockSpec((tq, QW), cur),
            pl.BlockSpec((tq, LANES), cur),
            pl.BlockSpec((WINDOW, LANES), prev),
            pl.BlockSpec((tq, LANES), cur_v),
            pl.BlockSpec((WINDOW, LANES), prev_v),
            pl.BlockSpec((tq, LANES), cur),
            pl.BlockSpec((tq, LANES), cur),
            pl.BlockSpec((WINDOW, LANES), prev),
            pl.BlockSpec((WINDOW, LANES), prev),
            pl.BlockSpec((1, LANES), const),
            pl.BlockSpec((1, LANES), const),
            pl.BlockSpec((SWA_HEADS, LANES), const),
        ],
        out_specs=pl.BlockSpec((tq, QW), cur),
        out_shape=jax.ShapeDtypeStruct((T, QW), BF16),
        compiler_params=_params(("arbitrary", "arbitrary")),
        name="swa",
    )(sq, skv, skv, skv, skv, cos_t, sg_t, cos_t, sg_t, qg, kg, sinks_b)


def _second_largest(a, b, c, d):
    hi1, lo1 = jnp.maximum(a, b), jnp.minimum(a, b)
    hi2, lo2 = jnp.maximum(c, d), jnp.minimum(c, d)
    return jnp.maximum(hi1, hi2), jnp.maximum(jnp.minimum(hi1, hi2), jnp.maximum(lo1, lo2))


def _argmax4(vals):
    best, idx = vals[0], jnp.zeros(vals[0].shape, I32)
    for k in range(1, 4):
        upd = vals[k] > best
        idx = jnp.where(upd, k, idx)
        best = jnp.where(upd, vals[k], best)
    return idx, best


def _pick4(idx, vals):
    return jnp.where(idx == 0, vals[0],
                     jnp.where(idx == 1, vals[1], jnp.where(idx == 2, vals[2], vals[3])))


def _postmix_kernel(yg_ref, ys_ref, sa_ref, sb_ref, x_ref, g1_ref, wug_ref, wus_ref, wo_ref,
                    gain_ref, sc_ref, sh_ref, wr_ref, br_ref,
                    x1_ref, h2e_ref, cls_ref, rank_ref, cnt_ref, carry_ref, *, tm):
    @pl.when(pl.program_id(0) == 0)
    def _():
        carry_ref[...] = jnp.zeros_like(carry_ref)

    D = x_ref.shape[1]
    sub = tm // POSTMIX_SUBTILES
    upper = jnp.where(_iota((sub, sub), 0) <= _iota((sub, sub), 1), 1.0, 0.0).astype(BF16)
    carry = carry_ref[...]
    scale = gain_ref[...] * (1.0 + sc_ref[...])
    for s in range(POSTMIX_SUBTILES):
        rows = slice(s * sub, (s + 1) * sub)
        u = _dot(yg_ref[rows, :], wug_ref[...]).astype(BF16)
        v = _dot(ys_ref[rows, :], wus_ref[...]).astype(BF16)
        merged = sa_ref[rows, :] * u + sb_ref[rows, :] * v
        x1 = x_ref[rows, :] + g1_ref[...] * _dot(merged, wo_ref[...])
        x1_ref[rows, :] = x1
        ms = jnp.mean(x1 * x1, axis=-1, keepdims=True)
        h2 = x1 * lax.rsqrt(ms + EPS) * scale + sh_ref[...]
        h2e_ref[rows, 0:D] = h2

        logits = _dot(h2.astype(BF16), wr_ref[...])
        aff = jnp.transpose(jax.nn.sigmoid(logits))[0:N_EXPERTS, :]
        sel = aff + br_ref[...]
        aff_r = [aff[e:e + 1, :] for e in range(N_EXPERTS)]
        sel_r = [sel[e:e + 1, :] for e in range(N_EXPERTS)]

        scores = []
        for g in range(N_GROUPS):
            m1, m2 = _second_largest(*sel_r[4 * g:4 * g + 4])
            scores.append(m1 + m2)
        grp, _ = _argmax4(scores)
        sel_g = [_pick4(grp, [sel_r[4 * g + k] for g in range(N_GROUPS)]) for k in range(4)]
        aff_g = [_pick4(grp, [aff_r[4 * g + k] for g in range(N_GROUPS)]) for k in range(4)]
        l1, _ = _argmax4(sel_g)
        masked = [jnp.where(l1 == k, -jnp.inf, sel_g[k]) for k in range(4)]
        l2, _ = _argmax4(masked)
        a1 = _pick4(l1, aff_g)
        a2 = _pick4(l2, aff_g)
        den = a1 + a2
        w1 = a1 / den
        w2 = a2 / den
        lo_e = jnp.minimum(l1, l2)
        hi_e = jnp.maximum(l1, l2)
        pair = jnp.where(lo_e == 0, hi_e - 1, jnp.where(lo_e == 1, hi_e + 1, 5))
        cls = grp * len(PAIRS) + pair
        w_lo = jnp.where(l1 < l2, w1, w2)
        w_hi = jnp.where(l1 < l2, w2, w1)

        onehot = _iota((CLASS_ROWS, sub), 0) == cls
        oh = jnp.where(onehot, 1.0, 0.0).astype(BF16)
        incl = _dot(oh, upper)
        total = _dot(oh, jnp.ones((sub, LANES), BF16))
        base = jnp.concatenate([carry] * (sub // LANES), axis=1)
        rank = jnp.sum(jnp.where(onehot, base + incl, 0.0), axis=0, keepdims=True) - 1.0
        carry = carry + total

        cls_ref[:, rows] = cls
        rank_ref[:, rows] = rank.astype(I32)
        w_rows = jnp.concatenate([w_lo, w_hi, jnp.zeros((LANES - 2, sub), F32)], axis=0)
        h2e_ref[rows, D:D + WCOLS] = jnp.transpose(w_rows)
    carry_ref[...] = carry
    cnt_ref[...] = carry


def _postmix(yg, ys, sa, sb, x, g1, wug, wus, wo, gain, sc, sh, wr, br, seq, tm):
    T, D = x.shape
    per_b = seq // tm
    nt = T // tm
    tok = lambda i: (i, 0)
    row = lambda i: (0, 0)
    bat = lambda i: (i // per_b, 0, 0)
    return pl.pallas_call(
        functools.partial(_postmix_kernel, tm=tm),
        grid=(nt,),
        in_specs=[
            pl.BlockSpec((tm, yg.shape[1]), tok),
            pl.BlockSpec((tm, ys.shape[1]), tok),
            pl.BlockSpec((tm, D), tok),
            pl.BlockSpec((tm, D), tok),
            pl.BlockSpec((tm, D), tok),
            pl.BlockSpec((None, 1, D), bat),
            pl.BlockSpec(wug.shape, row),
            pl.BlockSpec(wus.shape, row),
            pl.BlockSpec(wo.shape, row),
            pl.BlockSpec((1, D), row),
            pl.BlockSpec((None, 1, D), bat),
            pl.BlockSpec((None, 1, D), bat),
            pl.BlockSpec((D, LANES), row),
            pl.BlockSpec((N_EXPERTS, 1), row),
        ],
        out_specs=[
            pl.BlockSpec((tm, D), tok),
            pl.BlockSpec((tm, D + WCOLS), tok),
            pl.BlockSpec((None, 1, tm), lambda i: (i, 0, 0)),
            pl.BlockSpec((None, 1, tm), lambda i: (i, 0, 0)),
            pl.BlockSpec((CLASS_ROWS, LANES), row),
        ],
        out_shape=[
            jax.ShapeDtypeStruct((T, D), F32),
            jax.ShapeDtypeStruct((T, D + WCOLS), F32),
            jax.ShapeDtypeStruct((nt, 1, tm), I32),
            jax.ShapeDtypeStruct((nt, 1, tm), I32),
            jax.ShapeDtypeStruct((CLASS_ROWS, LANES), F32),
        ],
        scratch_shapes=[pltpu.VMEM((CLASS_ROWS, LANES), F32)],
        compiler_params=_params(("arbitrary",), VMEM_LIMIT),
        name="postmix",
    )(yg, ys, sa, sb, x, g1, wug, wus, wo, gain, sc, sh, wr, br)


def _plan_kernel(start_ref, cls_ref, rank_ref, dest_ref):
    cls = cls_ref[...]
    base = jnp.zeros(cls.shape, I32)
    for c in range(N_CLASSES):
        base = jnp.where(cls == c, start_ref[c], base)
    dest_ref[...] = base + rank_ref[...]


def _plan(start, cls, rank):
    nt, _, tm = cls.shape
    g = min(PLAN_TILES, nt)
    blk = pl.BlockSpec((g, 1, tm), lambda i, s: (i, 0, 0))
    return pl.pallas_call(
        _plan_kernel,
        grid_spec=pltpu.PrefetchScalarGridSpec(
            num_scalar_prefetch=1,
            grid=(nt // g,),
            in_specs=[blk, blk],
            out_specs=blk,
        ),
        out_shape=jax.ShapeDtypeStruct((nt, 1, tm), I32),
        compiler_params=_params(("arbitrary",)),
        name="plan",
    )(start, cls, rank)


ROW_UNROLL = 8


DISPATCH_SLOTS = 3


def _dispatch_kernel(dest_ref, h_ref, xs_in_ref, xs_ref, buf, load_sem, scat_sem, *, tm, nt):
    del xs_in_ref
    i = pl.program_id(0)

    def load(t, slot):
        rows = pl.ds(pl.multiple_of(t * tm, tm), tm)
        return pltpu.make_async_copy(h_ref.at[rows], buf.at[slot], load_sem.at[slot])

    def row_copy(slot, r, d):
        return pltpu.make_async_copy(buf.at[slot, pl.ds(r, 1)], xs_ref.at[pl.ds(d, 1)],
                                     scat_sem.at[slot])

    @pl.when(i == 0)
    def _():
        for t in range(min(DISPATCH_SLOTS - 1, nt)):
            load(t, t).start()

    for slot in range(DISPATCH_SLOTS):
        @pl.when((i < nt) & (lax.rem(i, DISPATCH_SLOTS) == slot))
        def _():
            load(i, slot).wait()
            for r in range(tm):
                row_copy(slot, r, dest_ref[0, r]).start(priority=r % 2)

    @pl.when(i > 0)
    def _():
        slot = lax.rem(i + DISPATCH_SLOTS - 1, DISPATCH_SLOTS)

        def drain(j, carry):
            for u in range(ROW_UNROLL):
                row_copy(slot, 0, 0).wait()
            return carry

        lax.fori_loop(0, tm // ROW_UNROLL, drain, 0)

    @pl.when(i + DISPATCH_SLOTS - 1 < nt)
    def _():
        t = i + DISPATCH_SLOTS - 1
        load(t, lax.rem(t, DISPATCH_SLOTS)).start()


def _dispatch(dest, h2e, xs_init):
    nt, _, tm = dest.shape
    width = h2e.shape[1]
    return pl.pallas_call(
        functools.partial(_dispatch_kernel, tm=tm, nt=nt),
        grid=(nt + 1,),
        in_specs=[
            pl.BlockSpec((None, 1, tm), lambda i: (jnp.minimum(i, nt - 1), 0, 0),
                         memory_space=pltpu.SMEM),
            pl.BlockSpec(memory_space=pl.ANY),
            pl.BlockSpec(memory_space=pl.ANY),
        ],
        out_specs=pl.BlockSpec(memory_space=pl.ANY),
        out_shape=jax.ShapeDtypeStruct(xs_init.shape, F32),
        scratch_shapes=[pltpu.VMEM((DISPATCH_SLOTS, tm, width), F32),
                        pltpu.SemaphoreType.DMA((DISPATCH_SLOTS,)),
                        pltpu.SemaphoreType.DMA((DISPATCH_SLOTS,))],
        input_output_aliases={2: 0},
        compiler_params=_params(("arbitrary",)),
        name="dispatch",
    )(dest, h2e, xs_init)


def _moe_kernel(elo_ref, ehi_ref, nact_ref, xs_ref, wg0, wu0, wd0, wg1, wu1, wd1, o_ref):
    j = pl.program_id(0)
    D = o_ref.shape[1]

    @pl.when(j < nact_ref[0])
    def _():
        x = xs_ref[:, 0:D].astype(BF16)
        w_lo = xs_ref[:, D:D + 1]
        w_hi = xs_ref[:, D + 1:D + 2]

        def ffn(wg, wu, wd):
            g = _dot(x, wg[...])
            a = (g * jax.nn.sigmoid(g)) * _dot(x, wu[...])
            return _dot(a.astype(BF16), wd[...])

        o_ref[...] = w_lo * ffn(wg0, wu0, wd0) + w_hi * ffn(wg1, wu1, wd1)

    @pl.when(j >= nact_ref[0])
    def _():
        o_ref[...] = jnp.zeros_like(o_ref)


def _moe(elo, ehi, nact, xs, w_gate, w_up, w_down, layer):
    ns, width = xs.shape
    D = width - WCOLS
    nb = ns // MOE_ROWS
    lo = lambda j, a, b, n: (layer, a[j], 0, 0)
    hi = lambda j, a, b, n: (layer, b[j], 0, 0)
    gu = (None, None, D, D_FF)
    dn = (None, None, D_FF, D)
    return pl.pallas_call(
        _moe_kernel,
        grid_spec=pltpu.PrefetchScalarGridSpec(
            num_scalar_prefetch=3,
            grid=(nb,),
            in_specs=[
                pl.BlockSpec((MOE_ROWS, width), lambda j, a, b, n: (jnp.minimum(j, n[0] - 1), 0)),
                pl.BlockSpec(gu, lo), pl.BlockSpec(gu, lo), pl.BlockSpec(dn, lo),
                pl.BlockSpec(gu, hi), pl.BlockSpec(gu, hi), pl.BlockSpec(dn, hi),
            ],
            out_specs=pl.BlockSpec((MOE_ROWS, D), lambda j, a, b, n: (j, 0)),
        ),
        out_shape=jax.ShapeDtypeStruct((ns, D), F32),
        compiler_params=_params(("arbitrary",), VMEM_LIMIT),
        name="moe",
    )(elo, ehi, nact, xs, w_gate, w_up, w_down, w_gate, w_up, w_down)


def _combine_kernel(dest_ref, x1_ref, g2_ref, ys_ref, o_ref, buf, sem, *, tm, nt):
    i = pl.program_id(0)

    def row_copy(slot, r, d):
        return pltpu.make_async_copy(ys_ref.at[pl.ds(d, 1)], buf.at[slot, pl.ds(r, 1)],
                                     sem.at[slot])

    for slot in range(2):
        @pl.when((i < nt) & (lax.rem(i, 2) == slot))
        def _():
            for r in range(tm):
                row_copy(slot, r, dest_ref[0, r]).start(priority=r % 2)

    @pl.when(i > 0)
    def _():
        slot = lax.rem(i + 1, 2)

        def drain(j, carry):
            for u in range(ROW_UNROLL):
                row_copy(slot, 0, 0).wait()
            return carry

        lax.fori_loop(0, tm // ROW_UNROLL, drain, 0)
        o_ref[...] = x1_ref[...] + g2_ref[...] * buf[slot]


def _combine(dest, x1, g2, ys, seq):
    nt, _, tm = dest.shape
    T, D = x1.shape
    per_b = seq // tm
    done = lambda i: jnp.maximum(i - 1, 0)
    return pl.pallas_call(
        functools.partial(_combine_kernel, tm=tm, nt=nt),
        grid=(nt + 1,),
        in_specs=[
            pl.BlockSpec((None, 1, tm), lambda i: (jnp.minimum(i, nt - 1), 0, 0),
                         memory_space=pltpu.SMEM),
            pl.BlockSpec((tm, D), lambda i: (done(i), 0)),
            pl.BlockSpec((None, 1, D), lambda i: (done(i) // per_b, 0, 0)),
            pl.BlockSpec(memory_space=pl.ANY),
        ],
        out_specs=pl.BlockSpec((tm, D), lambda i: (done(i), 0)),
        out_shape=jax.ShapeDtypeStruct((T, D), F32),
        scratch_shapes=[pltpu.VMEM((2, tm, D), F32), pltpu.SemaphoreType.DMA((2,))],
        compiler_params=_params(("arbitrary",)),
        name="combine",
    )(dest, x1, g2, ys)


def _block_plan(counts, nb):
    cnt = counts[:N_CLASSES, 0].astype(I32)
    nblk = (cnt + MOE_ROWS - 1) // MOE_ROWS
    cum = jnp.cumsum(nblk)
    start = ((cum - nblk) * MOE_ROWS).astype(I32)
    nact = cum[-1:].astype(I32)
    blk = jnp.arange(nb, dtype=I32)
    cls = jnp.sum(jnp.minimum(blk, nact[0] - 1)[:, None] >= cum[None, :], axis=1).astype(I32)
    cls = jnp.minimum(cls, N_CLASSES - 1)
    lo_tab = jnp.array([p[0] for p in PAIRS], I32)
    hi_tab = jnp.array([p[1] for p in PAIRS], I32)
    grp = cls // len(PAIRS)
    elo = grp * EXPERTS_PER_GROUP + lo_tab[cls % len(PAIRS)]
    ehi = grp * EXPERTS_PER_GROUP + hi_tab[cls % len(PAIRS)]
    start = jnp.concatenate([start, jnp.zeros((CLASS_ROWS - N_CLASSES,), I32)])
    return start, elo, ehi, nact


def kernel(x, c, positions, w_ada, b_ada, norm_mix, w_in, w_alpha2, b_alpha, gla_norm, q_norm,
           k_norm, sinks, w_up_gla, w_up_swa, w_out, norm_ffn, w_router, b_router, w_gate, w_up,
           w_down):
    B, S, D = x.shape
    L = w_ada.shape[0]
    T = B * S
    tm = min(512, S)

    mod = _modulation(c, w_ada, b_ada).reshape(L, B, 6, 1, D)
    cos_t, sg_t = _rope_tables(positions)

    sizes = (256, 256, 512, 16, 512, 512, 128, 128, 1024, 1024)
    offs = [0]
    for s in sizes:
        offs.append(offs[-1] + s)
    order = (0, 1, 2, 4, 5, 6, 7, 8, 9, 3)
    w_in_p = jnp.concatenate(
        [w_in[:, :, offs[i]:offs[i + 1]] for i in order]
        + [jnp.zeros((L, D, LANES - GLA_RANK), F32)], axis=-1).astype(BF16)
    wa2_p = jnp.concatenate(
        [w_alpha2, jnp.zeros((L, LANES - GLA_RANK, w_alpha2.shape[-1]), F32)], axis=1).astype(BF16)
    wr = jnp.concatenate(
        [w_router.astype(BF16), jnp.zeros((D, LANES - N_EXPERTS), BF16)], axis=1)
    br = b_router.reshape(N_EXPERTS, 1)
    wug = w_up_gla.astype(BF16)
    wus = w_up_swa.astype(BF16)
    wo = w_out.astype(BF16)
    wg = w_gate.astype(BF16)
    wu = w_up.astype(BF16)
    wd = w_down.astype(BF16)

    nb = T // MOE_ROWS + N_CLASSES
    xs = jnp.zeros((nb * MOE_ROWS, D + WCOLS), F32)
    xt = x.reshape(T, D)
    pending = None
    for l in range(L):
        sh1, sc1, g1, sh2, sc2, g2 = [mod[l, :, i] for i in range(6)]
        premix_args = (norm_mix[l].reshape(1, D), sc1, sh1, w_in_p[l], wa2_p[l],
                       b_alpha[l].reshape(1, -1))
        if pending is None:
            gq, gk, gv, la, rs, sq, skv, sa, sb = _premix(xt, *premix_args, S, tm)
        else:
            gq, gk, gv, la, rs, sq, skv, sa, sb, xt = _premix_combine(*pending, *premix_args, S)
        qg = jnp.tile(q_norm[l], LANES // SWA_HD).reshape(1, LANES)
        kg = jnp.tile(k_norm[l], LANES // SWA_HD).reshape(1, LANES)
        sinks_b = jnp.broadcast_to(sinks[l][:, None], (SWA_HEADS, LANES))
        y_gla = _gla(gq, gk, gv, la, rs, gla_norm[l].reshape(1, -1), B, S, min(GLA_TILE, S))
        y_swa = _swa(sq, skv, cos_t, sg_t, qg, kg, sinks_b, B, S, tm)
        x1, h2e, cls, rank, counts = _postmix(
            y_gla, y_swa, sa, sb, xt, g1, wug[l], wus[l], wo[l],
            norm_ffn[l].reshape(1, D), sc2, sh2, wr, br, S, tm)
        start, elo, ehi, nact = _block_plan(counts, nb)
        dest = _plan(start, cls, rank)
        xs = _dispatch(dest, h2e, xs)
        ys = _moe(elo, ehi, nact, xs, wg, wu, wd, l)
        pending = (dest, x1, g2, ys)
    xt = _combine(*pending, S)
    return xt.reshape(B, S, D)
```

```python
import functools

import jax
import jax.numpy as jnp
from jax import lax
from jax.experimental import pallas as pl
from jax.experimental.pallas import tpu as pltpu

F32 = jnp.float32
BF16 = jnp.bfloat16
I32 = jnp.int32
HIGHEST = lax.Precision.HIGHEST

GLA_HEADS = 4
GLA_DK = 64
GLA_DV = 128
GLA_RANK = 16
GLA_TAU = 16.0
GLA_CHUNK = 64
SWA_HEADS = 8
SWA_KV_HEADS = 2
SWA_HD = 64
WINDOW = 128
ROPE_DIMS = SWA_HD // 4
ROPE_THETA = 500000.0
N_EXPERTS = 16
N_GROUPS = 4
EXPERTS_PER_GROUP = 4
D_FF = 512
EPS = 1e-6

LANES = 128
SUBLANES = 8
VMEM_LIMIT = 56 * 1024 * 1024

PAIRS = ((0, 1), (0, 2), (0, 3), (1, 2), (1, 3), (2, 3))
N_CLASSES = N_GROUPS * len(PAIRS)
CLASS_ROWS = 32
MOE_ROWS = 512
POSTMIX_SUBTILES = 2
PREMIX_SUBTILES = 2
GLA_TILE = 1024
GLA_WAVE = 4
PLAN_TILES = 16
WCOLS = LANES


def _params(sem, vmem=None):
    return pltpu.CompilerParams(dimension_semantics=sem, vmem_limit_bytes=vmem)


def _dot(a, b):
    return jnp.dot(a, b, preferred_element_type=F32)


def _iota(shape, axis):
    return lax.broadcasted_iota(I32, shape, axis)


def _mod_kernel(c_ref, w_ref, b_ref, o_ref):
    c = c_ref[...]
    cond = c * jax.nn.sigmoid(c)
    o_ref[...] = jnp.dot(cond, w_ref[...], precision=HIGHEST,
                         preferred_element_type=F32) + b_ref[...]


def _modulation(c, w_ada, b_ada):
    L, D, D6 = w_ada.shape
    B = c.shape[0]
    nj = D6 // D
    return pl.pallas_call(
        _mod_kernel,
        grid=(L, nj),
        in_specs=[
            pl.BlockSpec((B, D), lambda l, j: (0, 0)),
            pl.BlockSpec((None, D, D), lambda l, j: (l, 0, j)),
            pl.BlockSpec((None, 1, D), lambda l, j: (l, 0, j)),
        ],
        out_specs=pl.BlockSpec((None, B, D), lambda l, j: (l, 0, j)),
        out_shape=jax.ShapeDtypeStruct((L, B, D6), F32),
        compiler_params=_params(("arbitrary", "arbitrary")),
        name="modulation",
    )(c, w_ada, b_ada.reshape(L, 1, D6))


def _rope_kernel(pos_ref, inv_ref, c_ref, s_ref):
    ang = pos_ref[...].astype(F32) * inv_ref[...]
    l64 = _iota(ang.shape, 1) & (SWA_HD - 1)
    half = ROPE_DIMS // 2
    cos = jnp.cos(ang)
    sin = jnp.sin(ang)
    c_ref[...] = jnp.where(l64 < ROPE_DIMS, cos, 1.0)
    s_ref[...] = jnp.where(l64 < half, -sin, jnp.where(l64 < ROPE_DIMS, sin, 0.0))


def _rope_tables(positions):
    T = positions.size
    half = ROPE_DIMS // 2
    inv_freq = jnp.power(ROPE_THETA, -jnp.arange(half, dtype=F32) / half)
    lane = jnp.arange(LANES)
    inv_lane = inv_freq[(lane % SWA_HD) % half].reshape(1, LANES)
    tm = min(1024, T)
    return pl.pallas_call(
        _rope_kernel,
        grid=(T // tm,),
        in_specs=[pl.BlockSpec((tm, 1), lambda i: (i, 0)),
                  pl.BlockSpec((1, LANES), lambda i: (0, 0))],
        out_specs=[pl.BlockSpec((tm, LANES), lambda i: (i, 0))] * 2,
        out_shape=[jax.ShapeDtypeStruct((T, LANES), F32)] * 2,
        compiler_params=_params(("arbitrary",)),
        name="rope_tables",
    )(positions.reshape(T, 1), inv_lane)


_GQ, _GK, _GV, _GR = 0, 256, 512, 1024
_SQ, _SK, _SV = 1536, 2048, 2176
_MA, _MB, _GA, _WIN = 2304, 3328, 4352, 4480


def _premix_groups(x_ref, gain_ref, sc_ref, sh_ref, w_ref, wa2_ref, ba_ref,
                   gq_ref, gk_ref, gv_ref, la_ref, gr_ref, sq_ref, skv_ref, ma_ref, mb_ref):
    sub = x_ref.shape[0] // PREMIX_SUBTILES
    groups = [slice(s * sub, (s + 1) * sub) for s in range(PREMIX_SUBTILES)]
    half = (_MB - _MA) // 2
    hbs = []
    for r in groups:
        x = x_ref[r, :]
        ms = jnp.mean(x * x, axis=-1, keepdims=True)
        h = x * lax.rsqrt(ms + EPS) * gain_ref[...]
        h = h * (1.0 + sc_ref[...]) + sh_ref[...]
        hbs.append(h.astype(BF16))

    def sec(lo, hi):
        return zip(groups, [_dot(hb, w_ref[:, lo:hi]) for hb in hbs])

    for r, z in sec(_GQ, _GK):
        gq_ref[r, :] = (z * (GLA_DK ** -0.5)).astype(BF16)
    for r, z in sec(_GK, _GV):
        gk_ref[r, :] = z.astype(BF16)
    for r, z in sec(_GV, _GR):
        gv_ref[r, :] = z.astype(BF16)
    for r, z in sec(_GR, _SQ):
        gr_ref[r, :] = (z * jax.nn.sigmoid(z)).astype(BF16)
    for r, z in sec(_SQ, _SK):
        sq_ref[r, :] = z.astype(BF16)
    for r, z in sec(_SK, _MA):
        skv_ref[r, :] = z.astype(BF16)
    for c in range(2):
        cols = slice(c * half, (c + 1) * half)
        for r, z in sec(_MA + c * half, _MA + (c + 1) * half):
            ma_ref[r, cols] = jax.nn.sigmoid(z).astype(BF16)
        for r, z in sec(_MB + c * half, _MB + (c + 1) * half):
            mb_ref[r, cols] = jax.nn.sigmoid(z).astype(BF16)
    for r, a_low in sec(_GA, _WIN):
        z = _dot(a_low.astype(BF16), wa2_ref[...]) + ba_ref[...]
        log_sig = jnp.minimum(z, 0.0) - jnp.log1p(jnp.exp(-jnp.abs(z)))
        la_ref[r, :] = log_sig * (1.0 / GLA_TAU)


def _premix_kernel(x_ref, *refs):
    _premix_groups(x_ref, *refs)


def _premix_combine_kernel(dnext_ref, dfirst_ref, x1_ref, g2_ref, ys_ref, *refs, nt):
    refs, x_ref, buf, sem = refs[:-3], refs[-3], refs[-2], refs[-1]
    i = pl.program_id(0)
    tm = x1_ref.shape[0]
    slot = lax.rem(i, 2)

    def row_copy(s_, r, d):
        return pltpu.make_async_copy(ys_ref.at[pl.ds(d, 1)], buf.at[s_, pl.ds(r, 1)],
                                     sem.at[s_])

    def issue(dest_ref, s_):
        for r in range(tm):
            row_copy(s_, r, dest_ref[0, r]).start(priority=r % 2)

    def drain(s_):
        def body(j, carry):
            for u in range(ROW_UNROLL):
                row_copy(s_, 0, 0).wait()
            return carry
        lax.fori_loop(0, tm // ROW_UNROLL, body, 0)

    @pl.when(i == 0)
    def _():
        issue(dfirst_ref, 0)

    drain(slot)
    for s_ in range(2):
        @pl.when(slot != s_)
        def _():
            issue(dnext_ref, s_)
    x_ref[...] = x1_ref[...] + g2_ref[...] * buf[slot]
    _premix_groups(x_ref, *refs)

    @pl.when(i == nt - 1)
    def _():
        drain(1 - slot)


_PREMIX_WIDTHS = (256, 256, 512, 256, 512, 512, 256, 1024, 1024)
_PREMIX_DTYPES = (BF16, BF16, BF16, F32, BF16, BF16, BF16, BF16, BF16)


def _premix_weight_specs(D):
    row = lambda i: (0, 0)
    return [pl.BlockSpec((D, _WIN), row), pl.BlockSpec((LANES, 256), row),
            pl.BlockSpec((1, 256), row)]


def _premix(x, gain, sc, sh, w_in_p, wa2_p, b_alpha, seq, tm):
    T, D = x.shape
    per_b = seq // tm
    tok = lambda i: (i, 0)
    bat = lambda i: (i // per_b, 0, 0)
    return pl.pallas_call(
        _premix_kernel,
        grid=(T // tm,),
        in_specs=[
            pl.BlockSpec((tm, D), tok),
            pl.BlockSpec((1, D), lambda i: (0, 0)),
            pl.BlockSpec((None, 1, D), bat),
            pl.BlockSpec((None, 1, D), bat),
        ] + _premix_weight_specs(D),
        out_specs=[pl.BlockSpec((tm, w), tok) for w in _PREMIX_WIDTHS],
        out_shape=[jax.ShapeDtypeStruct((T, w), dt)
                   for w, dt in zip(_PREMIX_WIDTHS, _PREMIX_DTYPES)],
        compiler_params=_params(("arbitrary",), VMEM_LIMIT),
        name="premix",
    )(x, gain, sc, sh, w_in_p, wa2_p, b_alpha)


def _premix_combine(dest, x1, g2, ys, gain, sc, sh, w_in_p, wa2_p, b_alpha, seq):
    nt, _, tm = dest.shape
    T, D = x1.shape
    per_b = seq // tm
    tok = lambda i: (i, 0)
    bat = lambda i: (i // per_b, 0, 0)
    smem = pltpu.SMEM
    widths = _PREMIX_WIDTHS + (D,)
    dtypes = _PREMIX_DTYPES + (F32,)
    return pl.pallas_call(
        functools.partial(_premix_combine_kernel, nt=nt),
        grid=(nt,),
        in_specs=[
            pl.BlockSpec((None, 1, tm), lambda i: (jnp.minimum(i + 1, nt - 1), 0, 0),
                         memory_space=smem),
            pl.BlockSpec((None, 1, tm), lambda i: (0, 0, 0), memory_space=smem),
            pl.BlockSpec((tm, D), tok),
            pl.BlockSpec((None, 1, D), bat),
            pl.BlockSpec(memory_space=pl.ANY),
            pl.BlockSpec((1, D), lambda i: (0, 0)),
            pl.BlockSpec((None, 1, D), bat),
            pl.BlockSpec((None, 1, D), bat),
        ] + _premix_weight_specs(D),
        out_specs=[pl.BlockSpec((tm, w), tok) for w in widths],
        out_shape=[jax.ShapeDtypeStruct((T, w), dt) for w, dt in zip(widths, dtypes)],
        scratch_shapes=[pltpu.VMEM((2, tm, D), F32), pltpu.SemaphoreType.DMA((2,))],
        compiler_params=_params(("arbitrary",), VMEM_LIMIT),
        name="premix_combine",
    )(dest, dest, x1, g2, ys, gain, sc, sh, w_in_p, wa2_p, b_alpha)


def _gla_body(q_ref, k_ref, v_ref, la_ref, rs_ref, gn_ref, o_ref, s_ref, nchunk):
    C = GLA_CHUNK
    H = GLA_HEADS
    HK = H * GLA_DK

    @pl.when(pl.program_id(1) == 0)
    def _():
        s_ref[...] = jnp.zeros_like(s_ref)

    tri = jnp.where(_iota((C, C), 1) <= _iota((C, C), 0), 1.0, 0.0).astype(BF16)
    causal = (_iota((C, HK), 1) & (C - 1)) <= _iota((C, HK), 0)
    lane_head = _iota((1, HK), 1) >> 6
    ones = jnp.ones((C, LANES), BF16)
    zero_blk = jnp.zeros((C, GLA_DV), BF16)
    gain = gn_ref[...]
    tn = (((0,), (0,)), ((), ()))
    nt = (((1,), (1,)), ((), ()))

    def block_diag(blocks):
        rows = [jnp.concatenate([zero_blk] * h + [blk] + [zero_blk] * (H - 1 - h), axis=1)
                for h, blk in enumerate(blocks)]
        return jnp.concatenate(rows, axis=0)

    lhs, v_bds, kvs, decays = [], [], [], []
    for w0 in range(0, nchunk, GLA_WAVE):
        wave = range(w0, min(w0 + GLA_WAVE, nchunk))
        bs, ds = [], []
        for c in wave:
            la = la_ref[c * C:(c + 1) * C, :]
            la_hi = la.astype(BF16)
            la_lo = (la - la_hi.astype(F32)).astype(BF16)
            bs.append(_dot(tri, la_hi) + _dot(tri, la_lo))
            ds.append(lax.dot_general(la_hi, ones, tn, preferred_element_type=F32)
                      + lax.dot_general(la_lo, ones, tn, preferred_element_type=F32))
        atts, q_decs = [], []
        for c, b, d_col in zip(wave, bs, ds):
            rows = slice(c * C, (c + 1) * C)
            decays.append(jnp.exp(d_col))
            q = q_ref[rows, :].astype(F32)
            k = k_ref[rows, :].astype(F32)
            v = v_ref[rows, :]
            q_dec = (q * jnp.exp(b)).astype(BF16)
            k_inv = k * jnp.exp(-b)
            k_end = (k_inv * jnp.exp(b[C - 1:C, :])).astype(BF16)
            k_inv = k_inv.astype(BF16)
            k_bd = jnp.concatenate(
                [jnp.where(lane_head == h, k_inv, jnp.zeros_like(k_inv)) for h in range(H)],
                axis=0)
            atts.append(lax.dot_general(q_dec, k_bd, nt, preferred_element_type=F32))
            q_decs.append(q_dec)
            v_bds.append(block_diag([v[:, h * GLA_DV:(h + 1) * GLA_DV] for h in range(H)]))
            kvs.append([lax.dot_general(k_end[:, p * LANES:(p + 1) * LANES],
                                        v[:, 2 * p * GLA_DV:(2 * p + 2) * GLA_DV], tn,
                                        preferred_element_type=F32) for p in range(H // 2)])
        for att, q_dec in zip(atts, q_decs):
            att = jnp.where(causal, att, 0.0).astype(BF16)
            lhs.append(jnp.concatenate([att, q_dec], axis=1))
        yield

    state = [s_ref[h] for h in range(H)]
    states = []
    for c in range(nchunk):
        states.append(state)
        state = [decays[c][h * GLA_DK:(h + 1) * GLA_DK, :] * state[h]
                 + kvs[c][h // 2][(h % 2) * GLA_DK:(h % 2 + 1) * GLA_DK,
                                  (h % 2) * GLA_DV:(h % 2 + 1) * GLA_DV]
                 for h in range(H)]
    for h in range(H):
        s_ref[h] = state[h]
    yield

    for w0 in range(0, nchunk, GLA_WAVE):
        wave = range(w0, min(w0 + GLA_WAVE, nchunk))
        os_ = []
        for c in wave:
            rhs = jnp.concatenate(
                [v_bds[c], block_diag([s.astype(BF16) for s in states[c]])], axis=0)
            os_.append(_dot(lhs[c], rhs))
        for c, o in zip(wave, os_):
            rows = slice(c * C, (c + 1) * C)
            outs = []
            for h in range(H):
                oh = o[:, h * GLA_DV:(h + 1) * GLA_DV]
                ms = jnp.mean(oh * oh, axis=-1, keepdims=True)
                outs.append(oh * lax.rsqrt(ms + EPS) * gain[:, h * GLA_DV:(h + 1) * GLA_DV])
            y = jnp.concatenate(outs, axis=1) * rs_ref[rows, :].astype(F32)
            o_ref[rows, :] = y.astype(BF16)
        yield


def _swa_body(q_ref, kc_ref, kp_ref, vc_ref, vp_ref, cc_ref, sc_ref, cp_ref, sp_ref,
              qg_ref, kg_ref, sink_ref, o_ref, tq):
    W = WINDOW
    nw = tq // W
    G = SWA_HEADS // SWA_KV_HEADS
    step = pl.program_id(1)

    lane = _iota((1, LANES), 1)
    lo_half = lane < SWA_HD
    first = (lane & (SWA_HD - 1)) < (ROPE_DIMS // 2)
    seg = ((_iota((LANES, LANES), 0) >> 6) == (_iota((LANES, LANES), 1) >> 6))
    seg_mean = jnp.where(seg, 1.0 / SWA_HD, 0.0).astype(BF16)

    def norm_rope(x, cos, sg, gain):
        ms = _dot((x * x).astype(BF16), seg_mean)
        y = x * lax.rsqrt(ms + EPS) * gain
        partner = jnp.where(first, pltpu.roll(y, LANES - ROPE_DIMS // 2, 1),
                            pltpu.roll(y, ROPE_DIMS // 2, 1))
        return y * cos + partner * sg

    cos_c = cc_ref[...]
    sg_c = sc_ref[...]
    cos_k = jnp.concatenate([cp_ref[...], cos_c], axis=0)
    sg_k = jnp.concatenate([sp_ref[...], sg_c], axis=0)
    kb = jnp.concatenate([kp_ref[...], kc_ref[...]], axis=0).astype(F32)
    kb = norm_rope(kb, cos_k, sg_k, kg_ref[...])
    kb_r = pltpu.roll(kb, SWA_HD, 1)
    vb = jnp.concatenate([vp_ref[...], vc_ref[...]], axis=0).astype(F32)
    vb_r = pltpu.roll(vb, SWA_HD, 1)
    k_dup = (jnp.where(lo_half, kb, kb_r).astype(BF16), jnp.where(lo_half, kb_r, kb).astype(BF16))
    v_dup = (jnp.where(lo_half, vb, vb_r).astype(BF16), jnp.where(lo_half, vb_r, vb).astype(BF16))

    q_cols = []
    for c in range(SWA_HEADS // 2):
        qc = q_ref[:, c * LANES:(c + 1) * LANES].astype(F32)
        q_cols.append(norm_rope(qc, cos_c, sg_c, qg_ref[...]) * (SWA_HD ** -0.5))
    yield

    cur_side = _iota((W, W), 0) <= _iota((W, W), 1)
    cur_side4 = jnp.concatenate([cur_side] * G, axis=1)
    sinks = sink_ref[...]
    nt = (((1,), (1,)), ((), ()))
    tn = (((0,), (0,)), ((), ()))

    for w in range(nw):
        for kvh in range(SWA_KV_HEADS):
            parts = []
            for cc in range(G // 2):
                qw = q_cols[kvh * (G // 2) + cc][w * W:(w + 1) * W, :]
                parts.append(jnp.where(lo_half, qw, 0.0))
                parts.append(jnp.where(lo_half, 0.0, qw))
            q_stack = jnp.concatenate(parts, axis=0).astype(BF16)
            k_prev = k_dup[kvh][w * W:(w + 1) * W, :]
            k_cur = k_dup[kvh][(w + 1) * W:(w + 2) * W, :]
            v_prev = v_dup[kvh][w * W:(w + 1) * W, :]
            v_cur = v_dup[kvh][(w + 1) * W:(w + 2) * W, :]
            s_cur = lax.dot_general(k_cur, q_stack, nt, preferred_element_type=F32)
            s_prev = lax.dot_general(k_prev, q_stack, nt, preferred_element_type=F32)
            if w == 0:
                s_prev = jnp.where(step > 0, s_prev, -1e30)
            s = jnp.where(cur_side4, s_cur, s_prev)
            sink = jnp.concatenate(
                [sinks[kvh * G + g:kvh * G + g + 1, :] for g in range(G)], axis=1)
            m = jnp.maximum(jnp.max(s, axis=0, keepdims=True), sink)
            p = jnp.exp(s - m)
            den = jnp.sum(p, axis=0, keepdims=True) + jnp.exp(sink - m)
            p = (p * (1.0 / den)).astype(BF16)
            zero = jnp.zeros_like(p)
            o = (lax.dot_general(jnp.where(cur_side4, p, zero), v_cur, tn,
                                 preferred_element_type=F32)
                 + lax.dot_general(jnp.where(cur_side4, zero, p), v_prev, tn,
                                   preferred_element_type=F32))
            og = [o[g * W:(g + 1) * W, :] for g in range(G)]
            for cc in range(G // 2):
                col = jnp.where(lo_half, og[2 * cc], og[2 * cc + 1])
                c0 = (kvh * (G // 2) + cc) * LANES
                o_ref[w * W:(w + 1) * W, c0:c0 + LANES] = col.astype(BF16)
            yield


def _run_body(body, *refs, **static):
    for _ in body(*refs, **static):
        pass


def _gla(gq, gk, gv, la, rs, gn, batch, seq, ts):
    T = gq.shape[0]
    per_b = seq // ts
    tok = lambda b, i: (b * per_b + i, 0)
    HK = GLA_HEADS * GLA_DK
    HV = GLA_HEADS * GLA_DV
    return pl.pallas_call(
        functools.partial(_run_body, _gla_body, nchunk=ts // GLA_CHUNK),
        grid=(batch, per_b),
        in_specs=[
            pl.BlockSpec((ts, HK), tok),
            pl.BlockSpec((ts, HK), tok),
            pl.BlockSpec((ts, HV), tok),
            pl.BlockSpec((ts, HK), tok),
            pl.BlockSpec((ts, HV), tok),
            pl.BlockSpec((1, HV), lambda b, i: (0, 0)),
        ],
        out_specs=pl.BlockSpec((ts, HV), tok),
        out_shape=jax.ShapeDtypeStruct((T, HV), BF16),
        scratch_shapes=[pltpu.VMEM((GLA_HEADS, GLA_DK, GLA_DV), F32)],
        compiler_params=_params(("arbitrary", "arbitrary")),
        name="gla",
    )(gq, gk, gv, la, rs, gn)


def _swa(sq, skv, cos_t, sg_t, qg, kg, sinks_b, batch, seq, tq):
    T = sq.shape[0]
    per_b = seq // tq
    r = tq // WINDOW
    cur = lambda b, i: (b * per_b + i, 0)
    prev = lambda b, i: (jnp.maximum((b * per_b + i) * r - 1, 0), 0)
    cur_v = lambda b, i: (b * per_b + i, 1)
    prev_v = lambda b, i: (jnp.maximum((b * per_b + i) * r - 1, 0), 1)
    const = lambda b, i: (0, 0)
    QW = SWA_HEADS * SWA_HD
    return pl.pallas_call(
        functools.partial(_run_body, _swa_body, tq=tq),
        grid=(batch, per_b),
        in_specs=[
            pl.BlockSpec((tq, QW), cur),
            pl.BlockSpec((tq, LANES), cur),
            pl.BlockSpec((WINDOW, LANES), prev),
            pl.BlockSpec((tq, LANES), cur_v),
            pl.BlockSpec((WINDOW, LANES), prev_v),
            pl.BlockSpec((tq, LANES), cur),
            pl.BlockSpec((tq, LANES), cur),
            pl.BlockSpec((WINDOW, LANES), prev),
            pl.BlockSpec((WINDOW, LANES), prev),
            pl.BlockSpec((1, LANES), const),
            pl.BlockSpec((1, LANES), const),
            pl.BlockSpec((SWA_HEADS, LANES), const),
        ],
        out_specs=pl.BlockSpec((tq, QW), cur),
        out_shape=jax.ShapeDtypeStruct((T, QW), BF16),
        compiler_params=_params(("arbitrary", "arbitrary")),
        name="swa",
    )(sq, skv, skv, skv, skv, cos_t, sg_t, cos_t, sg_t, qg, kg, sinks_b)


def _second_largest(a, b, c, d):
    hi1, lo1 = jnp.maximum(a, b), jnp.minimum(a, b)
    hi2, lo2 = jnp.maximum(c, d), jnp.minimum(c, d)
    return jnp.maximum(hi1, hi2), jnp.maximum(jnp.minimum(hi1, hi2), jnp.maximum(lo1, lo2))


def _argmax4(vals):
    best, idx = vals[0], jnp.zeros(vals[0].shape, I32)
    for k in range(1, 4):
        upd = vals[k] > best
        idx = jnp.where(upd, k, idx)
        best = jnp.where(upd, vals[k], best)
    return idx, best


def _pick4(idx, vals):
    return jnp.where(idx == 0, vals[0],
                     jnp.where(idx == 1, vals[1], jnp.where(idx == 2, vals[2], vals[3])))


def _postmix_kernel(yg_ref, ys_ref, sa_ref, sb_ref, x_ref, g1_ref, wug_ref, wus_ref, wo_ref,
                    gain_ref, sc_ref, sh_ref, wr_ref, br_ref,
                    x1_ref, h2e_ref, cls_ref, rank_ref, cnt_ref, carry_ref, *, tm):
    @pl.when(pl.program_id(0) == 0)
    def _():
        carry_ref[...] = jnp.zeros_like(carry_ref)

    D = x_ref.shape[1]
    sub = tm // POSTMIX_SUBTILES
    upper = jnp.where(_iota((sub, sub), 0) <= _iota((sub, sub), 1), 1.0, 0.0).astype(BF16)
    carry = carry_ref[...]
    scale = gain_ref[...] * (1.0 + sc_ref[...])
    groups = [slice(s * sub, (s + 1) * sub) for s in range(POSTMIX_SUBTILES)]
    us = [_dot(yg_ref[r, :], wug_ref[...]).astype(BF16) for r in groups]
    vs = [_dot(ys_ref[r, :], wus_ref[...]).astype(BF16) for r in groups]
    mixed = [_dot(sa_ref[r, :] * u + sb_ref[r, :] * v, wo_ref[...])
             for r, u, v in zip(groups, us, vs)]
    h2s = []
    for r, y in zip(groups, mixed):
        x1 = x_ref[r, :] + g1_ref[...] * y
        x1_ref[r, :] = x1
        ms = jnp.mean(x1 * x1, axis=-1, keepdims=True)
        h2 = x1 * lax.rsqrt(ms + EPS) * scale + sh_ref[...]
        h2e_ref[r, 0:D] = h2
        h2s.append(h2.astype(BF16))
    all_logits = [_dot(h, wr_ref[...]) for h in h2s]
    for rows, logits in zip(groups, all_logits):
        aff = jnp.transpose(jax.nn.sigmoid(logits))[0:N_EXPERTS, :]
        sel = aff + br_ref[...]
        aff_r = [aff[e:e + 1, :] for e in range(N_EXPERTS)]
        sel_r = [sel[e:e + 1, :] for e in range(N_EXPERTS)]

        scores = []
        for g in range(N_GROUPS):
            m1, m2 = _second_largest(*sel_r[4 * g:4 * g + 4])
            scores.append(m1 + m2)
        grp, _ = _argmax4(scores)
        sel_g = [_pick4(grp, [sel_r[4 * g + k] for g in range(N_GROUPS)]) for k in range(4)]
        aff_g = [_pick4(grp, [aff_r[4 * g + k] for g in range(N_GROUPS)]) for k in range(4)]
        l1, _ = _argmax4(sel_g)
        masked = [jnp.where(l1 == k, -jnp.inf, sel_g[k]) for k in range(4)]
        l2, _ = _argmax4(masked)
        a1 = _pick4(l1, aff_g)
        a2 = _pick4(l2, aff_g)
        den = a1 + a2
        w1 = a1 / den
        w2 = a2 / den
        lo_e = jnp.minimum(l1, l2)
        hi_e = jnp.maximum(l1, l2)
        pair = jnp.where(lo_e == 0, hi_e - 1, jnp.where(lo_e == 1, hi_e + 1, 5))
        cls = grp * len(PAIRS) + pair
        w_lo = jnp.where(l1 < l2, w1, w2)
        w_hi = jnp.where(l1 < l2, w2, w1)

        onehot = _iota((CLASS_ROWS, sub), 0) == cls
        oh = jnp.where(onehot, 1.0, 0.0).astype(BF16)
        incl = _dot(oh, upper)
        total = _dot(oh, jnp.ones((sub, LANES), BF16))
        base = jnp.concatenate([carry] * (sub // LANES), axis=1)
        rank = jnp.sum(jnp.where(onehot, base + incl, 0.0), axis=0, keepdims=True) - 1.0
        carry = carry + total

        cls_ref[:, rows] = cls
        rank_ref[:, rows] = rank.astype(I32)
        w_rows = jnp.concatenate([w_lo, w_hi, jnp.zeros((LANES - 2, sub), F32)], axis=0)
        h2e_ref[rows, D:D + WCOLS] = jnp.transpose(w_rows)
    carry_ref[...] = carry
    cnt_ref[...] = carry


def _postmix(yg, ys, sa, sb, x, g1, wug, wus, wo, gain, sc, sh, wr, br, seq, tm):
    T, D = x.shape
    per_b = seq // tm
    nt = T // tm
    tok = lambda i: (i, 0)
    row = lambda i: (0, 0)
    bat = lambda i: (i // per_b, 0, 0)
    return pl.pallas_call(
        functools.partial(_postmix_kernel, tm=tm),
        grid=(nt,),
        in_specs=[
            pl.BlockSpec((tm, yg.shape[1]), tok),
            pl.BlockSpec((tm, ys.shape[1]), tok),
            pl.BlockSpec((tm, D), tok),
            pl.BlockSpec((tm, D), tok),
            pl.BlockSpec((tm, D), tok),
            pl.BlockSpec((None, 1, D), bat),
            pl.BlockSpec(wug.shape, row),
            pl.BlockSpec(wus.shape, row),
            pl.BlockSpec(wo.shape, row),
            pl.BlockSpec((1, D), row),
            pl.BlockSpec((None, 1, D), bat),
            pl.BlockSpec((None, 1, D), bat),
            pl.BlockSpec((D, LANES), row),
            pl.BlockSpec((N_EXPERTS, 1), row),
        ],
        out_specs=[
            pl.BlockSpec((tm, D), tok),
            pl.BlockSpec((tm, D + WCOLS), tok),
            pl.BlockSpec((None, 1, tm), lambda i: (i, 0, 0)),
            pl.BlockSpec((None, 1, tm), lambda i: (i, 0, 0)),
            pl.BlockSpec((CLASS_ROWS, LANES), row),
        ],
        out_shape=[
            jax.ShapeDtypeStruct((T, D), F32),
            jax.ShapeDtypeStruct((T, D + WCOLS), F32),
            jax.ShapeDtypeStruct((nt, 1, tm), I32),
            jax.ShapeDtypeStruct((nt, 1, tm), I32),
            jax.ShapeDtypeStruct((CLASS_ROWS, LANES), F32),
        ],
        scratch_shapes=[pltpu.VMEM((CLASS_ROWS, LANES), F32)],
        compiler_params=_params(("arbitrary",), VMEM_LIMIT),
        name="postmix",
    )(yg, ys, sa, sb, x, g1, wug, wus, wo, gain, sc, sh, wr, br)


def _plan_kernel(start_ref, cls_ref, rank_ref, dest_ref):
    cls = cls_ref[...]
    base = jnp.zeros(cls.shape, I32)
    for c in range(N_CLASSES):
        base = jnp.where(cls == c, start_ref[c], base)
    dest_ref[...] = base + rank_ref[...]


def _plan(start, cls, rank):
    nt, _, tm = cls.shape
    g = min(PLAN_TILES, nt)
    blk = pl.BlockSpec((g, 1, tm), lambda i, s: (i, 0, 0))
    return pl.pallas_call(
        _plan_kernel,
        grid_spec=pltpu.PrefetchScalarGridSpec(
            num_scalar_prefetch=1,
            grid=(nt // g,),
            in_specs=[blk, blk],
            out_specs=blk,
        ),
        out_shape=jax.ShapeDtypeStruct((nt, 1, tm), I32),
        compiler_params=_params(("arbitrary",)),
        name="plan",
    )(start, cls, rank)


ROW_UNROLL = 8


DISPATCH_SLOTS = 3


def _dispatch_kernel(dest_ref, h_ref, xs_in_ref, xs_ref, buf, load_sem, scat_sem, *, tm, nt):
    del xs_in_ref
    i = pl.program_id(0)

    def load(t, slot):
        rows = pl.ds(pl.multiple_of(t * tm, tm), tm)
        return pltpu.make_async_copy(h_ref.at[rows], buf.at[slot], load_sem.at[slot])

    def row_copy(slot, r, d):
        return pltpu.make_async_copy(buf.at[slot, pl.ds(r, 1)], xs_ref.at[pl.ds(d, 1)],
                                     scat_sem.at[slot])

    @pl.when(i == 0)
    def _():
        for t in range(min(DISPATCH_SLOTS - 1, nt)):
            load(t, t).start()

    for slot in range(DISPATCH_SLOTS):
        @pl.when((i < nt) & (lax.rem(i, DISPATCH_SLOTS) == slot))
        def _():
            load(i, slot).wait()
            for r in range(tm):
                row_copy(slot, r, dest_ref[0, r]).start(priority=r % 2)

    @pl.when(i > 0)
    def _():
        slot = lax.rem(i + DISPATCH_SLOTS - 1, DISPATCH_SLOTS)

        def drain(j, carry):
            for u in range(ROW_UNROLL):
                row_copy(slot, 0, 0).wait()
            return carry

        lax.fori_loop(0, tm // ROW_UNROLL, drain, 0)

    @pl.when(i + DISPATCH_SLOTS - 1 < nt)
    def _():
        t = i + DISPATCH_SLOTS - 1
        load(t, lax.rem(t, DISPATCH_SLOTS)).start()


def _dispatch(dest, h2e, xs_init):
    nt, _, tm = dest.shape
    width = h2e.shape[1]
    return pl.pallas_call(
        functools.partial(_dispatch_kernel, tm=tm, nt=nt),
        grid=(nt + 1,),
        in_specs=[
            pl.BlockSpec((None, 1, tm), lambda i: (jnp.minimum(i, nt - 1), 0, 0),
                         memory_space=pltpu.SMEM),
            pl.BlockSpec(memory_space=pl.ANY),
            pl.BlockSpec(memory_space=pl.ANY),
        ],
        out_specs=pl.BlockSpec(memory_space=pl.ANY),
        out_shape=jax.ShapeDtypeStruct(xs_init.shape, F32),
        scratch_shapes=[pltpu.VMEM((DISPATCH_SLOTS, tm, width), F32),
                        pltpu.SemaphoreType.DMA((DISPATCH_SLOTS,)),
                        pltpu.SemaphoreType.DMA((DISPATCH_SLOTS,))],
        input_output_aliases={2: 0},
        compiler_params=_params(("arbitrary",)),
        name="dispatch",
    )(dest, h2e, xs_init)


def _moe_kernel(elo_ref, ehi_ref, nact_ref, xs_ref, wg0, wu0, wd0, wg1, wu1, wd1, o_ref):
    j = pl.program_id(0)
    D = o_ref.shape[1]

    @pl.when(j < nact_ref[0])
    def _():
        x = xs_ref[:, 0:D].astype(BF16)
        w_lo = xs_ref[:, D:D + 1]
        w_hi = xs_ref[:, D + 1:D + 2]

        gates = [_dot(x, wg[...]) for wg in (wg0, wg1)]
        ups = [_dot(x, wu[...]) for wu in (wu0, wu1)]
        acts = [((g * jax.nn.sigmoid(g)) * u).astype(BF16) for g, u in zip(gates, ups)]
        y_lo, y_hi = [_dot(a, wd[...]) for a, wd in zip(acts, (wd0, wd1))]
        o_ref[...] = w_lo * y_lo + w_hi * y_hi

    @pl.when(j >= nact_ref[0])
    def _():
        o_ref[...] = jnp.zeros_like(o_ref)


def _moe(elo, ehi, nact, xs, w_gate, w_up, w_down, layer):
    ns, width = xs.shape
    D = width - WCOLS
    nb = ns // MOE_ROWS
    lo = lambda j, a, b, n: (layer, a[j], 0, 0)
    hi = lambda j, a, b, n: (layer, b[j], 0, 0)
    gu = (None, None, D, D_FF)
    dn = (None, None, D_FF, D)
    return pl.pallas_call(
        _moe_kernel,
        grid_spec=pltpu.PrefetchScalarGridSpec(
            num_scalar_prefetch=3,
            grid=(nb,),
            in_specs=[
                pl.BlockSpec((MOE_ROWS, width), lambda j, a, b, n: (jnp.minimum(j, n[0] - 1), 0)),
                pl.BlockSpec(gu, lo), pl.BlockSpec(gu, lo), pl.BlockSpec(dn, lo),
                pl.BlockSpec(gu, hi), pl.BlockSpec(gu, hi), pl.BlockSpec(dn, hi),
            ],
            out_specs=pl.BlockSpec((MOE_ROWS, D), lambda j, a, b, n: (j, 0)),
        ),
        out_shape=jax.ShapeDtypeStruct((ns, D), F32),
        compiler_params=_params(("arbitrary",), VMEM_LIMIT),
        name="moe",
    )(elo, ehi, nact, xs, w_gate, w_up, w_down, w_gate, w_up, w_down)


def _combine_kernel(dest_ref, x1_ref, g2_ref, ys_ref, o_ref, buf, sem, *, tm, nt):
    i = pl.program_id(0)

    def row_copy(slot, r, d):
        return pltpu.make_async_copy(ys_ref.at[pl.ds(d, 1)], buf.at[slot, pl.ds(r, 1)],
                                     sem.at[slot])

    for slot in range(2):
        @pl.when((i < nt) & (lax.rem(i, 2) == slot))
        def _():
            for r in range(tm):
                row_copy(slot, r, dest_ref[0, r]).start(priority=r % 2)

    @pl.when(i > 0)
    def _():
        slot = lax.rem(i + 1, 2)

        def drain(j, carry):
            for u in range(ROW_UNROLL):
                row_copy(slot, 0, 0).wait()
            return carry

        lax.fori_loop(0, tm // ROW_UNROLL, drain, 0)
        o_ref[...] = x1_ref[...] + g2_ref[...] * buf[slot]


def _combine(dest, x1, g2, ys, seq):
    nt, _, tm = dest.shape
    T, D = x1.shape
    per_b = seq // tm
    done = lambda i: jnp.maximum(i - 1, 0)
    return pl.pallas_call(
        functools.partial(_combine_kernel, tm=tm, nt=nt),
        grid=(nt + 1,),
        in_specs=[
            pl.BlockSpec((None, 1, tm), lambda i: (jnp.minimum(i, nt - 1), 0, 0),
                         memory_space=pltpu.SMEM),
            pl.BlockSpec((tm, D), lambda i: (done(i), 0)),
            pl.BlockSpec((None, 1, D), lambda i: (done(i) // per_b, 0, 0)),
            pl.BlockSpec(memory_space=pl.ANY),
        ],
        out_specs=pl.BlockSpec((tm, D), lambda i: (done(i), 0)),
        out_shape=jax.ShapeDtypeStruct((T, D), F32),
        scratch_shapes=[pltpu.VMEM((2, tm, D), F32), pltpu.SemaphoreType.DMA((2,))],
        compiler_params=_params(("arbitrary",)),
        name="combine",
    )(dest, x1, g2, ys)


def _block_plan(counts, nb):
    cnt = counts[:N_CLASSES, 0].astype(I32)
    nblk = (cnt + MOE_ROWS - 1) // MOE_ROWS
    cum = jnp.cumsum(nblk)
    start = ((cum - nblk) * MOE_ROWS).astype(I32)
    nact = cum[-1:].astype(I32)
    blk = jnp.arange(nb, dtype=I32)
    cls = jnp.sum(jnp.minimum(blk, nact[0] - 1)[:, None] >= cum[None, :], axis=1).astype(I32)
    cls = jnp.minimum(cls, N_CLASSES - 1)
    lo_tab = jnp.array([p[0] for p in PAIRS], I32)
    hi_tab = jnp.array([p[1] for p in PAIRS], I32)
    grp = cls // len(PAIRS)
    elo = grp * EXPERTS_PER_GROUP + lo_tab[cls % len(PAIRS)]
    ehi = grp * EXPERTS_PER_GROUP + hi_tab[cls % len(PAIRS)]
    start = jnp.concatenate([start, jnp.zeros((CLASS_ROWS - N_CLASSES,), I32)])
    return start, elo, ehi, nact


def kernel(x, c, positions, w_ada, b_ada, norm_mix, w_in, w_alpha2, b_alpha, gla_norm, q_norm,
           k_norm, sinks, w_up_gla, w_up_swa, w_out, norm_ffn, w_router, b_router, w_gate, w_up,
           w_down):
    B, S, D = x.shape
    L = w_ada.shape[0]
    T = B * S
    tm = min(512, S)

    mod = _modulation(c, w_ada, b_ada).reshape(L, B, 6, 1, D)
    cos_t, sg_t = _rope_tables(positions)

    sizes = (256, 256, 512, 16, 512, 512, 128, 128, 1024, 1024)
    offs = [0]
    for s in sizes:
        offs.append(offs[-1] + s)
    order = (0, 1, 2, 4, 5, 6, 7, 8, 9, 3)
    w_in_p = jnp.concatenate(
        [w_in[:, :, offs[i]:offs[i + 1]] for i in order]
        + [jnp.zeros((L, D, LANES - GLA_RANK), F32)], axis=-1).astype(BF16)
    wa2_p = jnp.concatenate(
        [w_alpha2, jnp.zeros((L, LANES - GLA_RANK, w_alpha2.shape[-1]), F32)], axis=1).astype(BF16)
    wr = jnp.concatenate(
        [w_router.astype(BF16), jnp.zeros((D, LANES - N_EXPERTS), BF16)], axis=1)
    br = b_router.reshape(N_EXPERTS, 1)
    wug = w_up_gla.astype(BF16)
    wus = w_up_swa.astype(BF16)
    wo = w_out.astype(BF16)
    wg = w_gate.astype(BF16)
    wu = w_up.astype(BF16)
    wd = w_down.astype(BF16)

    nb = T // MOE_ROWS + N_CLASSES
    xs = jnp.zeros((nb * MOE_ROWS, D + WCOLS), F32)
    xt = x.reshape(T, D)
    pending = None
    for l in range(L):
        sh1, sc1, g1, sh2, sc2, g2 = [mod[l, :, i] for i in range(6)]
        premix_args = (norm_mix[l].reshape(1, D), sc1, sh1, w_in_p[l], wa2_p[l],
                       b_alpha[l].reshape(1, -1))
        if pending is None:
            gq, gk, gv, la, rs, sq, skv, sa, sb = _premix(xt, *premix_args, S, tm)
        else:
            gq, gk, gv, la, rs, sq, skv, sa, sb, xt = _premix_combine(*pending, *premix_args, S)
        qg = jnp.tile(q_norm[l], LANES // SWA_HD).reshape(1, LANES)
        kg = jnp.tile(k_norm[l], LANES // SWA_HD).reshape(1, LANES)
        sinks_b = jnp.broadcast_to(sinks[l][:, None], (SWA_HEADS, LANES))
        y_gla = _gla(gq, gk, gv, la, rs, gla_norm[l].reshape(1, -1), B, S, min(GLA_TILE, S))
        y_swa = _swa(sq, skv, cos_t, sg_t, qg, kg, sinks_b, B, S, tm)
        x1, h2e, cls, rank, counts = _postmix(
            y_gla, y_swa, sa, sb, xt, g1, wug[l], wus[l], wo[l],
            norm_ffn[l].reshape(1, D), sc2, sh2, wr, br, S, tm)
        start, elo, ehi, nact = _block_plan(counts, nb)
        dest = _plan(start, cls, rank)
        xs = _dispatch(dest, h2e, xs)
        ys = _moe(elo, ehi, nact, xs, wg, wu, wd, l)
        pending = (dest, x1, g2, ys)
    xt = _combine(*pending, S)
    return xt.reshape(B, S, D)
```

```python
import functools

import jax
import jax.numpy as jnp
from jax import lax
from jax.experimental import pallas as pl
from jax.experimental.pallas import tpu as pltpu

F32 = jnp.float32
BF16 = jnp.bfloat16
I32 = jnp.int32
HIGHEST = lax.Precision.HIGHEST

GLA_HEADS = 4
GLA_DK = 64
GLA_DV = 128
GLA_RANK = 16
GLA_TAU = 16.0
GLA_CHUNK = 64
SWA_HEADS = 8
SWA_KV_HEADS = 2
SWA_HD = 64
WINDOW = 128
ROPE_DIMS = SWA_HD // 4
ROPE_THETA = 500000.0
N_EXPERTS = 16
N_GROUPS = 4
EXPERTS_PER_GROUP = 4
D_FF = 512
EPS = 1e-6

LANES = 128
SUBLANES = 8
VMEM_LIMIT = 56 * 1024 * 1024

PAIRS = ((0, 1), (0, 2), (0, 3), (1, 2), (1, 3), (2, 3))
N_CLASSES = N_GROUPS * len(PAIRS)
CLASS_ROWS = 32
MOE_ROWS = 512
POSTMIX_SUBTILES = 2
PREMIX_SUBTILES = 1
GLA_TILE = 1024
GLA_WAVE = 4
PLAN_TILES = 16
WCOLS = LANES


def _params(sem, vmem=None):
    return pltpu.CompilerParams(dimension_semantics=sem, vmem_limit_bytes=vmem)


def _dot(a, b):
    return jnp.dot(a, b, preferred_element_type=F32)


def _iota(shape, axis):
    return lax.broadcasted_iota(I32, shape, axis)


def _mod_kernel(c_ref, w_ref, b_ref, o_ref):
    c = c_ref[...]
    cond = c * jax.nn.sigmoid(c)
    o_ref[...] = jnp.dot(cond, w_ref[...], precision=HIGHEST,
                         preferred_element_type=F32) + b_ref[...]


def _modulation(c, w_ada, b_ada):
    L, D, D6 = w_ada.shape
    B = c.shape[0]
    nj = D6 // D
    return pl.pallas_call(
        _mod_kernel,
        grid=(L, nj),
        in_specs=[
            pl.BlockSpec((B, D), lambda l, j: (0, 0)),
            pl.BlockSpec((None, D, D), lambda l, j: (l, 0, j)),
            pl.BlockSpec((None, 1, D), lambda l, j: (l, 0, j)),
        ],
        out_specs=pl.BlockSpec((None, B, D), lambda l, j: (l, 0, j)),
        out_shape=jax.ShapeDtypeStruct((L, B, D6), F32),
        compiler_params=_params(("arbitrary", "arbitrary")),
        name="modulation",
    )(c, w_ada, b_ada.reshape(L, 1, D6))


def _rope_kernel(pos_ref, inv_ref, c_ref, s_ref):
    ang = pos_ref[...].astype(F32) * inv_ref[...]
    l64 = _iota(ang.shape, 1) & (SWA_HD - 1)
    half = ROPE_DIMS // 2
    cos = jnp.cos(ang)
    sin = jnp.sin(ang)
    c_ref[...] = jnp.where(l64 < ROPE_DIMS, cos, 1.0)
    s_ref[...] = jnp.where(l64 < half, -sin, jnp.where(l64 < ROPE_DIMS, sin, 0.0))


def _rope_tables(positions):
    T = positions.size
    half = ROPE_DIMS // 2
    inv_freq = jnp.power(ROPE_THETA, -jnp.arange(half, dtype=F32) / half)
    lane = jnp.arange(LANES)
    inv_lane = inv_freq[(lane % SWA_HD) % half].reshape(1, LANES)
    tm = min(1024, T)
    return pl.pallas_call(
        _rope_kernel,
        grid=(T // tm,),
        in_specs=[pl.BlockSpec((tm, 1), lambda i: (i, 0)),
                  pl.BlockSpec((1, LANES), lambda i: (0, 0))],
        out_specs=[pl.BlockSpec((tm, LANES), lambda i: (i, 0))] * 2,
        out_shape=[jax.ShapeDtypeStruct((T, LANES), F32)] * 2,
        compiler_params=_params(("arbitrary",)),
        name="rope_tables",
    )(positions.reshape(T, 1), inv_lane)


_GQ, _GK, _GV, _GR = 0, 256, 512, 1024
_SQ, _SK, _SV = 1536, 2048, 2176
_MA, _MB, _GA, _WIN = 2304, 3328, 4352, 4480


def _premix_groups(x_ref, gain_ref, sc_ref, sh_ref, w_ref, wa2_ref, ba_ref,
                   gq_ref, gk_ref, gv_ref, la_ref, gr_ref, sq_ref, skv_ref, ma_ref, mb_ref):
    sub = x_ref.shape[0] // PREMIX_SUBTILES
    groups = [slice(s * sub, (s + 1) * sub) for s in range(PREMIX_SUBTILES)]
    half = (_MB - _MA) // 2
    hbs = []
    for r in groups:
        x = x_ref[r, :]
        ms = jnp.mean(x * x, axis=-1, keepdims=True)
        h = x * lax.rsqrt(ms + EPS) * gain_ref[...]
        h = h * (1.0 + sc_ref[...]) + sh_ref[...]
        hbs.append(h.astype(BF16))

    def sec(lo, hi):
        return zip(groups, [_dot(hb, w_ref[:, lo:hi]) for hb in hbs])

    for r, z in sec(_GQ, _GK):
        gq_ref[r, :] = (z * (GLA_DK ** -0.5)).astype(BF16)
    for r, z in sec(_GK, _GV):
        gk_ref[r, :] = z.astype(BF16)
    for r, z in sec(_GV, _GR):
        gv_ref[r, :] = z.astype(BF16)
    for r, z in sec(_GR, _SQ):
        gr_ref[r, :] = (z * jax.nn.sigmoid(z)).astype(BF16)
    for r, z in sec(_SQ, _SK):
        sq_ref[r, :] = z.astype(BF16)
    for r, z in sec(_SK, _MA):
        skv_ref[r, :] = z.astype(BF16)
    for c in range(2):
        cols = slice(c * half, (c + 1) * half)
        for r, z in sec(_MA + c * half, _MA + (c + 1) * half):
            ma_ref[r, cols] = jax.nn.sigmoid(z).astype(BF16)
        for r, z in sec(_MB + c * half, _MB + (c + 1) * half):
            mb_ref[r, cols] = jax.nn.sigmoid(z).astype(BF16)
    for r, a_low in sec(_GA, _WIN):
        z = _dot(a_low.astype(BF16), wa2_ref[...]) + ba_ref[...]
        log_sig = jnp.minimum(z, 0.0) - jnp.log1p(jnp.exp(-jnp.abs(z)))
        la_ref[r, :] = log_sig * (1.0 / GLA_TAU)


def _premix_kernel(x_ref, *refs):
    _premix_groups(x_ref, *refs)


def _premix_combine_kernel(dnext_ref, dfirst_ref, x1_ref, g2_ref, ys_ref, *refs, nt):
    refs, x_ref, buf, sem = refs[:-3], refs[-3], refs[-2], refs[-1]
    i = pl.program_id(0)
    tm = x1_ref.shape[0]
    slot = lax.rem(i, 2)

    def row_copy(s_, r, d):
        return pltpu.make_async_copy(ys_ref.at[pl.ds(d, 1)], buf.at[s_, pl.ds(r, 1)],
                                     sem.at[s_])

    def issue(dest_ref, s_):
        for r in range(tm):
            row_copy(s_, r, dest_ref[0, r]).start(priority=r % 2)

    def drain(s_):
        def body(j, carry):
            for u in range(ROW_UNROLL):
                row_copy(s_, 0, 0).wait()
            return carry
        lax.fori_loop(0, tm // ROW_UNROLL, body, 0)

    @pl.when(i == 0)
    def _():
        issue(dfirst_ref, 0)

    drain(slot)
    for s_ in range(2):
        @pl.when(slot != s_)
        def _():
            issue(dnext_ref, s_)
    x_ref[...] = x1_ref[...] + g2_ref[...] * buf[slot]
    _premix_groups(x_ref, *refs)

    @pl.when(i == nt - 1)
    def _():
        drain(1 - slot)


_PREMIX_WIDTHS = (256, 256, 512, 256, 512, 512, 256, 1024, 1024)
_PREMIX_DTYPES = (BF16, BF16, BF16, F32, BF16, BF16, BF16, BF16, BF16)


def _premix_weight_specs(D):
    row = lambda i: (0, 0)
    return [pl.BlockSpec((D, _WIN), row), pl.BlockSpec((LANES, 256), row),
            pl.BlockSpec((1, 256), row)]


def _premix(x, gain, sc, sh, w_in_p, wa2_p, b_alpha, seq, tm):
    T, D = x.shape
    per_b = seq // tm
    tok = lambda i: (i, 0)
    bat = lambda i: (i // per_b, 0, 0)
    return pl.pallas_call(
        _premix_kernel,
        grid=(T // tm,),
        in_specs=[
            pl.BlockSpec((tm, D), tok),
            pl.BlockSpec((1, D), lambda i: (0, 0)),
            pl.BlockSpec((None, 1, D), bat),
            pl.BlockSpec((None, 1, D), bat),
        ] + _premix_weight_specs(D),
        out_specs=[pl.BlockSpec((tm, w), tok) for w in _PREMIX_WIDTHS],
        out_shape=[jax.ShapeDtypeStruct((T, w), dt)
                   for w, dt in zip(_PREMIX_WIDTHS, _PREMIX_DTYPES)],
        compiler_params=_params(("arbitrary",), VMEM_LIMIT),
        name="premix",
    )(x, gain, sc, sh, w_in_p, wa2_p, b_alpha)


def _premix_combine(dest, x1, g2, ys, gain, sc, sh, w_in_p, wa2_p, b_alpha, seq):
    nt, _, tm = dest.shape
    T, D = x1.shape
    per_b = seq // tm
    tok = lambda i: (i, 0)
    bat = lambda i: (i // per_b, 0, 0)
    smem = pltpu.SMEM
    widths = _PREMIX_WIDTHS + (D,)
    dtypes = _PREMIX_DTYPES + (F32,)
    return pl.pallas_call(
        functools.partial(_premix_combine_kernel, nt=nt),
        grid=(nt,),
        in_specs=[
            pl.BlockSpec((None, 1, tm), lambda i: (jnp.minimum(i + 1, nt - 1), 0, 0),
                         memory_space=smem),
            pl.BlockSpec((None, 1, tm), lambda i: (0, 0, 0), memory_space=smem),
            pl.BlockSpec((tm, D), tok),
            pl.BlockSpec((None, 1, D), bat),
            pl.BlockSpec(memory_space=pl.ANY),
            pl.BlockSpec((1, D), lambda i: (0, 0)),
            pl.BlockSpec((None, 1, D), bat),
            pl.BlockSpec((None, 1, D), bat),
        ] + _premix_weight_specs(D),
        out_specs=[pl.BlockSpec((tm, w), tok) for w in widths],
        out_shape=[jax.ShapeDtypeStruct((T, w), dt) for w, dt in zip(widths, dtypes)],
        scratch_shapes=[pltpu.VMEM((2, tm, D), F32), pltpu.SemaphoreType.DMA((2,))],
        compiler_params=_params(("arbitrary",), VMEM_LIMIT),
        name="premix_combine",
    )(dest, dest, x1, g2, ys, gain, sc, sh, w_in_p, wa2_p, b_alpha)


def _gla_kernel(q_ref, k_ref, v_ref, la_ref, rs_ref, gn_ref, o_ref, s_ref, *, nchunk):
    C = GLA_CHUNK
    H = GLA_HEADS
    HK = H * GLA_DK

    @pl.when(pl.program_id(1) == 0)
    def _():
        s_ref[...] = jnp.zeros_like(s_ref)

    tri = jnp.where(_iota((C, C), 1) <= _iota((C, C), 0), 1.0, 0.0).astype(BF16)
    causal = (_iota((C, HK), 1) & (C - 1)) <= _iota((C, HK), 0)
    lane_head = _iota((1, HK), 1) >> 6
    ones = jnp.ones((C, LANES), BF16)
    zero_blk = jnp.zeros((C, GLA_DV), BF16)
    gain = gn_ref[...]
    tn = (((0,), (0,)), ((), ()))
    nt = (((1,), (1,)), ((), ()))

    def block_diag(blocks):
        rows = [jnp.concatenate([zero_blk] * h + [blk] + [zero_blk] * (H - 1 - h), axis=1)
                for h, blk in enumerate(blocks)]
        return jnp.concatenate(rows, axis=0)

    lhs, v_bds, kvs, decays = [], [], [], []
    for w0 in range(0, nchunk, GLA_WAVE):
        wave = range(w0, min(w0 + GLA_WAVE, nchunk))
        bs, ds = [], []
        for c in wave:
            la = la_ref[c * C:(c + 1) * C, :]
            la_hi = la.astype(BF16)
            la_lo = (la - la_hi.astype(F32)).astype(BF16)
            bs.append(_dot(tri, la_hi) + _dot(tri, la_lo))
            ds.append(lax.dot_general(la_hi, ones, tn, preferred_element_type=F32)
                      + lax.dot_general(la_lo, ones, tn, preferred_element_type=F32))
        atts, q_decs = [], []
        for c, b, d_col in zip(wave, bs, ds):
            rows = slice(c * C, (c + 1) * C)
            decays.append(jnp.exp(d_col))
            q = q_ref[rows, :].astype(F32)
            k = k_ref[rows, :].astype(F32)
            v = v_ref[rows, :]
            q_dec = (q * jnp.exp(b)).astype(BF16)
            k_inv = k * jnp.exp(-b)
            k_end = (k_inv * jnp.exp(b[C - 1:C, :])).astype(BF16)
            k_inv = k_inv.astype(BF16)
            k_bd = jnp.concatenate(
                [jnp.where(lane_head == h, k_inv, jnp.zeros_like(k_inv)) for h in range(H)],
                axis=0)
            atts.append(lax.dot_general(q_dec, k_bd, nt, preferred_element_type=F32))
            q_decs.append(q_dec)
            v_bds.append(block_diag([v[:, h * GLA_DV:(h + 1) * GLA_DV] for h in range(H)]))
            kvs.append([lax.dot_general(k_end[:, p * LANES:(p + 1) * LANES],
                                        v[:, 2 * p * GLA_DV:(2 * p + 2) * GLA_DV], tn,
                                        preferred_element_type=F32) for p in range(H // 2)])
        for att, q_dec in zip(atts, q_decs):
            att = jnp.where(causal, att, 0.0).astype(BF16)
            lhs.append(jnp.concatenate([att, q_dec], axis=1))

    state = [s_ref[h] for h in range(H)]
    states = []
    for c in range(nchunk):
        states.append(state)
        state = [decays[c][h * GLA_DK:(h + 1) * GLA_DK, :] * state[h]
                 + kvs[c][h // 2][(h % 2) * GLA_DK:(h % 2 + 1) * GLA_DK,
                                  (h % 2) * GLA_DV:(h % 2 + 1) * GLA_DV]
                 for h in range(H)]
    for h in range(H):
        s_ref[h] = state[h]

    for w0 in range(0, nchunk, GLA_WAVE):
        wave = range(w0, min(w0 + GLA_WAVE, nchunk))
        os_ = []
        for c in wave:
            rhs = jnp.concatenate(
                [v_bds[c], block_diag([s.astype(BF16) for s in states[c]])], axis=0)
            os_.append(_dot(lhs[c], rhs))
        for c, o in zip(wave, os_):
            rows = slice(c * C, (c + 1) * C)
            outs = []
            for h in range(H):
                oh = o[:, h * GLA_DV:(h + 1) * GLA_DV]
                ms = jnp.mean(oh * oh, axis=-1, keepdims=True)
                outs.append(oh * lax.rsqrt(ms + EPS) * gain[:, h * GLA_DV:(h + 1) * GLA_DV])
            y = jnp.concatenate(outs, axis=1) * rs_ref[rows, :].astype(F32)
            o_ref[rows, :] = y.astype(BF16)


def _swa_kernel(q_ref, kc_ref, kp_ref, vc_ref, vp_ref, cc_ref, sc_ref, cp_ref, sp_ref,
                qg_ref, kg_ref, sink_ref, o_ref, *, tq):
    W = WINDOW
    nw = tq // W
    G = SWA_HEADS // SWA_KV_HEADS
    step = pl.program_id(1)

    lane = _iota((1, LANES), 1)
    lo_half = lane < SWA_HD
    first = (lane & (SWA_HD - 1)) < (ROPE_DIMS // 2)
    seg = ((_iota((LANES, LANES), 0) >> 6) == (_iota((LANES, LANES), 1) >> 6))
    seg_mean = jnp.where(seg, 1.0 / SWA_HD, 0.0).astype(BF16)

    def norm_rope(x, cos, sg, gain):
        ms = _dot((x * x).astype(BF16), seg_mean)
        y = x * lax.rsqrt(ms + EPS) * gain
        partner = jnp.where(first, pltpu.roll(y, LANES - ROPE_DIMS // 2, 1),
                            pltpu.roll(y, ROPE_DIMS // 2, 1))
        return y * cos + partner * sg

    cos_c = cc_ref[...]
    sg_c = sc_ref[...]
    cos_k = jnp.concatenate([cp_ref[...], cos_c], axis=0)
    sg_k = jnp.concatenate([sp_ref[...], sg_c], axis=0)
    kb = jnp.concatenate([kp_ref[...], kc_ref[...]], axis=0).astype(F32)
    kb = norm_rope(kb, cos_k, sg_k, kg_ref[...])
    kb_r = pltpu.roll(kb, SWA_HD, 1)
    vb = jnp.concatenate([vp_ref[...], vc_ref[...]], axis=0).astype(F32)
    vb_r = pltpu.roll(vb, SWA_HD, 1)
    k_dup = (jnp.where(lo_half, kb, kb_r).astype(BF16), jnp.where(lo_half, kb_r, kb).astype(BF16))
    v_dup = (jnp.where(lo_half, vb, vb_r).astype(BF16), jnp.where(lo_half, vb_r, vb).astype(BF16))

    q_cols = []
    for c in range(SWA_HEADS // 2):
        qc = q_ref[:, c * LANES:(c + 1) * LANES].astype(F32)
        q_cols.append(norm_rope(qc, cos_c, sg_c, qg_ref[...]) * (SWA_HD ** -0.5))

    cur_side = _iota((W, W), 0) <= _iota((W, W), 1)
    cur_side4 = jnp.concatenate([cur_side] * G, axis=1)
    sinks = sink_ref[...]
    nt = (((1,), (1,)), ((), ()))
    tn = (((0,), (0,)), ((), ()))

    for w in range(nw):
        for kvh in range(SWA_KV_HEADS):
            parts = []
            for cc in range(G // 2):
                qw = q_cols[kvh * (G // 2) + cc][w * W:(w + 1) * W, :]
                parts.append(jnp.where(lo_half, qw, 0.0))
                parts.append(jnp.where(lo_half, 0.0, qw))
            q_stack = jnp.concatenate(parts, axis=0).astype(BF16)
            k_prev = k_dup[kvh][w * W:(w + 1) * W, :]
            k_cur = k_dup[kvh][(w + 1) * W:(w + 2) * W, :]
            v_prev = v_dup[kvh][w * W:(w + 1) * W, :]
            v_cur = v_dup[kvh][(w + 1) * W:(w + 2) * W, :]
            s_cur = lax.dot_general(k_cur, q_stack, nt, preferred_element_type=F32)
            s_prev = lax.dot_general(k_prev, q_stack, nt, preferred_element_type=F32)
            if w == 0:
                s_prev = jnp.where(step > 0, s_prev, -1e30)
            s = jnp.where(cur_side4, s_cur, s_prev)
            sink = jnp.concatenate(
                [sinks[kvh * G + g:kvh * G + g + 1, :] for g in range(G)], axis=1)
            m = jnp.maximum(jnp.max(s, axis=0, keepdims=True), sink)
            p = jnp.exp(s - m)
            den = jnp.sum(p, axis=0, keepdims=True) + jnp.exp(sink - m)
            p = (p * (1.0 / den)).astype(BF16)
            zero = jnp.zeros_like(p)
            o = (lax.dot_general(jnp.where(cur_side4, p, zero), v_cur, tn,
                                 preferred_element_type=F32)
                 + lax.dot_general(jnp.where(cur_side4, zero, p), v_prev, tn,
                                   preferred_element_type=F32))
            og = [o[g * W:(g + 1) * W, :] for g in range(G)]
            for cc in range(G // 2):
                col = jnp.where(lo_half, og[2 * cc], og[2 * cc + 1])
                c0 = (kvh * (G // 2) + cc) * LANES
                o_ref[w * W:(w + 1) * W, c0:c0 + LANES] = col.astype(BF16)


def _gla(gq, gk, gv, la, rs, gn, batch, seq, ts):
    T = gq.shape[0]
    per_b = seq // ts
    tok = lambda b, i: (b * per_b + i, 0)
    HK = GLA_HEADS * GLA_DK
    HV = GLA_HEADS * GLA_DV
    return pl.pallas_call(
        functools.partial(_gla_kernel, nchunk=ts // GLA_CHUNK),
        grid=(batch, per_b),
        in_specs=[
            pl.BlockSpec((ts, HK), tok),
            pl.BlockSpec((ts, HK), tok),
            pl.BlockSpec((ts, HV), tok),
            pl.BlockSpec((ts, HK), tok),
            pl.BlockSpec((ts, HV), tok),
            pl.BlockSpec((1, HV), lambda b, i: (0, 0)),
        ],
        out_specs=pl.BlockSpec((ts, HV), tok),
        out_shape=jax.ShapeDtypeStruct((T, HV), BF16),
        scratch_shapes=[pltpu.VMEM((GLA_HEADS, GLA_DK, GLA_DV), F32)],
        compiler_params=_params(("arbitrary", "arbitrary")),
        name="gla",
    )(gq, gk, gv, la, rs, gn)


def _swa(sq, skv, cos_t, sg_t, qg, kg, sinks_b, batch, seq, tq):
    T = sq.shape[0]
    per_b = seq // tq
    r = tq // WINDOW
    cur = lambda b, i: (b * per_b + i, 0)
    prev = lambda b, i: (jnp.maximum((b * per_b + i) * r - 1, 0), 0)
    cur_v = lambda b, i: (b * per_b + i, 1)
    prev_v = lambda b, i: (jnp.maximum((b * per_b + i) * r - 1, 0), 1)
    const = lambda b, i: (0, 0)
    QW = SWA_HEADS * SWA_HD
    return pl.pallas_call(
        functools.partial(_swa_kernel, tq=tq),
        grid=(batch, per_b),
        in_specs=[
            pl.BlockSpec((tq, QW), cur),
            pl.BlockSpec((tq, LANES), cur),
            pl.BlockSpec((WINDOW, LANES), prev),
            pl.BlockSpec((tq, LANES), cur_v),
            pl.BlockSpec((WINDOW, LANES), prev_v),
            pl.BlockSpec((tq, LANES), cur),
            pl.BlockSpec((tq, LANES), cur),
            pl.BlockSpec((WINDOW, LANES), prev),
            pl.BlockSpec((WINDOW, LANES), prev),
            pl.BlockSpec((1, LANES), const),
            pl.BlockSpec((1, LANES), const),
            pl.BlockSpec((SWA_HEADS, LANES), const),
        ],
        out_specs=pl.BlockSpec((tq, QW), cur),
        out_shape=jax.ShapeDtypeStruct((T, QW), BF16),
        compiler_params=_params(("arbitrary", "arbitrary")),
        name="swa",
    )(sq, skv, skv, skv, skv, cos_t, sg_t, cos_t, sg_t, qg, kg, sinks_b)


def _second_largest(a, b, c, d):
    hi1, lo1 = jnp.maximum(a, b), jnp.minimum(a, b)
    hi2, lo2 = jnp.maximum(c, d), jnp.minimum(c, d)
    return jnp.maximum(hi1, hi2), jnp.maximum(jnp.minimum(hi1, hi2), jnp.maximum(lo1, lo2))


def _argmax4(vals):
    best, idx = vals[0], jnp.zeros(vals[0].shape, I32)
    for k in range(1, 4):
        upd = vals[k] > best
        idx = jnp.where(upd, k, idx)
        best = jnp.where(upd, vals[k], best)
    return idx, best


def _pick4(idx, vals):
    return jnp.where(idx == 0, vals[0],
                     jnp.where(idx == 1, vals[1], jnp.where(idx == 2, vals[2], vals[3])))


def _postmix_kernel(yg_ref, ys_ref, sa_ref, sb_ref, x_ref, g1_ref, wug_ref, wus_ref, wo_ref,
                    gain_ref, sc_ref, sh_ref, wr_ref, br_ref,
                    x1_ref, h2e_ref, cls_ref, rank_ref, cnt_ref, carry_ref, *, tm):
    @pl.when(pl.program_id(0) == 0)
    def _():
        carry_ref[...] = jnp.zeros_like(carry_ref)

    D = x_ref.shape[1]
    sub = tm // POSTMIX_SUBTILES
    upper = jnp.where(_iota((sub, sub), 0) <= _iota((sub, sub), 1), 1.0, 0.0).astype(BF16)
    carry = carry_ref[...]
    scale = gain_ref[...] * (1.0 + sc_ref[...])
    groups = [slice(s * sub, (s + 1) * sub) for s in range(POSTMIX_SUBTILES)]
    us = [_dot(yg_ref[r, :], wug_ref[...]).astype(BF16) for r in groups]
    vs = [_dot(ys_ref[r, :], wus_ref[...]).astype(BF16) for r in groups]
    mixed = [_dot(sa_ref[r, :] * u + sb_ref[r, :] * v, wo_ref[...])
             for r, u, v in zip(groups, us, vs)]
    h2s = []
    for r, y in zip(groups, mixed):
        x1 = x_ref[r, :] + g1_ref[...] * y
        x1_ref[r, :] = x1
        ms = jnp.mean(x1 * x1, axis=-1, keepdims=True)
        h2 = x1 * lax.rsqrt(ms + EPS) * scale + sh_ref[...]
        h2e_ref[r, 0:D] = h2
        h2s.append(h2.astype(BF16))
    all_logits = [_dot(h, wr_ref[...]) for h in h2s]
    for rows, logits in zip(groups, all_logits):
        aff = jnp.transpose(jax.nn.sigmoid(logits))[0:N_EXPERTS, :]
        sel = aff + br_ref[...]
        aff_r = [aff[e:e + 1, :] for e in range(N_EXPERTS)]
        sel_r = [sel[e:e + 1, :] for e in range(N_EXPERTS)]

        scores = []
        for g in range(N_GROUPS):
            m1, m2 = _second_largest(*sel_r[4 * g:4 * g + 4])
            scores.append(m1 + m2)
        grp, _ = _argmax4(scores)
        sel_g = [_pick4(grp, [sel_r[4 * g + k] for g in range(N_GROUPS)]) for k in range(4)]
        aff_g = [_pick4(grp, [aff_r[4 * g + k] for g in range(N_GROUPS)]) for k in range(4)]
        l1, _ = _argmax4(sel_g)
        masked = [jnp.where(l1 == k, -jnp.inf, sel_g[k]) for k in range(4)]
        l2, _ = _argmax4(masked)
        a1 = _pick4(l1, aff_g)
        a2 = _pick4(l2, aff_g)
        den = a1 + a2
        w1 = a1 / den
        w2 = a2 / den
        lo_e = jnp.minimum(l1, l2)
        hi_e = jnp.maximum(l1, l2)
        pair = jnp.where(lo_e == 0, hi_e - 1, jnp.where(lo_e == 1, hi_e + 1, 5))
        cls = grp * len(PAIRS) + pair
        w_lo = jnp.where(l1 < l2, w1, w2)
        w_hi = jnp.where(l1 < l2, w2, w1)

        onehot = _iota((CLASS_ROWS, sub), 0) == cls
        oh = jnp.where(onehot, 1.0, 0.0).astype(BF16)
        incl = _dot(oh, upper)
        total = _dot(oh, jnp.ones((sub, LANES), BF16))
        base = jnp.concatenate([carry] * (sub // LANES), axis=1)
        rank = jnp.sum(jnp.where(onehot, base + incl, 0.0), axis=0, keepdims=True) - 1.0
        carry = carry + total

        cls_ref[:, rows] = cls
        rank_ref[:, rows] = rank.astype(I32)
        w_rows = jnp.concatenate([w_lo, w_hi, jnp.zeros((LANES - 2, sub), F32)], axis=0)
        h2e_ref[rows, D:D + WCOLS] = jnp.transpose(w_rows)
    carry_ref[...] = carry
    cnt_ref[...] = carry


def _postmix(yg, ys, sa, sb, x, g1, wug, wus, wo, gain, sc, sh, wr, br, seq, tm):
    T, D = x.shape
    per_b = seq // tm
    nt = T // tm
    tok = lambda i: (i, 0)
    row = lambda i: (0, 0)
    bat = lambda i: (i // per_b, 0, 0)
    return pl.pallas_call(
        functools.partial(_postmix_kernel, tm=tm),
        grid=(nt,),
        in_specs=[
            pl.BlockSpec((tm, yg.shape[1]), tok),
            pl.BlockSpec((tm, ys.shape[1]), tok),
            pl.BlockSpec((tm, D), tok),
            pl.BlockSpec((tm, D), tok),
            pl.BlockSpec((tm, D), tok),
            pl.BlockSpec((None, 1, D), bat),
            pl.BlockSpec(wug.shape, row),
            pl.BlockSpec(wus.shape, row),
            pl.BlockSpec(wo.shape, row),
            pl.BlockSpec((1, D), row),
            pl.BlockSpec((None, 1, D), bat),
            pl.BlockSpec((None, 1, D), bat),
            pl.BlockSpec((D, LANES), row),
            pl.BlockSpec((N_EXPERTS, 1), row),
        ],
        out_specs=[
            pl.BlockSpec((tm, D), tok),
            pl.BlockSpec((tm, D + WCOLS), tok),
            pl.BlockSpec((None, 1, tm), lambda i: (i, 0, 0)),
            pl.BlockSpec((None, 1, tm), lambda i: (i, 0, 0)),
            pl.BlockSpec((CLASS_ROWS, LANES), row),
        ],
        out_shape=[
            jax.ShapeDtypeStruct((T, D), F32),
            jax.ShapeDtypeStruct((T, D + WCOLS), F32),
            jax.ShapeDtypeStruct((nt, 1, tm), I32),
            jax.ShapeDtypeStruct((nt, 1, tm), I32),
            jax.ShapeDtypeStruct((CLASS_ROWS, LANES), F32),
        ],
        scratch_shapes=[pltpu.VMEM((CLASS_ROWS, LANES), F32)],
        compiler_params=_params(("arbitrary",), VMEM_LIMIT),
        name="postmix",
    )(yg, ys, sa, sb, x, g1, wug, wus, wo, gain, sc, sh, wr, br)


def _plan_kernel(start_ref, cls_ref, rank_ref, dest_ref):
    cls = cls_ref[...]
    base = jnp.zeros(cls.shape, I32)
    for c in range(N_CLASSES):
        base = jnp.where(cls == c, start_ref[c], base)
    dest_ref[...] = base + rank_ref[...]


def _plan(start, cls, rank):
    nt, _, tm = cls.shape
    g = min(PLAN_TILES, nt)
    blk = pl.BlockSpec((g, 1, tm), lambda i, s: (i, 0, 0))
    return pl.pallas_call(
        _plan_kernel,
        grid_spec=pltpu.PrefetchScalarGridSpec(
            num_scalar_prefetch=1,
            grid=(nt // g,),
            in_specs=[blk, blk],
            out_specs=blk,
        ),
        out_shape=jax.ShapeDtypeStruct((nt, 1, tm), I32),
        compiler_params=_params(("arbitrary",)),
        name="plan",
    )(start, cls, rank)


ROW_UNROLL = 8


DISPATCH_SLOTS = 3


def _dispatch_kernel(dest_ref, h_ref, xs_in_ref, xs_ref, buf, load_sem, scat_sem, *, tm, nt):
    del xs_in_ref
    i = pl.program_id(0)

    def load(t, slot):
        rows = pl.ds(pl.multiple_of(t * tm, tm), tm)
        return pltpu.make_async_copy(h_ref.at[rows], buf.at[slot], load_sem.at[slot])

    def row_copy(slot, r, d):
        return pltpu.make_async_copy(buf.at[slot, pl.ds(r, 1)], xs_ref.at[pl.ds(d, 1)],
                                     scat_sem.at[slot])

    @pl.when(i == 0)
    def _():
        for t in range(min(DISPATCH_SLOTS - 1, nt)):
            load(t, t).start()

    for slot in range(DISPATCH_SLOTS):
        @pl.when((i < nt) & (lax.rem(i, DISPATCH_SLOTS) == slot))
        def _():
            load(i, slot).wait()
            for r in range(tm):
                row_copy(slot, r, dest_ref[0, r]).start(priority=r % 2)

    @pl.when(i > 0)
    def _():
        slot = lax.rem(i + DISPATCH_SLOTS - 1, DISPATCH_SLOTS)

        def drain(j, carry):
            for u in range(ROW_UNROLL):
                row_copy(slot, 0, 0).wait()
            return carry

        lax.fori_loop(0, tm // ROW_UNROLL, drain, 0)

    @pl.when(i + DISPATCH_SLOTS - 1 < nt)
    def _():
        t = i + DISPATCH_SLOTS - 1
        load(t, lax.rem(t, DISPATCH_SLOTS)).start()


def _dispatch(dest, h2e, xs_init):
    nt, _, tm = dest.shape
    width = h2e.shape[1]
    return pl.pallas_call(
        functools.partial(_dispatch_kernel, tm=tm, nt=nt),
        grid=(nt + 1,),
        in_specs=[
            pl.BlockSpec((None, 1, tm), lambda i: (jnp.minimum(i, nt - 1), 0, 0),
                         memory_space=pltpu.SMEM),
            pl.BlockSpec(memory_space=pl.ANY),
            pl.BlockSpec(memory_space=pl.ANY),
        ],
        out_specs=pl.BlockSpec(memory_space=pl.ANY),
        out_shape=jax.ShapeDtypeStruct(xs_init.shape, F32),
        scratch_shapes=[pltpu.VMEM((DISPATCH_SLOTS, tm, width), F32),
                        pltpu.SemaphoreType.DMA((DISPATCH_SLOTS,)),
                        pltpu.SemaphoreType.DMA((DISPATCH_SLOTS,))],
        input_output_aliases={2: 0},
        compiler_params=_params(("arbitrary",)),
        name="dispatch",
    )(dest, h2e, xs_init)


def _moe_kernel(elo_ref, ehi_ref, nact_ref, xs_ref, wg0, wu0, wd0, wg1, wu1, wd1, o_ref):
    j = pl.program_id(0)
    D = o_ref.shape[1]

    @pl.when(j < nact_ref[0])
    def _():
        x = xs_ref[:, 0:D].astype(BF16)
        w_lo = xs_ref[:, D:D + 1]
        w_hi = xs_ref[:, D + 1:D + 2]

        gates = [_dot(x, wg[...]) for wg in (wg0, wg1)]
        ups = [_dot(x, wu[...]) for wu in (wu0, wu1)]
        acts = [((g * jax.nn.sigmoid(g)) * u).astype(BF16) for g, u in zip(gates, ups)]
        y_lo, y_hi = [_dot(a, wd[...]) for a, wd in zip(acts, (wd0, wd1))]
        o_ref[...] = w_lo * y_lo + w_hi * y_hi

    @pl.when(j >= nact_ref[0])
    def _():
        o_ref[...] = jnp.zeros_like(o_ref)


def _moe(elo, ehi, nact, xs, w_gate, w_up, w_down, layer):
    ns, width = xs.shape
    D = width - WCOLS
    nb = ns // MOE_ROWS
    lo = lambda j, a, b, n: (layer, a[j], 0, 0)
    hi = lambda j, a, b, n: (layer, b[j], 0, 0)
    gu = (None, None, D, D_FF)
    dn = (None, None, D_FF, D)
    return pl.pallas_call(
        _moe_kernel,
        grid_spec=pltpu.PrefetchScalarGridSpec(
            num_scalar_prefetch=3,
            grid=(nb,),
            in_specs=[
                pl.BlockSpec((MOE_ROWS, width), lambda j, a, b, n: (jnp.minimum(j, n[0] - 1), 0)),
                pl.BlockSpec(gu, lo), pl.BlockSpec(gu, lo), pl.BlockSpec(dn, lo),
                pl.BlockSpec(gu, hi), pl.BlockSpec(gu, hi), pl.BlockSpec(dn, hi),
            ],
            out_specs=pl.BlockSpec((MOE_ROWS, D), lambda j, a, b, n: (j, 0)),
        ),
        out_shape=jax.ShapeDtypeStruct((ns, D), F32),
        compiler_params=_params(("arbitrary",), VMEM_LIMIT),
        name="moe",
    )(elo, ehi, nact, xs, w_gate, w_up, w_down, w_gate, w_up, w_down)


def _combine_kernel(dest_ref, x1_ref, g2_ref, ys_ref, o_ref, buf, sem, *, tm, nt):
    i = pl.program_id(0)

    def row_copy(slot, r, d):
        return pltpu.make_async_copy(ys_ref.at[pl.ds(d, 1)], buf.at[slot, pl.ds(r, 1)],
                                     sem.at[slot])

    for slot in range(2):
        @pl.when((i < nt) & (lax.rem(i, 2) == slot))
        def _():
            for r in range(tm):
                row_copy(slot, r, dest_ref[0, r]).start(priority=r % 2)

    @pl.when(i > 0)
    def _():
        slot = lax.rem(i + 1, 2)

        def drain(j, carry):
            for u in range(ROW_UNROLL):
                row_copy(slot, 0, 0).wait()
            return carry

        lax.fori_loop(0, tm // ROW_UNROLL, drain, 0)
        o_ref[...] = x1_ref[...] + g2_ref[...] * buf[slot]


def _combine(dest, x1, g2, ys, seq):
    nt, _, tm = dest.shape
    T, D = x1.shape
    per_b = seq // tm
    done = lambda i: jnp.maximum(i - 1, 0)
    return pl.pallas_call(
        functools.partial(_combine_kernel, tm=tm, nt=nt),
        grid=(nt + 1,),
        in_specs=[
            pl.BlockSpec((None, 1, tm), lambda i: (jnp.minimum(i, nt - 1), 0, 0),
                         memory_space=pltpu.SMEM),
            pl.BlockSpec((tm, D), lambda i: (done(i), 0)),
            pl.BlockSpec((None, 1, D), lambda i: (done(i) // per_b, 0, 0)),
            pl.BlockSpec(memory_space=pl.ANY),
        ],
        out_specs=pl.BlockSpec((tm, D), lambda i: (done(i), 0)),
        out_shape=jax.ShapeDtypeStruct((T, D), F32),
        scratch_shapes=[pltpu.VMEM((2, tm, D), F32), pltpu.SemaphoreType.DMA((2,))],
        compiler_params=_params(("arbitrary",)),
        name="combine",
    )(dest, x1, g2, ys)


def _block_plan(counts, nb):
    cnt = counts[:N_CLASSES, 0].astype(I32)
    nblk = (cnt + MOE_ROWS - 1) // MOE_ROWS
    cum = jnp.cumsum(nblk)
    start = ((cum - nblk) * MOE_ROWS).astype(I32)
    nact = cum[-1:].astype(I32)
    blk = jnp.arange(nb, dtype=I32)
    cls = jnp.sum(jnp.minimum(blk, nact[0] - 1)[:, None] >= cum[None, :], axis=1).astype(I32)
    cls = jnp.minimum(cls, N_CLASSES - 1)
    lo_tab = jnp.array([p[0] for p in PAIRS], I32)
    hi_tab = jnp.array([p[1] for p in PAIRS], I32)
    grp = cls // len(PAIRS)
    elo = grp * EXPERTS_PER_GROUP + lo_tab[cls % len(PAIRS)]
    ehi = grp * EXPERTS_PER_GROUP + hi_tab[cls % len(PAIRS)]
    start = jnp.concatenate([start, jnp.zeros((CLASS_ROWS - N_CLASSES,), I32)])
    return start, elo, ehi, nact


def kernel(x, c, positions, w_ada, b_ada, norm_mix, w_in, w_alpha2, b_alpha, gla_norm, q_norm,
           k_norm, sinks, w_up_gla, w_up_swa, w_out, norm_ffn, w_router, b_router, w_gate, w_up,
           w_down):
    B, S, D = x.shape
    L = w_ada.shape[0]
    T = B * S
    tm = min(512, S)

    mod = _modulation(c, w_ada, b_ada).reshape(L, B, 6, 1, D)
    cos_t, sg_t = _rope_tables(positions)

    sizes = (256, 256, 512, 16, 512, 512, 128, 128, 1024, 1024)
    offs = [0]
    for s in sizes:
        offs.append(offs[-1] + s)
    order = (0, 1, 2, 4, 5, 6, 7, 8, 9, 3)
    w_in_p = jnp.concatenate(
        [w_in[:, :, offs[i]:offs[i + 1]] for i in order]
        + [jnp.zeros((L, D, LANES - GLA_RANK), F32)], axis=-1).astype(BF16)
    wa2_p = jnp.concatenate(
        [w_alpha2, jnp.zeros((L, LANES - GLA_RANK, w_alpha2.shape[-1]), F32)], axis=1).astype(BF16)
    wr = jnp.concatenate(
        [w_router.astype(BF16), jnp.zeros((D, LANES - N_EXPERTS), BF16)], axis=1)
    br = b_router.reshape(N_EXPERTS, 1)
    wug = w_up_gla.astype(BF16)
    wus = w_up_swa.astype(BF16)
    wo = w_out.astype(BF16)
    wg = w_gate.astype(BF16)
    wu = w_up.astype(BF16)
    wd = w_down.astype(BF16)

    nb = T // MOE_ROWS + N_CLASSES
    xs = jnp.zeros((nb * MOE_ROWS, D + WCOLS), F32)
    xt = x.reshape(T, D)
    pending = None
    for l in range(L):
        sh1, sc1, g1, sh2, sc2, g2 = [mod[l, :, i] for i in range(6)]
        premix_args = (norm_mix[l].reshape(1, D), sc1, sh1, w_in_p[l], wa2_p[l],
                       b_alpha[l].reshape(1, -1))
        if pending is None:
            gq, gk, gv, la, rs, sq, skv, sa, sb = _premix(xt, *premix_args, S, tm)
        else:
            gq, gk, gv, la, rs, sq, skv, sa, sb, xt = _premix_combine(*pending, *premix_args, S)
        qg = jnp.tile(q_norm[l], LANES // SWA_HD).reshape(1, LANES)
        kg = jnp.tile(k_norm[l], LANES // SWA_HD).reshape(1, LANES)
        sinks_b = jnp.broadcast_to(sinks[l][:, None], (SWA_HEADS, LANES))
        y_gla = _gla(gq, gk, gv, la, rs, gla_norm[l].reshape(1, -1), B, S, min(GLA_TILE, S))
        y_swa = _swa(sq, skv, cos_t, sg_t, qg, kg, sinks_b, B, S, tm)
        x1, h2e, cls, rank, counts = _postmix(
            y_gla, y_swa, sa, sb, xt, g1, wug[l], wus[l], wo[l],
            norm_ffn[l].reshape(1, D), sc2, sh2, wr, br, S, tm)
        start, elo, ehi, nact = _block_plan(counts, nb)
        dest = _plan(start, cls, rank)
        xs = _dispatch(dest, h2e, xs)
        ys = _moe(elo, ehi, nact, xs, wg, wu, wd, l)
        pending = (dest, x1, g2, ys)
    xt = _combine(*pending, S)
    return xt.reshape(B, S, D)
```

```python
import functools

import jax
import jax.numpy as jnp
from jax import lax
from jax.experimental import pallas as pl
from jax.experimental.pallas import tpu as pltpu

F32 = jnp.float32
BF16 = jnp.bfloat16
I32 = jnp.int32
HIGHEST = lax.Precision.HIGHEST

GLA_HEADS = 4
GLA_DK = 64
GLA_DV = 128
GLA_RANK = 16
GLA_TAU = 16.0
GLA_CHUNK = 64
SWA_HEADS = 8
SWA_KV_HEADS = 2
SWA_HD = 64
WINDOW = 128
ROPE_DIMS = SWA_HD // 4
ROPE_THETA = 500000.0
N_EXPERTS = 16
N_GROUPS = 4
EXPERTS_PER_GROUP = 4
D_FF = 512
EPS = 1e-6

LANES = 128
SUBLANES = 8
VMEM_LIMIT = 56 * 1024 * 1024

PAIRS = ((0, 1), (0, 2), (0, 3), (1, 2), (1, 3), (2, 3))
N_CLASSES = N_GROUPS * len(PAIRS)
CLASS_ROWS = 32
MOE_ROWS = 512
POSTMIX_SUBTILES = 2
PREMIX_SUBTILES = 1
GLA_TILE = 1024
GLA_WAVE = 4
PLAN_TILES = 16
WCOLS = LANES


def _params(sem, vmem=None):
    return pltpu.CompilerParams(dimension_semantics=sem, vmem_limit_bytes=vmem)


def _dot(a, b):
    return jnp.dot(a, b, preferred_element_type=F32)


def _iota(shape, axis):
    return lax.broadcasted_iota(I32, shape, axis)


def _mod_kernel(c_ref, w_ref, b_ref, o_ref):
    c = c_ref[...]
    cond = c * jax.nn.sigmoid(c)
    o_ref[...] = jnp.dot(cond, w_ref[...], precision=HIGHEST,
                         preferred_element_type=F32) + b_ref[...]


def _modulation(c, w_ada, b_ada):
    L, D, D6 = w_ada.shape
    B = c.shape[0]
    nj = D6 // D
    return pl.pallas_call(
        _mod_kernel,
        grid=(L, nj),
        in_specs=[
            pl.BlockSpec((B, D), lambda l, j: (0, 0)),
            pl.BlockSpec((None, D, D), lambda l, j: (l, 0, j)),
            pl.BlockSpec((None, 1, D), lambda l, j: (l, 0, j)),
        ],
        out_specs=pl.BlockSpec((None, B, D), lambda l, j: (l, 0, j)),
        out_shape=jax.ShapeDtypeStruct((L, B, D6), F32),
        compiler_params=_params(("arbitrary", "arbitrary")),
        name="modulation",
    )(c, w_ada, b_ada.reshape(L, 1, D6))


ROPE_PACK = LANES // ROPE_DIMS


def _rope_kernel(pos_ref, inv_ref, c_ref, s_ref):
    tm = pos_ref.shape[0]
    rb = tm // ROPE_PACK
    half = ROPE_DIMS // 2
    lane = _iota((1, LANES), 1)
    pos = pos_ref[...].astype(F32)
    packed = jnp.zeros((rb, LANES), F32)
    for k in range(ROPE_PACK):
        in_block = (lane >> (ROPE_DIMS.bit_length() - 1)) == k
        packed = jnp.where(in_block, pos[k * rb:(k + 1) * rb, :], packed)
    ang = packed * inv_ref[...]
    cos = jnp.cos(ang)
    sin = jnp.sin(ang)
    sg = jnp.where((lane & (ROPE_DIMS - 1)) < half, -sin, sin)
    head0 = lane < ROPE_DIMS
    head1 = (lane >= SWA_HD) & (lane < SWA_HD + ROPE_DIMS)
    for k in range(ROPE_PACK):
        rows = slice(k * rb, (k + 1) * rb)
        to0 = (LANES - ROPE_DIMS * k) % LANES
        to1 = (SWA_HD - ROPE_DIMS * k) % LANES
        c_ref[rows, :] = jnp.where(head0, pltpu.roll(cos, to0, 1),
                                   jnp.where(head1, pltpu.roll(cos, to1, 1), 1.0))
        s_ref[rows, :] = jnp.where(head0, pltpu.roll(sg, to0, 1),
                                   jnp.where(head1, pltpu.roll(sg, to1, 1), 0.0))


def _rope_tables(positions):
    T = positions.size
    half = ROPE_DIMS // 2
    inv_freq = jnp.power(ROPE_THETA, -jnp.arange(half, dtype=F32) / half)
    lane = jnp.arange(LANES)
    inv_lane = inv_freq[(lane % ROPE_DIMS) % half].reshape(1, LANES)
    tm = min(1024, T)
    return pl.pallas_call(
        _rope_kernel,
        grid=(T // tm,),
        in_specs=[pl.BlockSpec((tm, 1), lambda i: (i, 0)),
                  pl.BlockSpec((1, LANES), lambda i: (0, 0))],
        out_specs=[pl.BlockSpec((tm, LANES), lambda i: (i, 0))] * 2,
        out_shape=[jax.ShapeDtypeStruct((T, LANES), F32)] * 2,
        compiler_params=_params(("arbitrary",)),
        name="rope_tables",
    )(positions.reshape(T, 1), inv_lane)


_GQ, _GK, _GV, _GR = 0, 256, 512, 1024
_SQ, _SK, _SV = 1536, 2048, 2176
_MA, _MB, _GA, _WIN = 2304, 3328, 4352, 4480


def _premix_groups(x_ref, gain_ref, sc_ref, sh_ref, w_ref, wa2_ref, ba_ref,
                   gq_ref, gk_ref, gv_ref, la_ref, gr_ref, sq_ref, skv_ref, ma_ref, mb_ref):
    sub = x_ref.shape[0] // PREMIX_SUBTILES
    groups = [slice(s * sub, (s + 1) * sub) for s in range(PREMIX_SUBTILES)]
    half = (_MB - _MA) // 2
    hbs = []
    for r in groups:
        x = x_ref[r, :]
        ms = jnp.mean(x * x, axis=-1, keepdims=True)
        h = x * lax.rsqrt(ms + EPS) * gain_ref[...]
        h = h * (1.0 + sc_ref[...]) + sh_ref[...]
        hbs.append(h.astype(BF16))

    def sec(lo, hi):
        return zip(groups, [_dot(hb, w_ref[:, lo:hi]) for hb in hbs])

    for r, z in sec(_GQ, _GK):
        gq_ref[r, :] = (z * (GLA_DK ** -0.5)).astype(BF16)
    for r, z in sec(_GK, _GV):
        gk_ref[r, :] = z.astype(BF16)
    for r, z in sec(_GV, _GR):
        gv_ref[r, :] = z.astype(BF16)
    for r, z in sec(_GR, _SQ):
        gr_ref[r, :] = (z * jax.nn.sigmoid(z)).astype(BF16)
    for r, z in sec(_SQ, _SK):
        sq_ref[r, :] = z.astype(BF16)
    for r, z in sec(_SK, _MA):
        skv_ref[r, :] = z.astype(BF16)
    for c in range(2):
        cols = slice(c * half, (c + 1) * half)
        for r, z in sec(_MA + c * half, _MA + (c + 1) * half):
            ma_ref[r, cols] = jax.nn.sigmoid(z).astype(BF16)
        for r, z in sec(_MB + c * half, _MB + (c + 1) * half):
            mb_ref[r, cols] = jax.nn.sigmoid(z).astype(BF16)
    for r, a_low in sec(_GA, _WIN):
        z = _dot(a_low.astype(BF16), wa2_ref[...]) + ba_ref[...]
        log_sig = jnp.minimum(z, 0.0) - jnp.log1p(jnp.exp(-jnp.abs(z)))
        la_ref[r, :] = log_sig * (1.0 / GLA_TAU)


def _premix_kernel(x_ref, *refs):
    _premix_groups(x_ref, *refs)


def _premix_combine_kernel(dnext_ref, dfirst_ref, x1_ref, g2_ref, ys_ref, *refs, nt):
    refs, x_ref, buf, sem = refs[:-3], refs[-3], refs[-2], refs[-1]
    i = pl.program_id(0)
    tm = x1_ref.shape[0]
    slot = lax.rem(i, 2)

    def row_copy(s_, r, d):
        return pltpu.make_async_copy(ys_ref.at[pl.ds(d, 1)], buf.at[s_, pl.ds(r, 1)],
                                     sem.at[s_])

    def issue(dest_ref, s_):
        for r in range(tm):
            row_copy(s_, r, dest_ref[0, r]).start(priority=r % 2)

    def drain(s_):
        def body(j, carry):
            for u in range(ROW_UNROLL):
                row_copy(s_, 0, 0).wait()
            return carry
        lax.fori_loop(0, tm // ROW_UNROLL, body, 0)

    @pl.when(i == 0)
    def _():
        issue(dfirst_ref, 0)

    drain(slot)
    for s_ in range(2):
        @pl.when(slot != s_)
        def _():
            issue(dnext_ref, s_)
    x_ref[...] = x1_ref[...] + g2_ref[...] * buf[slot]
    _premix_groups(x_ref, *refs)

    @pl.when(i == nt - 1)
    def _():
        drain(1 - slot)


_PREMIX_WIDTHS = (256, 256, 512, 256, 512, 512, 256, 1024, 1024)
_PREMIX_DTYPES = (BF16, BF16, BF16, F32, BF16, BF16, BF16, BF16, BF16)


def _premix_weight_specs(D):
    row = lambda i: (0, 0)
    return [pl.BlockSpec((D, _WIN), row), pl.BlockSpec((LANES, 256), row),
            pl.BlockSpec((1, 256), row)]


def _premix(x, gain, sc, sh, w_in_p, wa2_p, b_alpha, seq, tm):
    T, D = x.shape
    per_b = seq // tm
    tok = lambda i: (i, 0)
    bat = lambda i: (i // per_b, 0, 0)
    return pl.pallas_call(
        _premix_kernel,
        grid=(T // tm,),
        in_specs=[
            pl.BlockSpec((tm, D), tok),
            pl.BlockSpec((1, D), lambda i: (0, 0)),
            pl.BlockSpec((None, 1, D), bat),
            pl.BlockSpec((None, 1, D), bat),
        ] + _premix_weight_specs(D),
        out_specs=[pl.BlockSpec((tm, w), tok) for w in _PREMIX_WIDTHS],
        out_shape=[jax.ShapeDtypeStruct((T, w), dt)
                   for w, dt in zip(_PREMIX_WIDTHS, _PREMIX_DTYPES)],
        compiler_params=_params(("arbitrary",), VMEM_LIMIT),
        name="premix",
    )(x, gain, sc, sh, w_in_p, wa2_p, b_alpha)


def _premix_combine(dest, x1, g2, ys, gain, sc, sh, w_in_p, wa2_p, b_alpha, seq):
    nt, _, tm = dest.shape
    T, D = x1.shape
    per_b = seq // tm
    tok = lambda i: (i, 0)
    bat = lambda i: (i // per_b, 0, 0)
    smem = pltpu.SMEM
    widths = _PREMIX_WIDTHS + (D,)
    dtypes = _PREMIX_DTYPES + (F32,)
    return pl.pallas_call(
        functools.partial(_premix_combine_kernel, nt=nt),
        grid=(nt,),
        in_specs=[
            pl.BlockSpec((None, 1, tm), lambda i: (jnp.minimum(i + 1, nt - 1), 0, 0),
                         memory_space=smem),
            pl.BlockSpec((None, 1, tm), lambda i: (0, 0, 0), memory_space=smem),
            pl.BlockSpec((tm, D), tok),
            pl.BlockSpec((None, 1, D), bat),
            pl.BlockSpec(memory_space=pl.ANY),
            pl.BlockSpec((1, D), lambda i: (0, 0)),
            pl.BlockSpec((None, 1, D), bat),
            pl.BlockSpec((None, 1, D), bat),
        ] + _premix_weight_specs(D),
        out_specs=[pl.BlockSpec((tm, w), tok) for w in widths],
        out_shape=[jax.ShapeDtypeStruct((T, w), dt) for w, dt in zip(widths, dtypes)],
        scratch_shapes=[pltpu.VMEM((2, tm, D), F32), pltpu.SemaphoreType.DMA((2,))],
        compiler_params=_params(("arbitrary",), VMEM_LIMIT),
        name="premix_combine",
    )(dest, dest, x1, g2, ys, gain, sc, sh, w_in_p, wa2_p, b_alpha)


def _gla_kernel(q_ref, k_ref, v_ref, la_ref, rs_ref, gn_ref, o_ref, s_ref, *, nchunk):
    C = GLA_CHUNK
    H = GLA_HEADS
    HK = H * GLA_DK

    @pl.when(pl.program_id(1) == 0)
    def _():
        s_ref[...] = jnp.zeros_like(s_ref)

    tri = jnp.where(_iota((C, C), 1) <= _iota((C, C), 0), 1.0, 0.0).astype(BF16)
    causal = (_iota((C, HK), 1) & (C - 1)) <= _iota((C, HK), 0)
    lane_head = _iota((1, HK), 1) >> 6
    ones = jnp.ones((C, LANES), BF16)
    zero_blk = jnp.zeros((C, GLA_DV), BF16)
    gain = gn_ref[...]
    tn = (((0,), (0,)), ((), ()))
    nt = (((1,), (1,)), ((), ()))

    def block_diag(blocks):
        rows = [jnp.concatenate([zero_blk] * h + [blk] + [zero_blk] * (H - 1 - h), axis=1)
                for h, blk in enumerate(blocks)]
        return jnp.concatenate(rows, axis=0)

    lhs, v_bds, kvs, decays = [], [], [], []
    for w0 in range(0, nchunk, GLA_WAVE):
        wave = range(w0, min(w0 + GLA_WAVE, nchunk))
        bs, ds = [], []
        for c in wave:
            la = la_ref[c * C:(c + 1) * C, :]
            la_hi = la.astype(BF16)
            la_lo = (la - la_hi.astype(F32)).astype(BF16)
            bs.append(_dot(tri, la_hi) + _dot(tri, la_lo))
            ds.append(lax.dot_general(la_hi, ones, tn, preferred_element_type=F32)
                      + lax.dot_general(la_lo, ones, tn, preferred_element_type=F32))
        atts, q_decs = [], []
        for c, b, d_col in zip(wave, bs, ds):
            rows = slice(c * C, (c + 1) * C)
            decays.append(jnp.exp(d_col))
            q = q_ref[rows, :].astype(F32)
            k = k_ref[rows, :].astype(F32)
            v = v_ref[rows, :]
            q_dec = (q * jnp.exp(b)).astype(BF16)
            k_inv = k * jnp.exp(-b)
            k_end = (k_inv * jnp.exp(b[C - 1:C, :])).astype(BF16)
            k_inv = k_inv.astype(BF16)
            k_bd = jnp.concatenate(
                [jnp.where(lane_head == h, k_inv, jnp.zeros_like(k_inv)) for h in range(H)],
                axis=0)
            atts.append(lax.dot_general(q_dec, k_bd, nt, preferred_element_type=F32))
            q_decs.append(q_dec)
            v_bds.append(block_diag([v[:, h * GLA_DV:(h + 1) * GLA_DV] for h in range(H)]))
            kvs.append([lax.dot_general(k_end[:, p * LANES:(p + 1) * LANES],
                                        v[:, 2 * p * GLA_DV:(2 * p + 2) * GLA_DV], tn,
                                        preferred_element_type=F32) for p in range(H // 2)])
        for att, q_dec in zip(atts, q_decs):
            att = jnp.where(causal, att, 0.0).astype(BF16)
            lhs.append(jnp.concatenate([att, q_dec], axis=1))

    state = [s_ref[h] for h in range(H)]
    states = []
    for c in range(nchunk):
        states.append(state)
        state = [decays[c][h * GLA_DK:(h + 1) * GLA_DK, :] * state[h]
                 + kvs[c][h // 2][(h % 2) * GLA_DK:(h % 2 + 1) * GLA_DK,
                                  (h % 2) * GLA_DV:(h % 2 + 1) * GLA_DV]
                 for h in range(H)]
    for h in range(H):
        s_ref[h] = state[h]

    for w0 in range(0, nchunk, GLA_WAVE):
        wave = range(w0, min(w0 + GLA_WAVE, nchunk))
        os_ = []
        for c in wave:
            rhs = jnp.concatenate(
                [v_bds[c], block_diag([s.astype(BF16) for s in states[c]])], axis=0)
            os_.append(_dot(lhs[c], rhs))
        for c, o in zip(wave, os_):
            rows = slice(c * C, (c + 1) * C)
            outs = []
            for h in range(H):
                oh = o[:, h * GLA_DV:(h + 1) * GLA_DV]
                ms = jnp.mean(oh * oh, axis=-1, keepdims=True)
                outs.append(oh * lax.rsqrt(ms + EPS) * gain[:, h * GLA_DV:(h + 1) * GLA_DV])
            y = jnp.concatenate(outs, axis=1) * rs_ref[rows, :].astype(F32)
            o_ref[rows, :] = y.astype(BF16)


def _swa_kernel(q_ref, kc_ref, kp_ref, vc_ref, vp_ref, cc_ref, sc_ref, cp_ref, sp_ref,
                qg_ref, kg_ref, sink_ref, o_ref, *, tq):
    W = WINDOW
    nw = tq // W
    G = SWA_HEADS // SWA_KV_HEADS
    step = pl.program_id(1)

    lane = _iota((1, LANES), 1)
    lo_half = lane < SWA_HD
    first = (lane & (SWA_HD - 1)) < (ROPE_DIMS // 2)
    seg = ((_iota((LANES, LANES), 0) >> 6) == (_iota((LANES, LANES), 1) >> 6))
    seg_mean = jnp.where(seg, 1.0 / SWA_HD, 0.0).astype(BF16)

    def norm_rope(x, cos, sg, gain):
        ms = _dot((x * x).astype(BF16), seg_mean)
        y = x * lax.rsqrt(ms + EPS) * gain
        partner = jnp.where(first, pltpu.roll(y, LANES - ROPE_DIMS // 2, 1),
                            pltpu.roll(y, ROPE_DIMS // 2, 1))
        return y * cos + partner * sg

    cos_c = cc_ref[...]
    sg_c = sc_ref[...]
    cos_k = jnp.concatenate([cp_ref[...], cos_c], axis=0)
    sg_k = jnp.concatenate([sp_ref[...], sg_c], axis=0)
    kb = jnp.concatenate([kp_ref[...], kc_ref[...]], axis=0).astype(F32)
    kb = norm_rope(kb, cos_k, sg_k, kg_ref[...])
    kb_r = pltpu.roll(kb, SWA_HD, 1)
    vb = jnp.concatenate([vp_ref[...], vc_ref[...]], axis=0).astype(F32)
    vb_r = pltpu.roll(vb, SWA_HD, 1)
    k_dup = (jnp.where(lo_half, kb, kb_r).astype(BF16), jnp.where(lo_half, kb_r, kb).astype(BF16))
    v_dup = (jnp.where(lo_half, vb, vb_r).astype(BF16), jnp.where(lo_half, vb_r, vb).astype(BF16))

    q_cols = []
    for c in range(SWA_HEADS // 2):
        qc = q_ref[:, c * LANES:(c + 1) * LANES].astype(F32)
        q_cols.append(norm_rope(qc, cos_c, sg_c, qg_ref[...]) * (SWA_HD ** -0.5))

    cur_side = _iota((W, W), 0) <= _iota((W, W), 1)
    cur_side4 = jnp.concatenate([cur_side] * G, axis=1)
    sinks = sink_ref[...]
    nt = (((1,), (1,)), ((), ()))
    tn = (((0,), (0,)), ((), ()))

    for w in range(nw):
        for kvh in range(SWA_KV_HEADS):
            parts = []
            for cc in range(G // 2):
                qw = q_cols[kvh * (G // 2) + cc][w * W:(w + 1) * W, :]
                parts.append(jnp.where(lo_half, qw, 0.0))
                parts.append(jnp.where(lo_half, 0.0, qw))
            q_stack = jnp.concatenate(parts, axis=0).astype(BF16)
            k_prev = k_dup[kvh][w * W:(w + 1) * W, :]
            k_cur = k_dup[kvh][(w + 1) * W:(w + 2) * W, :]
            v_prev = v_dup[kvh][w * W:(w + 1) * W, :]
            v_cur = v_dup[kvh][(w + 1) * W:(w + 2) * W, :]
            s_cur = lax.dot_general(k_cur, q_stack, nt, preferred_element_type=F32)
            s_prev = lax.dot_general(k_prev, q_stack, nt, preferred_element_type=F32)
            if w == 0:
                s_prev = jnp.where(step > 0, s_prev, -1e30)
            s = jnp.where(cur_side4, s_cur, s_prev)
            sink = jnp.concatenate(
                [sinks[kvh * G + g:kvh * G + g + 1, :] for g in range(G)], axis=1)
            m = jnp.maximum(jnp.max(s, axis=0, keepdims=True), sink)
            p = jnp.exp(s - m)
            den = jnp.sum(p, axis=0, keepdims=True) + jnp.exp(sink - m)
            p = (p * (1.0 / den)).astype(BF16)
            zero = jnp.zeros_like(p)
            o = (lax.dot_general(jnp.where(cur_side4, p, zero), v_cur, tn,
                                 preferred_element_type=F32)
                 + lax.dot_general(jnp.where(cur_side4, zero, p), v_prev, tn,
                                   preferred_element_type=F32))
            og = [o[g * W:(g + 1) * W, :] for g in range(G)]
            for cc in range(G // 2):
                col = jnp.where(lo_half, og[2 * cc], og[2 * cc + 1])
                c0 = (kvh * (G // 2) + cc) * LANES
                o_ref[w * W:(w + 1) * W, c0:c0 + LANES] = col.astype(BF16)


def _gla(gq, gk, gv, la, rs, gn, batch, seq, ts):
    T = gq.shape[0]
    per_b = seq // ts
    tok = lambda b, i: (b * per_b + i, 0)
    HK = GLA_HEADS * GLA_DK
    HV = GLA_HEADS * GLA_DV
    return pl.pallas_call(
        functools.partial(_gla_kernel, nchunk=ts // GLA_CHUNK),
        grid=(batch, per_b),
        in_specs=[
            pl.BlockSpec((ts, HK), tok),
            pl.BlockSpec((ts, HK), tok),
            pl.BlockSpec((ts, HV), tok),
            pl.BlockSpec((ts, HK), tok),
            pl.BlockSpec((ts, HV), tok),
            pl.BlockSpec((1, HV), lambda b, i: (0, 0)),
        ],
        out_specs=pl.BlockSpec((ts, HV), tok),
        out_shape=jax.ShapeDtypeStruct((T, HV), BF16),
        scratch_shapes=[pltpu.VMEM((GLA_HEADS, GLA_DK, GLA_DV), F32)],
        compiler_params=_params(("arbitrary", "arbitrary")),
        name="gla",
    )(gq, gk, gv, la, rs, gn)


def _swa(sq, skv, cos_t, sg_t, qg, kg, sinks_b, batch, seq, tq):
    T = sq.shape[0]
    per_b = seq // tq
    r = tq // WINDOW
    cur = lambda b, i: (b * per_b + i, 0)
    prev = lambda b, i: (jnp.maximum((b * per_b + i) * r - 1, 0), 0)
    cur_v = lambda b, i: (b * per_b + i, 1)
    prev_v = lambda b, i: (jnp.maximum((b * per_b + i) * r - 1, 0), 1)
    const = lambda b, i: (0, 0)
    QW = SWA_HEADS * SWA_HD
    return pl.pallas_call(
        functools.partial(_swa_kernel, tq=tq),
        grid=(batch, per_b),
        in_specs=[
            pl.BlockSpec((tq, QW), cur),
            pl.BlockSpec((tq, LANES), cur),
            pl.BlockSpec((WINDOW, LANES), prev),
            pl.BlockSpec((tq, LANES), cur_v),
            pl.BlockSpec((WINDOW, LANES), prev_v),
            pl.BlockSpec((tq, LANES), cur),
            pl.BlockSpec((tq, LANES), cur),
            pl.BlockSpec((WINDOW, LANES), prev),
            pl.BlockSpec((WINDOW, LANES), prev),
            pl.BlockSpec((1, LANES), const),
            pl.BlockSpec((1, LANES), const),
            pl.BlockSpec((SWA_HEADS, LANES), const),
        ],
        out_specs=pl.BlockSpec((tq, QW), cur),
        out_shape=jax.ShapeDtypeStruct((T, QW), BF16),
        compiler_params=_params(("arbitrary", "arbitrary")),
        name="swa",
    )(sq, skv, skv, skv, skv, cos_t, sg_t, cos_t, sg_t, qg, kg, sinks_b)


def _second_largest(a, b, c, d):
    hi1, lo1 = jnp.maximum(a, b), jnp.minimum(a, b)
    hi2, lo2 = jnp.maximum(c, d), jnp.minimum(c, d)
    return jnp.maximum(hi1, hi2), jnp.maximum(jnp.minimum(hi1, hi2), jnp.maximum(lo1, lo2))


def _argmax4(vals):
    best, idx = vals[0], jnp.zeros(vals[0].shape, I32)
    for k in range(1, 4):
        upd = vals[k] > best
        idx = jnp.where(upd, k, idx)
        best = jnp.where(upd, vals[k], best)
    return idx, best


def _pick4(idx, vals):
    return jnp.where(idx == 0, vals[0],
                     jnp.where(idx == 1, vals[1], jnp.where(idx == 2, vals[2], vals[3])))


def _postmix_kernel(yg_ref, ys_ref, sa_ref, sb_ref, x_ref, g1_ref, wug_ref, wus_ref, wo_ref,
                    gain_ref, sc_ref, sh_ref, wr_ref, br_ref,
                    x1_ref, h2e_ref, cls_ref, rank_ref, cnt_ref, carry_ref, *, tm):
    @pl.when(pl.program_id(0) == 0)
    def _():
        carry_ref[...] = jnp.zeros_like(carry_ref)

    D = x_ref.shape[1]
    sub = tm // POSTMIX_SUBTILES
    upper = jnp.where(_iota((sub, sub), 0) <= _iota((sub, sub), 1), 1.0, 0.0).astype(BF16)
    carry = carry_ref[...]
    scale = gain_ref[...] * (1.0 + sc_ref[...])
    groups = [slice(s * sub, (s + 1) * sub) for s in range(POSTMIX_SUBTILES)]
    us = [_dot(yg_ref[r, :], wug_ref[...]).astype(BF16) for r in groups]
    vs = [_dot(ys_ref[r, :], wus_ref[...]).astype(BF16) for r in groups]
    mixed = [_dot(sa_ref[r, :] * u + sb_ref[r, :] * v, wo_ref[...])
             for r, u, v in zip(groups, us, vs)]
    h2s = []
    for r, y in zip(groups, mixed):
        x1 = x_ref[r, :] + g1_ref[...] * y
        x1_ref[r, :] = x1
        ms = jnp.mean(x1 * x1, axis=-1, keepdims=True)
        h2 = x1 * lax.rsqrt(ms + EPS) * scale + sh_ref[...]
        h2e_ref[r, 0:D] = h2
        h2s.append(h2.astype(BF16))
    all_logits = [_dot(h, wr_ref[...]) for h in h2s]
    for rows, logits in zip(groups, all_logits):
        aff = jnp.transpose(jax.nn.sigmoid(logits))[0:N_EXPERTS, :]
        sel = aff + br_ref[...]
        aff_r = [aff[e:e + 1, :] for e in range(N_EXPERTS)]
        sel_r = [sel[e:e + 1, :] for e in range(N_EXPERTS)]

        scores = []
        for g in range(N_GROUPS):
            m1, m2 = _second_largest(*sel_r[4 * g:4 * g + 4])
            scores.append(m1 + m2)
        grp, _ = _argmax4(scores)
        sel_g = [_pick4(grp, [sel_r[4 * g + k] for g in range(N_GROUPS)]) for k in range(4)]
        aff_g = [_pick4(grp, [aff_r[4 * g + k] for g in range(N_GROUPS)]) for k in range(4)]
        l1, _ = _argmax4(sel_g)
        masked = [jnp.where(l1 == k, -jnp.inf, sel_g[k]) for k in range(4)]
        l2, _ = _argmax4(masked)
        a1 = _pick4(l1, aff_g)
        a2 = _pick4(l2, aff_g)
        den = a1 + a2
        w1 = a1 / den
        w2 = a2 / den
        lo_e = jnp.minimum(l1, l2)
        hi_e = jnp.maximum(l1, l2)
        pair = jnp.where(lo_e == 0, hi_e - 1, jnp.where(lo_e == 1, hi_e + 1, 5))
        cls = grp * len(PAIRS) + pair
        w_lo = jnp.where(l1 < l2, w1, w2)
        w_hi = jnp.where(l1 < l2, w2, w1)

        onehot = _iota((CLASS_ROWS, sub), 0) == cls
        oh = jnp.where(onehot, 1.0, 0.0).astype(BF16)
        incl = _dot(oh, upper)
        total = _dot(oh, jnp.ones((sub, LANES), BF16))
        base = jnp.concatenate([carry] * (sub // LANES), axis=1)
        rank = jnp.sum(jnp.where(onehot, base + incl, 0.0), axis=0, keepdims=True) - 1.0
        carry = carry + total

        cls_ref[:, rows] = cls
        rank_ref[:, rows] = rank.astype(I32)
        w_rows = jnp.concatenate([w_lo, w_hi, jnp.zeros((LANES - 2, sub), F32)], axis=0)
        h2e_ref[rows, D:D + WCOLS] = jnp.transpose(w_rows)
    carry_ref[...] = carry
    cnt_ref[...] = carry


def _postmix(yg, ys, sa, sb, x, g1, wug, wus, wo, gain, sc, sh, wr, br, seq, tm):
    T, D = x.shape
    per_b = seq // tm
    nt = T // tm
    tok = lambda i: (i, 0)
    row = lambda i: (0, 0)
    bat = lambda i: (i // per_b, 0, 0)
    return pl.pallas_call(
        functools.partial(_postmix_kernel, tm=tm),
        grid=(nt,),
        in_specs=[
            pl.BlockSpec((tm, yg.shape[1]), tok),
            pl.BlockSpec((tm, ys.shape[1]), tok),
            pl.BlockSpec((tm, D), tok),
            pl.BlockSpec((tm, D), tok),
            pl.BlockSpec((tm, D), tok),
            pl.BlockSpec((None, 1, D), bat),
            pl.BlockSpec(wug.shape, row),
            pl.BlockSpec(wus.shape, row),
            pl.BlockSpec(wo.shape, row),
            pl.BlockSpec((1, D), row),
            pl.BlockSpec((None, 1, D), bat),
            pl.BlockSpec((None, 1, D), bat),
            pl.BlockSpec((D, LANES), row),
            pl.BlockSpec((N_EXPERTS, 1), row),
        ],
        out_specs=[
            pl.BlockSpec((tm, D), tok),
            pl.BlockSpec((tm, D + WCOLS), tok),
            pl.BlockSpec((None, 1, tm), lambda i: (i, 0, 0)),
            pl.BlockSpec((None, 1, tm), lambda i: (i, 0, 0)),
            pl.BlockSpec((CLASS_ROWS, LANES), row),
        ],
        out_shape=[
            jax.ShapeDtypeStruct((T, D), F32),
            jax.ShapeDtypeStruct((T, D + WCOLS), F32),
            jax.ShapeDtypeStruct((nt, 1, tm), I32),
            jax.ShapeDtypeStruct((nt, 1, tm), I32),
            jax.ShapeDtypeStruct((CLASS_ROWS, LANES), F32),
        ],
        scratch_shapes=[pltpu.VMEM((CLASS_ROWS, LANES), F32)],
        compiler_params=_params(("arbitrary",), VMEM_LIMIT),
        name="postmix",
    )(yg, ys, sa, sb, x, g1, wug, wus, wo, gain, sc, sh, wr, br)


def _plan_kernel(start_ref, cls_ref, rank_ref, dest_ref):
    cls = cls_ref[...]
    base = jnp.zeros(cls.shape, I32)
    for c in range(N_CLASSES):
        base = jnp.where(cls == c, start_ref[c], base)
    dest_ref[...] = base + rank_ref[...]


def _plan(start, cls, rank):
    nt, _, tm = cls.shape
    g = min(PLAN_TILES, nt)
    blk = pl.BlockSpec((g, 1, tm), lambda i, s: (i, 0, 0))
    return pl.pallas_call(
        _plan_kernel,
        grid_spec=pltpu.PrefetchScalarGridSpec(
            num_scalar_prefetch=1,
            grid=(nt // g,),
            in_specs=[blk, blk],
            out_specs=blk,
        ),
        out_shape=jax.ShapeDtypeStruct((nt, 1, tm), I32),
        compiler_params=_params(("arbitrary",)),
        name="plan",
    )(start, cls, rank)


ROW_UNROLL = 8


DISPATCH_SLOTS = 3


def _dispatch_kernel(dest_ref, h_ref, xs_in_ref, xs_ref, buf, load_sem, scat_sem, *, tm, nt):
    del xs_in_ref
    i = pl.program_id(0)

    def load(t, slot):
        rows = pl.ds(pl.multiple_of(t * tm, tm), tm)
        return pltpu.make_async_copy(h_ref.at[rows], buf.at[slot], load_sem.at[slot])

    def row_copy(slot, r, d):
        return pltpu.make_async_copy(buf.at[slot, pl.ds(r, 1)], xs_ref.at[pl.ds(d, 1)],
                                     scat_sem.at[slot])

    @pl.when(i == 0)
    def _():
        for t in range(min(DISPATCH_SLOTS - 1, nt)):
            load(t, t).start()

    for slot in range(DISPATCH_SLOTS):
        @pl.when((i < nt) & (lax.rem(i, DISPATCH_SLOTS) == slot))
        def _():
            load(i, slot).wait()
            for r in range(tm):
                row_copy(slot, r, dest_ref[0, r]).start(priority=r % 2)

    @pl.when(i > 0)
    def _():
        slot = lax.rem(i + DISPATCH_SLOTS - 1, DISPATCH_SLOTS)

        def drain(j, carry):
            for u in range(ROW_UNROLL):
                row_copy(slot, 0, 0).wait()
            return carry

        lax.fori_loop(0, tm // ROW_UNROLL, drain, 0)

    @pl.when(i + DISPATCH_SLOTS - 1 < nt)
    def _():
        t = i + DISPATCH_SLOTS - 1
        load(t, lax.rem(t, DISPATCH_SLOTS)).start()


def _dispatch(dest, h2e, xs_init):
    nt, _, tm = dest.shape
    width = h2e.shape[1]
    return pl.pallas_call(
        functools.partial(_dispatch_kernel, tm=tm, nt=nt),
        grid=(nt + 1,),
        in_specs=[
            pl.BlockSpec((None, 1, tm), lambda i: (jnp.minimum(i, nt - 1), 0, 0),
                         memory_space=pltpu.SMEM),
            pl.BlockSpec(memory_space=pl.ANY),
            pl.BlockSpec(memory_space=pl.ANY),
        ],
        out_specs=pl.BlockSpec(memory_space=pl.ANY),
        out_shape=jax.ShapeDtypeStruct(xs_init.shape, F32),
        scratch_shapes=[pltpu.VMEM((DISPATCH_SLOTS, tm, width), F32),
                        pltpu.SemaphoreType.DMA((DISPATCH_SLOTS,)),
                        pltpu.SemaphoreType.DMA((DISPATCH_SLOTS,))],
        input_output_aliases={2: 0},
        compiler_params=_params(("arbitrary",)),
        name="dispatch",
    )(dest, h2e, xs_init)


def _moe_kernel(elo_ref, ehi_ref, nact_ref, xs_ref, wg0, wu0, wd0, wg1, wu1, wd1, o_ref):
    j = pl.program_id(0)
    D = o_ref.shape[1]

    @pl.when(j < nact_ref[0])
    def _():
        x = xs_ref[:, 0:D].astype(BF16)
        w_lo = xs_ref[:, D:D + 1]
        w_hi = xs_ref[:, D + 1:D + 2]

        gates = [_dot(x, wg[...]) for wg in (wg0, wg1)]
        ups = [_dot(x, wu[...]) for wu in (wu0, wu1)]
        acts = [((g * jax.nn.sigmoid(g)) * u).astype(BF16) for g, u in zip(gates, ups)]
        y_lo, y_hi = [_dot(a, wd[...]) for a, wd in zip(acts, (wd0, wd1))]
        o_ref[...] = w_lo * y_lo + w_hi * y_hi

    @pl.when(j >= nact_ref[0])
    def _():
        o_ref[...] = jnp.zeros_like(o_ref)


def _moe(elo, ehi, nact, xs, w_gate, w_up, w_down, layer):
    ns, width = xs.shape
    D = width - WCOLS
    nb = ns // MOE_ROWS
    lo = lambda j, a, b, n: (layer, a[j], 0, 0)
    hi = lambda j, a, b, n: (layer, b[j], 0, 0)
    gu = (None, None, D, D_FF)
    dn = (None, None, D_FF, D)
    return pl.pallas_call(
        _moe_kernel,
        grid_spec=pltpu.PrefetchScalarGridSpec(
            num_scalar_prefetch=3,
            grid=(nb,),
            in_specs=[
                pl.BlockSpec((MOE_ROWS, width), lambda j, a, b, n: (jnp.minimum(j, n[0] - 1), 0)),
                pl.BlockSpec(gu, lo), pl.BlockSpec(gu, lo), pl.BlockSpec(dn, lo),
                pl.BlockSpec(gu, hi), pl.BlockSpec(gu, hi), pl.BlockSpec(dn, hi),
            ],
            out_specs=pl.BlockSpec((MOE_ROWS, D), lambda j, a, b, n: (j, 0)),
        ),
        out_shape=jax.ShapeDtypeStruct((ns, D), F32),
        compiler_params=_params(("arbitrary",), VMEM_LIMIT),
        name="moe",
    )(elo, ehi, nact, xs, w_gate, w_up, w_down, w_gate, w_up, w_down)


def _combine_kernel(dest_ref, x1_ref, g2_ref, ys_ref, o_ref, buf, sem, *, tm, nt):
    i = pl.program_id(0)

    def row_copy(slot, r, d):
        return pltpu.make_async_copy(ys_ref.at[pl.ds(d, 1)], buf.at[slot, pl.ds(r, 1)],
                                     sem.at[slot])

    for slot in range(2):
        @pl.when((i < nt) & (lax.rem(i, 2) == slot))
        def _():
            for r in range(tm):
                row_copy(slot, r, dest_ref[0, r]).start(priority=r % 2)

    @pl.when(i > 0)
    def _():
        slot = lax.rem(i + 1, 2)

        def drain(j, carry):
            for u in range(ROW_UNROLL):
                row_copy(slot, 0, 0).wait()
            return carry

        lax.fori_loop(0, tm // ROW_UNROLL, drain, 0)
        o_ref[...] = x1_ref[...] + g2_ref[...] * buf[slot]


def _combine(dest, x1, g2, ys, seq):
    nt, _, tm = dest.shape
    T, D = x1.shape
    per_b = seq // tm
    done = lambda i: jnp.maximum(i - 1, 0)
    return pl.pallas_call(
        functools.partial(_combine_kernel, tm=tm, nt=nt),
        grid=(nt + 1,),
        in_specs=[
            pl.BlockSpec((None, 1, tm), lambda i: (jnp.minimum(i, nt - 1), 0, 0),
                         memory_space=pltpu.SMEM),
            pl.BlockSpec((tm, D), lambda i: (done(i), 0)),
            pl.BlockSpec((None, 1, D), lambda i: (done(i) // per_b, 0, 0)),
            pl.BlockSpec(memory_space=pl.ANY),
        ],
        out_specs=pl.BlockSpec((tm, D), lambda i: (done(i), 0)),
        out_shape=jax.ShapeDtypeStruct((T, D), F32),
        scratch_shapes=[pltpu.VMEM((2, tm, D), F32), pltpu.SemaphoreType.DMA((2,))],
        compiler_params=_params(("arbitrary",)),
        name="combine",
    )(dest, x1, g2, ys)


def _block_plan(counts, nb):
    cnt = counts[:N_CLASSES, 0].astype(I32)
    nblk = (cnt + MOE_ROWS - 1) // MOE_ROWS
    cum = jnp.cumsum(nblk)
    start = ((cum - nblk) * MOE_ROWS).astype(I32)
    nact = cum[-1:].astype(I32)
    blk = jnp.arange(nb, dtype=I32)
    cls = jnp.sum(jnp.minimum(blk, nact[0] - 1)[:, None] >= cum[None, :], axis=1).astype(I32)
    cls = jnp.minimum(cls, N_CLASSES - 1)
    lo_tab = jnp.array([p[0] for p in PAIRS], I32)
    hi_tab = jnp.array([p[1] for p in PAIRS], I32)
    grp = cls // len(PAIRS)
    elo = grp * EXPERTS_PER_GROUP + lo_tab[cls % len(PAIRS)]
    ehi = grp * EXPERTS_PER_GROUP + hi_tab[cls % len(PAIRS)]
    start = jnp.concatenate([start, jnp.zeros((CLASS_ROWS - N_CLASSES,), I32)])
    return start, elo, ehi, nact


def kernel(x, c, positions, w_ada, b_ada, norm_mix, w_in, w_alpha2, b_alpha, gla_norm, q_norm,
           k_norm, sinks, w_up_gla, w_up_swa, w_out, norm_ffn, w_router, b_router, w_gate, w_up,
           w_down):
    B, S, D = x.shape
    L = w_ada.shape[0]
    T = B * S
    tm = min(512, S)

    mod = _modulation(c, w_ada, b_ada).reshape(L, B, 6, 1, D)
    cos_t, sg_t = _rope_tables(positions)

    sizes = (256, 256, 512, 16, 512, 512, 128, 128, 1024, 1024)
    offs = [0]
    for s in sizes:
        offs.append(offs[-1] + s)
    order = (0, 1, 2, 4, 5, 6, 7, 8, 9, 3)
    w_in_p = jnp.concatenate(
        [w_in[:, :, offs[i]:offs[i + 1]] for i in order]
        + [jnp.zeros((L, D, LANES - GLA_RANK), F32)], axis=-1).astype(BF16)
    wa2_p = jnp.concatenate(
        [w_alpha2, jnp.zeros((L, LANES - GLA_RANK, w_alpha2.shape[-1]), F32)], axis=1).astype(BF16)
    wr = jnp.concatenate(
        [w_router.astype(BF16), jnp.zeros((D, LANES - N_EXPERTS), BF16)], axis=1)
    br = b_router.reshape(N_EXPERTS, 1)
    wug = w_up_gla.astype(BF16)
    wus = w_up_swa.astype(BF16)
    wo = w_out.astype(BF16)
    wg = w_gate.astype(BF16)
    wu = w_up.astype(BF16)
    wd = w_down.astype(BF16)

    nb = T // MOE_ROWS + N_CLASSES
    xs = jnp.zeros((nb * MOE_ROWS, D + WCOLS), F32)
    xt = x.reshape(T, D)
    pending = None
    for l in range(L):
        sh1, sc1, g1, sh2, sc2, g2 = [mod[l, :, i] for i in range(6)]
        premix_args = (norm_mix[l].reshape(1, D), sc1, sh1, w_in_p[l], wa2_p[l],
                       b_alpha[l].reshape(1, -1))
        if pending is None:
            gq, gk, gv, la, rs, sq, skv, sa, sb = _premix(xt, *premix_args, S, tm)
        else:
            gq, gk, gv, la, rs, sq, skv, sa, sb, xt = _premix_combine(*pending, *premix_args, S)
        qg = jnp.tile(q_norm[l], LANES // SWA_HD).reshape(1, LANES)
        kg = jnp.tile(k_norm[l], LANES // SWA_HD).reshape(1, LANES)
        sinks_b = jnp.broadcast_to(sinks[l][:, None], (SWA_HEADS, LANES))
        y_gla = _gla(gq, gk, gv, la, rs, gla_norm[l].reshape(1, -1), B, S, min(GLA_TILE, S))
        y_swa = _swa(sq, skv, cos_t, sg_t, qg, kg, sinks_b, B, S, tm)
        x1, h2e, cls, rank, counts = _postmix(
            y_gla, y_swa, sa, sb, xt, g1, wug[l], wus[l], wo[l],
            norm_ffn[l].reshape(1, D), sc2, sh2, wr, br, S, tm)
        start, elo, ehi, nact = _block_plan(counts, nb)
        dest = _plan(start, cls, rank)
        xs = _dispatch(dest, h2e, xs)
        ys = _moe(elo, ehi, nact, xs, wg, wu, wd, l)
        pending = (dest, x1, g2, ys)
    xt = _combine(*pending, S)
    return xt.reshape(B, S, D)
```

```python
import functools

import jax
import jax.numpy as jnp
from jax import lax
from jax.experimental import pallas as pl
from jax.experimental.pallas import tpu as pltpu

F32 = jnp.float32
BF16 = jnp.bfloat16
I32 = jnp.int32
HIGHEST = lax.Precision.HIGHEST

GLA_HEADS = 4
GLA_DK = 64
GLA_DV = 128
GLA_RANK = 16
GLA_TAU = 16.0
GLA_CHUNK = 64
SWA_HEADS = 8
SWA_KV_HEADS = 2
SWA_HD = 64
WINDOW = 128
ROPE_DIMS = SWA_HD // 4
ROPE_THETA = 500000.0
N_EXPERTS = 16
N_GROUPS = 4
EXPERTS_PER_GROUP = 4
D_FF = 512
EPS = 1e-6

LANES = 128
SUBLANES = 8
VMEM_LIMIT = 56 * 1024 * 1024

PAIRS = ((0, 1), (0, 2), (0, 3), (1, 2), (1, 3), (2, 3))
N_CLASSES = N_GROUPS * len(PAIRS)
CLASS_ROWS = 32
MOE_ROWS = 512
POSTMIX_SUBTILES = 2
PREMIX_SUBTILES = 1
GLA_TILE = 1024
GLA_WAVE = 4
PLAN_TILES = 16
WCOLS = LANES


def _params(sem, vmem=None):
    return pltpu.CompilerParams(dimension_semantics=sem, vmem_limit_bytes=vmem)


def _dot(a, b):
    return jnp.dot(a, b, preferred_element_type=F32)


def _iota(shape, axis):
    return lax.broadcasted_iota(I32, shape, axis)


def _mod_kernel(c_ref, w_ref, b_ref, o_ref):
    c = c_ref[...]
    cond = c * jax.nn.sigmoid(c)
    o_ref[...] = jnp.dot(cond, w_ref[...], precision=HIGHEST,
                         preferred_element_type=F32) + b_ref[...]


def _modulation(c, w_ada, b_ada):
    L, D, D6 = w_ada.shape
    B = c.shape[0]
    nj = D6 // D
    return pl.pallas_call(
        _mod_kernel,
        grid=(L, nj),
        in_specs=[
            pl.BlockSpec((B, D), lambda l, j: (0, 0)),
            pl.BlockSpec((None, D, D), lambda l, j: (l, 0, j)),
            pl.BlockSpec((None, 1, D), lambda l, j: (l, 0, j)),
        ],
        out_specs=pl.BlockSpec((None, B, D), lambda l, j: (l, 0, j)),
        out_shape=jax.ShapeDtypeStruct((L, B, D6), F32),
        compiler_params=_params(("arbitrary", "arbitrary")),
        name="modulation",
    )(c, w_ada, b_ada.reshape(L, 1, D6))


ROPE_PACK = LANES // ROPE_DIMS


def _rope_kernel(pos_ref, inv_ref, c_ref, s_ref):
    tm = pos_ref.shape[0]
    rb = tm // ROPE_PACK
    half = ROPE_DIMS // 2
    lane = _iota((1, LANES), 1)
    pos = pos_ref[...].astype(F32)
    packed = jnp.zeros((rb, LANES), F32)
    for k in range(ROPE_PACK):
        in_block = (lane >> (ROPE_DIMS.bit_length() - 1)) == k
        packed = jnp.where(in_block, pos[k * rb:(k + 1) * rb, :], packed)
    ang = packed * inv_ref[...]
    cos = jnp.cos(ang)
    sin = jnp.sin(ang)
    sg = jnp.where((lane & (ROPE_DIMS - 1)) < half, -sin, sin)
    head0 = lane < ROPE_DIMS
    head1 = (lane >= SWA_HD) & (lane < SWA_HD + ROPE_DIMS)
    for k in range(ROPE_PACK):
        rows = slice(k * rb, (k + 1) * rb)
        to0 = (LANES - ROPE_DIMS * k) % LANES
        to1 = (SWA_HD - ROPE_DIMS * k) % LANES
        c_ref[rows, :] = jnp.where(head0, pltpu.roll(cos, to0, 1),
                                   jnp.where(head1, pltpu.roll(cos, to1, 1), 1.0))
        s_ref[rows, :] = jnp.where(head0, pltpu.roll(sg, to0, 1),
                                   jnp.where(head1, pltpu.roll(sg, to1, 1), 0.0))


def _rope_tables(positions):
    T = positions.size
    half = ROPE_DIMS // 2
    inv_freq = jnp.power(ROPE_THETA, -jnp.arange(half, dtype=F32) / half)
    lane = jnp.arange(LANES)
    inv_lane = inv_freq[(lane % ROPE_DIMS) % half].reshape(1, LANES)
    tm = min(1024, T)
    return pl.pallas_call(
        _rope_kernel,
        grid=(T // tm,),
        in_specs=[pl.BlockSpec((tm, 1), lambda i: (i, 0)),
                  pl.BlockSpec((1, LANES), lambda i: (0, 0))],
        out_specs=[pl.BlockSpec((tm, LANES), lambda i: (i, 0))] * 2,
        out_shape=[jax.ShapeDtypeStruct((T, LANES), F32)] * 2,
        compiler_params=_params(("arbitrary",)),
        name="rope_tables",
    )(positions.reshape(T, 1), inv_lane)


_GQ, _GK, _GV, _GR = 0, 256, 512, 1024
_SQ, _SK, _SV = 1536, 2048, 2176
_MA, _MB, _GA, _WIN = 2304, 3328, 4352, 4480


def _premix_groups(x_ref, gain_ref, sc_ref, sh_ref, w_ref, wa2_ref, ba_ref,
                   gq_ref, gk_ref, gv_ref, la_ref, gr_ref, sq_ref, skv_ref, ma_ref, mb_ref):
    sub = x_ref.shape[0] // PREMIX_SUBTILES
    groups = [slice(s * sub, (s + 1) * sub) for s in range(PREMIX_SUBTILES)]
    half = (_MB - _MA) // 2
    hbs = []
    for r in groups:
        x = x_ref[r, :]
        ms = jnp.mean(x * x, axis=-1, keepdims=True)
        h = x * lax.rsqrt(ms + EPS) * gain_ref[...]
        h = h * (1.0 + sc_ref[...]) + sh_ref[...]
        hbs.append(h.astype(BF16))

    def sec(lo, hi):
        return zip(groups, [_dot(hb, w_ref[:, lo:hi]) for hb in hbs])

    for r, z in sec(_GQ, _GK):
        gq_ref[r, :] = (z * (GLA_DK ** -0.5)).astype(BF16)
    for r, z in sec(_GK, _GV):
        gk_ref[r, :] = z.astype(BF16)
    for r, z in sec(_GV, _GR):
        gv_ref[r, :] = z.astype(BF16)
    for r, z in sec(_GR, _SQ):
        gr_ref[r, :] = (z * jax.nn.sigmoid(z)).astype(BF16)
    for r, z in sec(_SQ, _SK):
        sq_ref[r, :] = z.astype(BF16)
    for r, z in sec(_SK, _MA):
        skv_ref[r, :] = z.astype(BF16)
    for c in range(2):
        cols = slice(c * half, (c + 1) * half)
        for r, z in sec(_MA + c * half, _MA + (c + 1) * half):
            ma_ref[r, cols] = jax.nn.sigmoid(z).astype(BF16)
        for r, z in sec(_MB + c * half, _MB + (c + 1) * half):
            mb_ref[r, cols] = jax.nn.sigmoid(z).astype(BF16)
    for r, a_low in sec(_GA, _WIN):
        z = _dot(a_low.astype(BF16), wa2_ref[...]) + ba_ref[...]
        log_sig = jnp.minimum(z, 0.0) - jnp.log1p(jnp.exp(-jnp.abs(z)))
        la_ref[r, :] = log_sig * (1.0 / GLA_TAU)


def _premix_kernel(x_ref, *refs):
    _premix_groups(x_ref, *refs)


def _premix_combine_kernel(dnext_ref, dfirst_ref, x1_ref, g2_ref, ys_ref, *refs, nt):
    refs, x_ref, buf, sem = refs[:-3], refs[-3], refs[-2], refs[-1]
    i = pl.program_id(0)
    tm = x1_ref.shape[0]
    slot = lax.rem(i, 2)

    def row_copy(s_, r, d):
        return pltpu.make_async_copy(ys_ref.at[pl.ds(d, 1)], buf.at[s_, pl.ds(r, 1)],
                                     sem.at[s_])

    def issue(dest_ref, s_):
        for r in range(tm):
            row_copy(s_, r, dest_ref[0, r]).start(priority=r % 2)

    def drain(s_):
        def body(j, carry):
            for u in range(ROW_UNROLL):
                row_copy(s_, 0, 0).wait()
            return carry
        lax.fori_loop(0, tm // ROW_UNROLL, body, 0)

    @pl.when(i == 0)
    def _():
        issue(dfirst_ref, 0)

    drain(slot)
    for s_ in range(2):
        @pl.when(slot != s_)
        def _():
            issue(dnext_ref, s_)
    x_ref[...] = x1_ref[...] + g2_ref[...] * buf[slot]
    _premix_groups(x_ref, *refs)

    @pl.when(i == nt - 1)
    def _():
        drain(1 - slot)


_PREMIX_WIDTHS = (256, 256, 512, 256, 512, 512, 256, 1024, 1024)
_PREMIX_DTYPES = (BF16, BF16, BF16, F32, BF16, BF16, BF16, BF16, BF16)


def _premix_weight_specs(D):
    row = lambda i: (0, 0)
    return [pl.BlockSpec((D, _WIN), row), pl.BlockSpec((LANES, 256), row),
            pl.BlockSpec((1, 256), row)]


def _premix(x, gain, sc, sh, w_in_p, wa2_p, b_alpha, seq, tm):
    T, D = x.shape
    per_b = seq // tm
    tok = lambda i: (i, 0)
    bat = lambda i: (i // per_b, 0, 0)
    return pl.pallas_call(
        _premix_kernel,
        grid=(T // tm,),
        in_specs=[
            pl.BlockSpec((tm, D), tok),
            pl.BlockSpec((1, D), lambda i: (0, 0)),
            pl.BlockSpec((None, 1, D), bat),
            pl.BlockSpec((None, 1, D), bat),
        ] + _premix_weight_specs(D),
        out_specs=[pl.BlockSpec((tm, w), tok) for w in _PREMIX_WIDTHS],
        out_shape=[jax.ShapeDtypeStruct((T, w), dt)
                   for w, dt in zip(_PREMIX_WIDTHS, _PREMIX_DTYPES)],
        compiler_params=_params(("arbitrary",), VMEM_LIMIT),
        name="premix",
    )(x, gain, sc, sh, w_in_p, wa2_p, b_alpha)


def _premix_combine(dest, x1, g2, ys, gain, sc, sh, w_in_p, wa2_p, b_alpha, seq):
    nt, _, tm = dest.shape
    T, D = x1.shape
    per_b = seq // tm
    tok = lambda i: (i, 0)
    bat = lambda i: (i // per_b, 0, 0)
    smem = pltpu.SMEM
    widths = _PREMIX_WIDTHS + (D,)
    dtypes = _PREMIX_DTYPES + (F32,)
    return pl.pallas_call(
        functools.partial(_premix_combine_kernel, nt=nt),
        grid=(nt,),
        in_specs=[
            pl.BlockSpec((None, 1, tm), lambda i: (jnp.minimum(i + 1, nt - 1), 0, 0),
                         memory_space=smem),
            pl.BlockSpec((None, 1, tm), lambda i: (0, 0, 0), memory_space=smem),
            pl.BlockSpec((tm, D), tok),
            pl.BlockSpec((None, 1, D), bat),
            pl.BlockSpec(memory_space=pl.ANY),
            pl.BlockSpec((1, D), lambda i: (0, 0)),
            pl.BlockSpec((None, 1, D), bat),
            pl.BlockSpec((None, 1, D), bat),
        ] + _premix_weight_specs(D),
        out_specs=[pl.BlockSpec((tm, w), tok) for w in widths],
        out_shape=[jax.ShapeDtypeStruct((T, w), dt) for w, dt in zip(widths, dtypes)],
        scratch_shapes=[pltpu.VMEM((2, tm, D), F32), pltpu.SemaphoreType.DMA((2,))],
        compiler_params=_params(("arbitrary",), VMEM_LIMIT),
        name="premix_combine",
    )(dest, dest, x1, g2, ys, gain, sc, sh, w_in_p, wa2_p, b_alpha)


def _gla_kernel(q_ref, k_ref, v_ref, la_ref, rs_ref, gn_ref, o_ref, s_ref, *, nchunk):
    C = GLA_CHUNK
    H = GLA_HEADS
    HK = H * GLA_DK

    @pl.when(pl.program_id(1) == 0)
    def _():
        s_ref[...] = jnp.zeros_like(s_ref)

    tri = jnp.where(_iota((C, C), 1) <= _iota((C, C), 0), 1.0, 0.0).astype(BF16)
    causal = (_iota((C, HK), 1) & (C - 1)) <= _iota((C, HK), 0)
    lane_head = _iota((1, HK), 1) >> 6
    ones = jnp.ones((C, LANES), BF16)
    zero_blk = jnp.zeros((C, GLA_DV), BF16)
    gain = gn_ref[...]
    tn = (((0,), (0,)), ((), ()))
    nt = (((1,), (1,)), ((), ()))

    def block_diag(blocks):
        rows = [jnp.concatenate([zero_blk] * h + [blk] + [zero_blk] * (H - 1 - h), axis=1)
                for h, blk in enumerate(blocks)]
        return jnp.concatenate(rows, axis=0)

    lhs, v_bds, kvs, decays = [], [], [], []
    for w0 in range(0, nchunk, GLA_WAVE):
        wave = range(w0, min(w0 + GLA_WAVE, nchunk))
        bs, ds = [], []
        for c in wave:
            la = la_ref[c * C:(c + 1) * C, :]
            la_hi = la.astype(BF16)
            la_lo = (la - la_hi.astype(F32)).astype(BF16)
            bs.append(_dot(tri, la_hi) + _dot(tri, la_lo))
            ds.append(lax.dot_general(la_hi, ones, tn, preferred_element_type=F32)
                      + lax.dot_general(la_lo, ones, tn, preferred_element_type=F32))
        atts, q_decs = [], []
        for c, b, d_col in zip(wave, bs, ds):
            rows = slice(c * C, (c + 1) * C)
            decays.append(jnp.exp(d_col))
            q = q_ref[rows, :].astype(F32)
            k = k_ref[rows, :].astype(F32)
            v = v_ref[rows, :]
            q_dec = (q * jnp.exp(b)).astype(BF16)
            k_inv = k * jnp.exp(-b)
            k_end = (k_inv * jnp.exp(b[C - 1:C, :])).astype(BF16)
            k_inv = k_inv.astype(BF16)
            k_bd = jnp.concatenate(
                [jnp.where(lane_head == h, k_inv, jnp.zeros_like(k_inv)) for h in range(H)],
                axis=0)
            atts.append(lax.dot_general(q_dec, k_bd, nt, preferred_element_type=F32))
            q_decs.append(q_dec)
            v_bds.append(block_diag([v[:, h * GLA_DV:(h + 1) * GLA_DV] for h in range(H)]))
            kvs.append([lax.dot_general(k_end[:, p * LANES:(p + 1) * LANES],
                                        v[:, 2 * p * GLA_DV:(2 * p + 2) * GLA_DV], tn,
                                        preferred_element_type=F32) for p in range(H // 2)])
        for att, q_dec in zip(atts, q_decs):
            att = jnp.where(causal, att, 0.0).astype(BF16)
            lhs.append(jnp.concatenate([att, q_dec], axis=1))

    state = [s_ref[h] for h in range(H)]
    states = []
    for c in range(nchunk):
        states.append(state)
        state = [decays[c][h * GLA_DK:(h + 1) * GLA_DK, :] * state[h]
                 + kvs[c][h // 2][(h % 2) * GLA_DK:(h % 2 + 1) * GLA_DK,
                                  (h % 2) * GLA_DV:(h % 2 + 1) * GLA_DV]
                 for h in range(H)]
    for h in range(H):
        s_ref[h] = state[h]

    for w0 in range(0, nchunk, GLA_WAVE):
        wave = range(w0, min(w0 + GLA_WAVE, nchunk))
        os_ = []
        for c in wave:
            rhs = jnp.concatenate(
                [v_bds[c], block_diag([s.astype(BF16) for s in states[c]])], axis=0)
            os_.append(_dot(lhs[c], rhs))
        for c, o in zip(wave, os_):
            rows = slice(c * C, (c + 1) * C)
            outs = []
            for h in range(H):
                oh = o[:, h * GLA_DV:(h + 1) * GLA_DV]
                ms = jnp.mean(oh * oh, axis=-1, keepdims=True)
                outs.append(oh * lax.rsqrt(ms + EPS) * gain[:, h * GLA_DV:(h + 1) * GLA_DV])
            y = jnp.concatenate(outs, axis=1) * rs_ref[rows, :].astype(F32)
            o_ref[rows, :] = y.astype(BF16)


def _swa_kernel(q_ref, kc_ref, kp_ref, vc_ref, vp_ref, cc_ref, sc_ref, cp_ref, sp_ref,
                qg_ref, kg_ref, sink_ref, o_ref, *, tq):
    W = WINDOW
    nw = tq // W
    G = SWA_HEADS // SWA_KV_HEADS
    step = pl.program_id(1)

    lane = _iota((1, LANES), 1)
    lo_half = lane < SWA_HD
    first = (lane & (SWA_HD - 1)) < (ROPE_DIMS // 2)
    seg = ((_iota((LANES, LANES), 0) >> 6) == (_iota((LANES, LANES), 1) >> 6))
    seg_mean = jnp.where(seg, 1.0 / SWA_HD, 0.0).astype(BF16)

    def norm_rope(x, cos, sg, gain):
        ms = _dot((x * x).astype(BF16), seg_mean)
        y = x * lax.rsqrt(ms + EPS) * gain
        partner = jnp.where(first, pltpu.roll(y, LANES - ROPE_DIMS // 2, 1),
                            pltpu.roll(y, ROPE_DIMS // 2, 1))
        return y * cos + partner * sg

    cos_c = cc_ref[...]
    sg_c = sc_ref[...]
    cos_k = jnp.concatenate([cp_ref[...], cos_c], axis=0)
    sg_k = jnp.concatenate([sp_ref[...], sg_c], axis=0)
    kb = jnp.concatenate([kp_ref[...], kc_ref[...]], axis=0).astype(F32)
    kb = norm_rope(kb, cos_k, sg_k, kg_ref[...])
    kb_r = pltpu.roll(kb, SWA_HD, 1)
    vb = jnp.concatenate([vp_ref[...], vc_ref[...]], axis=0).astype(F32)
    vb_r = pltpu.roll(vb, SWA_HD, 1)
    k_dup = (jnp.where(lo_half, kb, kb_r).astype(BF16), jnp.where(lo_half, kb_r, kb).astype(BF16))
    v_dup = (jnp.where(lo_half, vb, vb_r).astype(BF16), jnp.where(lo_half, vb_r, vb).astype(BF16))

    q_cols = []
    for c in range(SWA_HEADS // 2):
        qc = q_ref[:, c * LANES:(c + 1) * LANES].astype(F32)
        q_cols.append(norm_rope(qc, cos_c, sg_c, qg_ref[...]) * (SWA_HD ** -0.5))

    cur_side = _iota((W, W), 0) <= _iota((W, W), 1)
    cur_side4 = jnp.concatenate([cur_side] * G, axis=1)
    sinks = sink_ref[...]
    nt = (((1,), (1,)), ((), ()))
    tn = (((0,), (0,)), ((), ()))

    for w in range(nw):
        for kvh in range(SWA_KV_HEADS):
            parts = []
            for cc in range(G // 2):
                qw = q_cols[kvh * (G // 2) + cc][w * W:(w + 1) * W, :]
                parts.append(jnp.where(lo_half, qw, 0.0))
                parts.append(jnp.where(lo_half, 0.0, qw))
            q_stack = jnp.concatenate(parts, axis=0).astype(BF16)
            k_prev = k_dup[kvh][w * W:(w + 1) * W, :]
            k_cur = k_dup[kvh][(w + 1) * W:(w + 2) * W, :]
            v_prev = v_dup[kvh][w * W:(w + 1) * W, :]
            v_cur = v_dup[kvh][(w + 1) * W:(w + 2) * W, :]
            s_cur = lax.dot_general(k_cur, q_stack, nt, preferred_element_type=F32)
            s_prev = lax.dot_general(k_prev, q_stack, nt, preferred_element_type=F32)
            if w == 0:
                s_prev = jnp.where(step > 0, s_prev, -1e30)
            s = jnp.where(cur_side4, s_cur, s_prev)
            sink = jnp.concatenate(
                [sinks[kvh * G + g:kvh * G + g + 1, :] for g in range(G)], axis=1)
            m = jnp.maximum(jnp.max(s, axis=0, keepdims=True), sink)
            p = jnp.exp(s - m)
            den = jnp.sum(p, axis=0, keepdims=True) + jnp.exp(sink - m)
            p = (p * (1.0 / den)).astype(BF16)
            zero = jnp.zeros_like(p)
            o = (lax.dot_general(jnp.where(cur_side4, p, zero), v_cur, tn,
                                 preferred_element_type=F32)
                 + lax.dot_general(jnp.where(cur_side4, zero, p), v_prev, tn,
                                   preferred_element_type=F32))
            og = [o[g * W:(g + 1) * W, :] for g in range(G)]
            for cc in range(G // 2):
                col = jnp.where(lo_half, og[2 * cc], og[2 * cc + 1])
                c0 = (kvh * (G // 2) + cc) * LANES
                o_ref[w * W:(w + 1) * W, c0:c0 + LANES] = col.astype(BF16)


def _gla(gq, gk, gv, la, rs, gn, batch, seq, ts):
    T = gq.shape[0]
    per_b = seq // ts
    tok = lambda b, i: (b * per_b + i, 0)
    HK = GLA_HEADS * GLA_DK
    HV = GLA_HEADS * GLA_DV
    return pl.pallas_call(
        functools.partial(_gla_kernel, nchunk=ts // GLA_CHUNK),
        grid=(batch, per_b),
        in_specs=[
            pl.BlockSpec((ts, HK), tok),
            pl.BlockSpec((ts, HK), tok),
            pl.BlockSpec((ts, HV), tok),
            pl.BlockSpec((ts, HK), tok),
            pl.BlockSpec((ts, HV), tok),
            pl.BlockSpec((1, HV), lambda b, i: (0, 0)),
        ],
        out_specs=pl.BlockSpec((ts, HV), tok),
        out_shape=jax.ShapeDtypeStruct((T, HV), BF16),
        scratch_shapes=[pltpu.VMEM((GLA_HEADS, GLA_DK, GLA_DV), F32)],
        compiler_params=_params(("arbitrary", "arbitrary")),
        name="gla",
    )(gq, gk, gv, la, rs, gn)


def _swa(sq, skv, cos_t, sg_t, qg, kg, sinks_b, batch, seq, tq):
    T = sq.shape[0]
    per_b = seq // tq
    r = tq // WINDOW
    cur = lambda b, i: (b * per_b + i, 0)
    prev = lambda b, i: (jnp.maximum((b * per_b + i) * r - 1, 0), 0)
    cur_v = lambda b, i: (b * per_b + i, 1)
    prev_v = lambda b, i: (jnp.maximum((b * per_b + i) * r - 1, 0), 1)
    const = lambda b, i: (0, 0)
    QW = SWA_HEADS * SWA_HD
    return pl.pallas_call(
        functools.partial(_swa_kernel, tq=tq),
        grid=(batch, per_b),
        in_specs=[
            pl.BlockSpec((tq, QW), cur),
            pl.BlockSpec((tq, LANES), cur),
            pl.BlockSpec((WINDOW, LANES), prev),
            pl.BlockSpec((tq, LANES), cur_v),
            pl.BlockSpec((WINDOW, LANES), prev_v),
            pl.BlockSpec((tq, LANES), cur),
            pl.BlockSpec((tq, LANES), cur),
            pl.BlockSpec((WINDOW, LANES), prev),
            pl.BlockSpec((WINDOW, LANES), prev),
            pl.BlockSpec((1, LANES), const),
            pl.BlockSpec((1, LANES), const),
            pl.BlockSpec((SWA_HEADS, LANES), const),
        ],
        out_specs=pl.BlockSpec((tq, QW), cur),
        out_shape=jax.ShapeDtypeStruct((T, QW), BF16),
        compiler_params=_params(("arbitrary", "arbitrary")),
        name="swa",
    )(sq, skv, skv, skv, skv, cos_t, sg_t, cos_t, sg_t, qg, kg, sinks_b)


def _second_largest(a, b, c, d):
    hi1, lo1 = jnp.maximum(a, b), jnp.minimum(a, b)
    hi2, lo2 = jnp.maximum(c, d), jnp.minimum(c, d)
    return jnp.maximum(hi1, hi2), jnp.maximum(jnp.minimum(hi1, hi2), jnp.maximum(lo1, lo2))


def _argmax4(vals):
    best, idx = vals[0], jnp.zeros(vals[0].shape, I32)
    for k in range(1, 4):
        upd = vals[k] > best
        idx = jnp.where(upd, k, idx)
        best = jnp.where(upd, vals[k], best)
    return idx, best


def _pick4(idx, vals):
    return jnp.where(idx == 0, vals[0],
                     jnp.where(idx == 1, vals[1], jnp.where(idx == 2, vals[2], vals[3])))


def _postmix_kernel(yg_ref, ys_ref, sa_ref, sb_ref, x_ref, g1_ref, wug_ref, wus_ref, wo_ref,
                    gain_ref, sc_ref, sh_ref, wr_ref, br_ref,
                    x1_ref, h2e_ref, cls_ref, rank_ref, cnt_ref, carry_ref, *, tm):
    @pl.when(pl.program_id(0) == 0)
    def _():
        carry_ref[...] = jnp.zeros_like(carry_ref)

    D = x_ref.shape[1]
    sub = tm // POSTMIX_SUBTILES
    upper = jnp.where(_iota((sub, sub), 0) <= _iota((sub, sub), 1), 1.0, 0.0).astype(BF16)
    carry = carry_ref[...]
    scale = gain_ref[...] * (1.0 + sc_ref[...])
    groups = [slice(s * sub, (s + 1) * sub) for s in range(POSTMIX_SUBTILES)]
    us = [_dot(yg_ref[r, :], wug_ref[...]).astype(BF16) for r in groups]
    vs = [_dot(ys_ref[r, :], wus_ref[...]).astype(BF16) for r in groups]
    mixed = [_dot(sa_ref[r, :] * u + sb_ref[r, :] * v, wo_ref[...])
             for r, u, v in zip(groups, us, vs)]
    h2s = []
    for r, y in zip(groups, mixed):
        x1 = x_ref[r, :] + g1_ref[...] * y
        x1_ref[r, :] = x1
        ms = jnp.mean(x1 * x1, axis=-1, keepdims=True)
        h2 = x1 * lax.rsqrt(ms + EPS) * scale + sh_ref[...]
        h2e_ref[r, 0:D] = h2
        h2s.append(h2.astype(BF16))
    all_logits = [_dot(h, wr_ref[...]) for h in h2s]
    for rows, logits in zip(groups, all_logits):
        aff = jnp.transpose(jax.nn.sigmoid(logits))[0:N_EXPERTS, :]
        sel = aff + br_ref[...]
        aff_r = [aff[e:e + 1, :] for e in range(N_EXPERTS)]
        sel_r = [sel[e:e + 1, :] for e in range(N_EXPERTS)]

        scores = []
        for g in range(N_GROUPS):
            m1, m2 = _second_largest(*sel_r[4 * g:4 * g + 4])
            scores.append(m1 + m2)
        grp, _ = _argmax4(scores)
        sel_g = [_pick4(grp, [sel_r[4 * g + k] for g in range(N_GROUPS)]) for k in range(4)]
        aff_g = [_pick4(grp, [aff_r[4 * g + k] for g in range(N_GROUPS)]) for k in range(4)]
        l1, _ = _argmax4(sel_g)
        masked = [jnp.where(l1 == k, -jnp.inf, sel_g[k]) for k in range(4)]
        l2, _ = _argmax4(masked)
        a1 = _pick4(l1, aff_g)
        a2 = _pick4(l2, aff_g)
        den = a1 + a2
        w1 = a1 / den
        w2 = a2 / den
        lo_e = jnp.minimum(l1, l2)
        hi_e = jnp.maximum(l1, l2)
        pair = jnp.where(lo_e == 0, hi_e - 1, jnp.where(lo_e == 1, hi_e + 1, 5))
        cls = grp * len(PAIRS) + pair
        w_lo = jnp.where(l1 < l2, w1, w2)
        w_hi = jnp.where(l1 < l2, w2, w1)

        onehot = _iota((CLASS_ROWS, sub), 0) == cls
        oh = jnp.where(onehot, 1.0, 0.0).astype(BF16)
        incl = _dot(oh, upper)
        total = _dot(oh, jnp.ones((sub, LANES), BF16))
        base = jnp.concatenate([carry] * (sub // LANES), axis=1)
        rank = jnp.sum(jnp.where(onehot, base + incl, 0.0), axis=0, keepdims=True) - 1.0
        carry = carry + total

        cls_ref[:, rows] = cls
        rank_ref[:, rows] = rank.astype(I32)
        w_rows = jnp.concatenate([w_lo, w_hi, jnp.zeros((LANES - 2, sub), F32)], axis=0)
        h2e_ref[rows, D:D + WCOLS] = jnp.transpose(w_rows)
    carry_ref[...] = carry
    cnt_ref[...] = carry


def _postmix(yg, ys, sa, sb, x, g1, wug, wus, wo, gain, sc, sh, wr, br, seq, tm):
    T, D = x.shape
    per_b = seq // tm
    nt = T // tm
    tok = lambda i: (i, 0)
    row = lambda i: (0, 0)
    bat = lambda i: (i // per_b, 0, 0)
    return pl.pallas_call(
        functools.partial(_postmix_kernel, tm=tm),
        grid=(nt,),
        in_specs=[
            pl.BlockSpec((tm, yg.shape[1]), tok),
            pl.BlockSpec((tm, ys.shape[1]), tok),
            pl.BlockSpec((tm, D), tok),
            pl.BlockSpec((tm, D), tok),
            pl.BlockSpec((tm, D), tok),
            pl.BlockSpec((None, 1, D), bat),
            pl.BlockSpec(wug.shape, row),
            pl.BlockSpec(wus.shape, row),
            pl.BlockSpec(wo.shape, row),
            pl.BlockSpec((1, D), row),
            pl.BlockSpec((None, 1, D), bat),
            pl.BlockSpec((None, 1, D), bat),
            pl.BlockSpec((D, LANES), row),
            pl.BlockSpec((N_EXPERTS, 1), row),
        ],
        out_specs=[
            pl.BlockSpec((tm, D), tok),
            pl.BlockSpec((tm, D + WCOLS), tok),
            pl.BlockSpec((None, 1, tm), lambda i: (i, 0, 0)),
            pl.BlockSpec((None, 1, tm), lambda i: (i, 0, 0)),
            pl.BlockSpec((CLASS_ROWS, LANES), row),
        ],
        out_shape=[
            jax.ShapeDtypeStruct((T, D), F32),
            jax.ShapeDtypeStruct((T, D + WCOLS), F32),
            jax.ShapeDtypeStruct((nt, 1, tm), I32),
            jax.ShapeDtypeStruct((nt, 1, tm), I32),
            jax.ShapeDtypeStruct((CLASS_ROWS, LANES), F32),
        ],
        scratch_shapes=[pltpu.VMEM((CLASS_ROWS, LANES), F32)],
        compiler_params=_params(("arbitrary",), VMEM_LIMIT),
        name="postmix",
    )(yg, ys, sa, sb, x, g1, wug, wus, wo, gain, sc, sh, wr, br)


def _plan_kernel(start_ref, cls_ref, rank_ref, dest_ref):
    cls = cls_ref[...]
    base = jnp.zeros(cls.shape, I32)
    for c in range(N_CLASSES):
        base = jnp.where(cls == c, start_ref[c], base)
    dest_ref[...] = base + rank_ref[...]


def _plan(start, cls, rank):
    nt, _, tm = cls.shape
    g = min(PLAN_TILES, nt)
    blk = pl.BlockSpec((g, 1, tm), lambda i, s: (i, 0, 0))
    return pl.pallas_call(
        _plan_kernel,
        grid_spec=pltpu.PrefetchScalarGridSpec(
            num_scalar_prefetch=1,
            grid=(nt // g,),
            in_specs=[blk, blk],
            out_specs=blk,
        ),
        out_shape=jax.ShapeDtypeStruct((nt, 1, tm), I32),
        compiler_params=_params(("arbitrary",)),
        name="plan",
    )(start, cls, rank)


ROW_UNROLL = 8


DISPATCH_SLOTS = 3


def _dispatch_kernel(dest_ref, h_ref, xs_in_ref, xs_ref, buf, load_sem, scat_sem, *, tm, nt):
    del xs_in_ref
    i = pl.program_id(0)

    def load(t, slot):
        rows = pl.ds(pl.multiple_of(t * tm, tm), tm)
        return pltpu.make_async_copy(h_ref.at[rows], buf.at[slot], load_sem.at[slot])

    def row_copy(slot, r, d):
        return pltpu.make_async_copy(buf.at[slot, pl.ds(r, 1)], xs_ref.at[pl.ds(d, 1)],
                                     scat_sem.at[slot])

    @pl.when(i == 0)
    def _():
        for t in range(min(DISPATCH_SLOTS - 1, nt)):
            load(t, t).start()

    for slot in range(DISPATCH_SLOTS):
        @pl.when((i < nt) & (lax.rem(i, DISPATCH_SLOTS) == slot))
        def _():
            load(i, slot).wait()
            for r in range(tm):
                row_copy(slot, r, dest_ref[0, r]).start(priority=r % 2)

    @pl.when(i > 0)
    def _():
        slot = lax.rem(i + DISPATCH_SLOTS - 1, DISPATCH_SLOTS)

        def drain(j, carry):
            for u in range(ROW_UNROLL):
                row_copy(slot, 0, 0).wait()
            return carry

        lax.fori_loop(0, tm // ROW_UNROLL, drain, 0)

    @pl.when(i + DISPATCH_SLOTS - 1 < nt)
    def _():
        t = i + DISPATCH_SLOTS - 1
        load(t, lax.rem(t, DISPATCH_SLOTS)).start()


def _dispatch(dest, h2e, xs_init):
    nt, _, tm = dest.shape
    width = h2e.shape[1]
    return pl.pallas_call(
        functools.partial(_dispatch_kernel, tm=tm, nt=nt),
        grid=(nt + 1,),
        in_specs=[
            pl.BlockSpec((None, 1, tm), lambda i: (jnp.minimum(i, nt - 1), 0, 0),
                         memory_space=pltpu.SMEM),
            pl.BlockSpec(memory_space=pl.ANY),
            pl.BlockSpec(memory_space=pl.ANY),
        ],
        out_specs=pl.BlockSpec(memory_space=pl.ANY),
        out_shape=jax.ShapeDtypeStruct(xs_init.shape, F32),
        scratch_shapes=[pltpu.VMEM((DISPATCH_SLOTS, tm, width), F32),
                        pltpu.SemaphoreType.DMA((DISPATCH_SLOTS,)),
                        pltpu.SemaphoreType.DMA((DISPATCH_SLOTS,))],
        input_output_aliases={2: 0},
        compiler_params=_params(("arbitrary",)),
        name="dispatch",
    )(dest, h2e, xs_init)


def _moe_kernel(elo_ref, ehi_ref, nact_ref, xs_ref, wg0, wu0, wd0, wg1, wu1, wd1, o_ref):
    j = pl.program_id(0)
    D = o_ref.shape[1]

    @pl.when(j < nact_ref[0])
    def _():
        x = xs_ref[:, 0:D].astype(BF16)
        w_lo = xs_ref[:, D:D + 1]
        w_hi = xs_ref[:, D + 1:D + 2]

        gates = [_dot(x, wg[...]) for wg in (wg0, wg1)]
        ups = [_dot(x, wu[...]) for wu in (wu0, wu1)]
        acts = [((g * jax.nn.sigmoid(g)) * u).astype(BF16) for g, u in zip(gates, ups)]
        y_lo, y_hi = [_dot(a, wd[...]) for a, wd in zip(acts, (wd0, wd1))]
        o_ref[...] = w_lo * y_lo + w_hi * y_hi

    @pl.when(j >= nact_ref[0])
    def _():
        o_ref[...] = jnp.zeros_like(o_ref)


def _moe(elo, ehi, nact, xs, w_gate, w_up, w_down, layer):
    ns, width = xs.shape
    D = width - WCOLS
    nb = ns // MOE_ROWS
    lo = lambda j, a, b, n: (layer, a[j], 0, 0)
    hi = lambda j, a, b, n: (layer, b[j], 0, 0)
    gu = (None, None, D, D_FF)
    dn = (None, None, D_FF, D)
    return pl.pallas_call(
        _moe_kernel,
        grid_spec=pltpu.PrefetchScalarGridSpec(
            num_scalar_prefetch=3,
            grid=(nb,),
            in_specs=[
                pl.BlockSpec((MOE_ROWS, width), lambda j, a, b, n: (jnp.minimum(j, n[0] - 1), 0)),
                pl.BlockSpec(gu, lo), pl.BlockSpec(gu, lo), pl.BlockSpec(dn, lo),
                pl.BlockSpec(gu, hi), pl.BlockSpec(gu, hi), pl.BlockSpec(dn, hi),
            ],
            out_specs=pl.BlockSpec((MOE_ROWS, D), lambda j, a, b, n: (j, 0)),
        ),
        out_shape=jax.ShapeDtypeStruct((ns, D), F32),
        compiler_params=_params(("arbitrary",), VMEM_LIMIT),
        name="moe",
    )(elo, ehi, nact, xs, w_gate, w_up, w_down, w_gate, w_up, w_down)


def _combine_kernel(dest_ref, x1_ref, g2_ref, ys_ref, o_ref, buf, sem, *, tm, nt):
    i = pl.program_id(0)

    def row_copy(slot, r, d):
        return pltpu.make_async_copy(ys_ref.at[pl.ds(d, 1)], buf.at[slot, pl.ds(r, 1)],
                                     sem.at[slot])

    for slot in range(2):
        @pl.when((i < nt) & (lax.rem(i, 2) == slot))
        def _():
            for r in range(tm):
                row_copy(slot, r, dest_ref[0, r]).start(priority=r % 2)

    @pl.when(i > 0)
    def _():
        slot = lax.rem(i + 1, 2)

        def drain(j, carry):
            for u in range(ROW_UNROLL):
                row_copy(slot, 0, 0).wait()
            return carry

        lax.fori_loop(0, tm // ROW_UNROLL, drain, 0)
        o_ref[...] = x1_ref[...] + g2_ref[...] * buf[slot]


def _combine(dest, x1, g2, ys, seq):
    nt, _, tm = dest.shape
    T, D = x1.shape
    per_b = seq // tm
    done = lambda i: jnp.maximum(i - 1, 0)
    return pl.pallas_call(
        functools.partial(_combine_kernel, tm=tm, nt=nt),
        grid=(nt + 1,),
        in_specs=[
            pl.BlockSpec((None, 1, tm), lambda i: (jnp.minimum(i, nt - 1), 0, 0),
                         memory_space=pltpu.SMEM),
            pl.BlockSpec((tm, D), lambda i: (done(i), 0)),
            pl.BlockSpec((None, 1, D), lambda i: (done(i) // per_b, 0, 0)),
            pl.BlockSpec(memory_space=pl.ANY),
        ],
        out_specs=pl.BlockSpec((tm, D), lambda i: (done(i), 0)),
        out_shape=jax.ShapeDtypeStruct((T, D), F32),
        scratch_shapes=[pltpu.VMEM((2, tm, D), F32), pltpu.SemaphoreType.DMA((2,))],
        compiler_params=_params(("arbitrary",)),
        name="combine",
    )(dest, x1, g2, ys)


def _block_plan(counts, nb):
    cnt = counts[:N_CLASSES, 0].astype(I32)
    nblk = (cnt + MOE_ROWS - 1) // MOE_ROWS
    cum = jnp.cumsum(nblk)
    start = ((cum - nblk) * MOE_ROWS).astype(I32)
    nact = cum[-1:].astype(I32)
    blk = jnp.arange(nb, dtype=I32)
    cls = jnp.sum(jnp.minimum(blk, nact[0] - 1)[:, None] >= cum[None, :], axis=1).astype(I32)
    cls = jnp.minimum(cls, N_CLASSES - 1)
    lo_tab = jnp.array([p[0] for p in PAIRS], I32)
    hi_tab = jnp.array([p[1] for p in PAIRS], I32)
    grp = cls // len(PAIRS)
    elo = grp * EXPERTS_PER_GROUP + lo_tab[cls % len(PAIRS)]
    ehi = grp * EXPERTS_PER_GROUP + hi_tab[cls % len(PAIRS)]
    start = jnp.concatenate([start, jnp.zeros((CLASS_ROWS - N_CLASSES,), I32)])
    return start, elo, ehi, nact


def kernel(x, c, positions, w_ada, b_ada, norm_mix, w_in, w_alpha2, b_alpha, gla_norm, q_norm,
           k_norm, sinks, w_up_gla, w_up_swa, w_out, norm_ffn, w_router, b_router, w_gate, w_up,
           w_down):
    B, S, D = x.shape
    L = w_ada.shape[0]
    T = B * S
    tm = min(512, S)

    mod = _modulation(c, w_ada, b_ada).reshape(L, B, 6, 1, D)
    cos_t, sg_t = _rope_tables(positions)

    sizes = (256, 256, 512, 16, 512, 512, 128, 128, 1024, 1024)
    offs = [0]
    for s in sizes:
        offs.append(offs[-1] + s)
    order = (0, 1, 2, 4, 5, 6, 7, 8, 9, 3)
    w_in16 = w_in.astype(BF16)
    w_in_p = jnp.concatenate(
        [w_in16[:, :, offs[i]:offs[i + 1]] for i in order]
        + [jnp.zeros((L, D, LANES - GLA_RANK), BF16)], axis=-1)
    wa2_p = jnp.concatenate(
        [w_alpha2, jnp.zeros((L, LANES - GLA_RANK, w_alpha2.shape[-1]), F32)], axis=1).astype(BF16)
    wr = jnp.concatenate(
        [w_router.astype(BF16), jnp.zeros((D, LANES - N_EXPERTS), BF16)], axis=1)
    br = b_router.reshape(N_EXPERTS, 1)
    wug = w_up_gla.astype(BF16)
    wus = w_up_swa.astype(BF16)
    wo = w_out.astype(BF16)
    wg = w_gate.astype(BF16)
    wu = w_up.astype(BF16)
    wd = w_down.astype(BF16)

    nb = T // MOE_ROWS + N_CLASSES
    xs = jnp.zeros((nb * MOE_ROWS, D + WCOLS), F32)
    xt = x.reshape(T, D)
    pending = None
    for l in range(L):
        sh1, sc1, g1, sh2, sc2, g2 = [mod[l, :, i] for i in range(6)]
        premix_args = (norm_mix[l].reshape(1, D), sc1, sh1, w_in_p[l], wa2_p[l],
                       b_alpha[l].reshape(1, -1))
        if pending is None:
            gq, gk, gv, la, rs, sq, skv, sa, sb = _premix(xt, *premix_args, S, tm)
        else:
            gq, gk, gv, la, rs, sq, skv, sa, sb, xt = _premix_combine(*pending, *premix_args, S)
        qg = jnp.tile(q_norm[l], LANES // SWA_HD).reshape(1, LANES)
        kg = jnp.tile(k_norm[l], LANES // SWA_HD).reshape(1, LANES)
        sinks_b = jnp.broadcast_to(sinks[l][:, None], (SWA_HEADS, LANES))
        y_gla = _gla(gq, gk, gv, la, rs, gla_norm[l].reshape(1, -1), B, S, min(GLA_TILE, S))
        y_swa = _swa(sq, skv, cos_t, sg_t, qg, kg, sinks_b, B, S, tm)
        x1, h2e, cls, rank, counts = _postmix(
            y_gla, y_swa, sa, sb, xt, g1, wug[l], wus[l], wo[l],
            norm_ffn[l].reshape(1, D), sc2, sh2, wr, br, S, tm)
        start, elo, ehi, nact = _block_plan(counts, nb)
        dest = _plan(start, cls, rank)
        xs = _dispatch(dest, h2e, xs)
        ys = _moe(elo, ehi, nact, xs, wg, wu, wd, l)
        pending = (dest, x1, g2, ys)
    xt = _combine(*pending, S)
    return xt.reshape(B, S, D)
```
